```python
import jax, jax.numpy as jnp
from jax import lax
import numpy as np

D_MODEL = 1024
BATCH = 4
SEQ = 8192
DEPTH = 1

SSM_WIDTH = D_MODEL // 2
SSM_GROUP = 16
SSM_GROUPS = SSM_WIDTH // SSM_GROUP
SSM_STATE = 64
NSA_HEADS = 8
NSA_KV_GROUPS = 2
HEADS_PER_GROUP = NSA_HEADS // NSA_KV_GROUPS
HEAD_DIM = 64
NSA_WIDTH = NSA_HEADS * HEAD_DIM
CMP_STRIDE = 16
CMP_BLOCK = 2 * CMP_STRIDE
CMP_HIDDEN = 2 * HEAD_DIM
SEL_BLOCK = 64
SEL_TOP_K = 16
WINDOW = 512
Q_BLOCK = 128
D_FF = 4 * D_MODEL
N_MIXERS = 2
KV_WIDTH = 3 * 2 * NSA_KV_GROUPS * HEAD_DIM
IN_WIDTH = SSM_WIDTH + NSA_WIDTH + KV_WIDTH + 3 * NSA_HEADS + N_MIXERS * D_MODEL
EPS = 1e-6
NEG_INF = -1e30
FORCE_BONUS = 1e4

kernel_name = "hybrid_s5_nsa_gated_block"


def rmsnorm(x, g):
    xf = x.astype(jnp.float32)
    y = xf * lax.rsqrt(jnp.mean(xf * xf, axis=-1, keepdims=True) + EPS)
    return (y * g.astype(jnp.float32)).astype(x.dtype)


def masked_softmax(s, mask):
    s = jnp.where(mask, s.astype(jnp.float32), NEG_INF)
    p = jax.nn.softmax(s, axis=-1)
    return jnp.where(mask, p, 0.0)


def alibi_slopes():
    h = jnp.arange(1, NSA_HEADS + 1, dtype=jnp.float32)
    return jnp.exp2(-8.0 * h / NSA_HEADS).reshape(NSA_KV_GROUPS, HEADS_PER_GROUP, 1, 1)


def s5_mixer(u, a_re, a_im, log_dt, b_re, b_im, c_re, c_im, d_skip, w_glu, b_glu):
    bsz, s, _ = u.shape
    f32 = jnp.float32
    ug = u.astype(f32).reshape(bsz, s, SSM_GROUPS, SSM_GROUP)
    ar = a_re.astype(f32)
    ai = a_im.astype(f32)
    dt = jnp.exp(log_dt.astype(f32))[:, None]
    mag = jnp.exp(ar * dt)
    abar_re = mag * jnp.cos(ai * dt)
    abar_im = mag * jnp.sin(ai * dt)
    den = ar * ar + ai * ai
    nr = abar_re - 1.0
    fr = (nr * ar + abar_im * ai) / den
    fi = (abar_im * ar - nr * ai) / den
    br = b_re.astype(f32)
    bi = b_im.astype(f32)
    bbar_re = fr[..., None] * br - fi[..., None] * bi
    bbar_im = fr[..., None] * bi + fi[..., None] * br
    bu_re = jnp.einsum('bsgc,gpc->bsgp', ug, bbar_re)
    bu_im = jnp.einsum('bsgc,gpc->bsgp', ug, bbar_im)
    a_seq_re = jnp.broadcast_to(abar_re, (1, s, SSM_GROUPS, SSM_STATE))
    a_seq_im = jnp.broadcast_to(abar_im, (1, s, SSM_GROUPS, SSM_STATE))

    def combine(left, right):
        a1r, a1i, b1r, b1i = left
        a2r, a2i, b2r, b2i = right
        return (a2r * a1r - a2i * a1i, a2r * a1i + a2i * a1r,
                a2r * b1r - a2i * b1i + b2r, a2r * b1i + a2i * b1r + b2i)

    _, _, xr, xi = lax.associative_scan(combine, (a_seq_re, a_seq_im, bu_re, bu_im), axis=1)
    y = (jnp.einsum('bsgp,gcp->bsgc', xr, c_re.astype(f32))
         - jnp.einsum('bsgp,gcp->bsgc', xi, c_im.astype(f32))
         + d_skip.astype(f32).reshape(SSM_GROUPS, SSM_GROUP) * ug).reshape(bsz, s, SSM_WIDTH)
    yg = jax.nn.gelu(y)
    out = yg * jax.nn.sigmoid(yg @ w_glu.astype(f32) + b_glu.astype(f32))
    return out.astype(u.dtype)


def compress_blocks(tok, pe, w1, w2):
    bsz, g, s, d = tok.shape
    chunks = tok.reshape(bsz, g, s // CMP_STRIDE, CMP_STRIDE, d)
    blocks = jnp.concatenate([chunks[:, :, :-1], chunks[:, :, 1:]], axis=3) + pe
    flat = blocks.reshape(bsz, g, s // CMP_STRIDE - 1, CMP_BLOCK * d)
    return jax.nn.gelu(flat @ w1) @ w2


def nsa_mixer(q, kv, gate_logits, cmp_pe_k, cmp_pe_v, cmp_wk1, cmp_wk2, cmp_wv1, cmp_wv2,
              q_norm_g, k_norm_g):
    bsz, s, _ = q.shape
    G, HG, d = NSA_KV_GROUPS, HEADS_PER_GROUP, HEAD_DIM
    f32 = jnp.float32
    q = rmsnorm(q.reshape(bsz, s, G, HG, d), q_norm_g).transpose(0, 2, 3, 1, 4)
    kv = kv.reshape(bsz, s, 3, 2, G, d).transpose(2, 3, 0, 4, 1, 5)
    kc = rmsnorm(compress_blocks(kv[0, 0], cmp_pe_k, cmp_wk1, cmp_wk2), k_norm_g[0])
    vc = compress_blocks(kv[0, 1], cmp_pe_v, cmp_wv1, cmp_wv2)
    n_cmp = kc.shape[2]
    n_sel = s // SEL_BLOCK
    k_top = min(SEL_TOP_K, n_sel)
    ks = rmsnorm(kv[1, 0], k_norm_g[1]).reshape(bsz, G, n_sel, SEL_BLOCK, d)
    vs = kv[1, 1].reshape(bsz, G, n_sel, SEL_BLOCK, d)
    pad = ((0, 0), (0, 0), (WINDOW, 0), (0, 0))
    kw = jnp.pad(rmsnorm(kv[2, 0], k_norm_g[2]), pad)
    vw = jnp.pad(kv[2, 1], pad)
    gates = jax.nn.sigmoid(gate_logits.reshape(bsz, s, 3, G, HG).astype(f32))
    gates = gates.astype(q.dtype).transpose(0, 3, 4, 1, 2)
    slopes = alibi_slopes()
    scale = HEAD_DIM ** -0.5
    c_end = (jnp.arange(n_cmp) * CMP_STRIDE + CMP_BLOCK - 1).astype(f32)
    blk_ids = jnp.arange(n_sel)
    bi = jnp.arange(bsz)[:, None, None, None]
    gi = jnp.arange(G)[None, :, None, None]
    sel_off = jnp.arange(SEL_BLOCK)
    win_off = jnp.arange(Q_BLOCK + WINDOW)

    def query_block(i):
        qs = i * Q_BLOCK
        qb = lax.dynamic_slice_in_dim(q, qs, Q_BLOCK, axis=3)
        gb = lax.dynamic_slice_in_dim(gates, qs, Q_BLOCK, axis=3)
        t = qs + jnp.arange(Q_BLOCK)
        tf = t.astype(f32)
        dist_c = tf[:, None] - c_end[None, :]
        sc = jnp.einsum('bghqd,bgcd->bghqc', qb, kc).astype(f32) * scale - slopes * dist_c
        pc = masked_softmax(sc, dist_c >= 0)
        o_cmp = jnp.einsum('bghqc,bgcd->bghqd', pc.astype(vc.dtype), vc)
        imp = jnp.pad(pc.sum(axis=2), ((0, 0), (0, 0), (0, 0), (0, 1)))
        chunks = imp.reshape(bsz, G, Q_BLOCK, n_sel, SEL_BLOCK // CMP_STRIDE)
        prev = jnp.pad(chunks[..., :-1, -1], ((0, 0), (0, 0), (0, 0), (1, 0)))
        imp_sel = chunks.sum(axis=-1) + prev
        cur = t // SEL_BLOCK
        valid = blk_ids[None, :] <= cur[:, None]
        forced = ((blk_ids[None, :] == 0) | (blk_ids[None, :] == cur[:, None])
                  | (blk_ids[None, :] == cur[:, None] - 1))
        imp_sel = jnp.where(valid, imp_sel + jnp.where(forced, FORCE_BONUS, 0.0), NEG_INF)
        _, idx = lax.top_k(imp_sel, k_top)
        k_g = ks[bi, gi, idx].reshape(bsz, G, Q_BLOCK, k_top * SEL_BLOCK, d)
        v_g = vs[bi, gi, idx].reshape(bsz, G, Q_BLOCK, k_top * SEL_BLOCK, d)
        pos = (idx[..., None] * SEL_BLOCK + sel_off).reshape(bsz, G, Q_BLOCK, k_top * SEL_BLOCK)
        dist_s = (t[None, None, :, None] - pos).astype(f32)[:, :, None]
        ss = jnp.einsum('bghqd,bgqkd->bghqk', qb, k_g).astype(f32) * scale - slopes * dist_s
        ps = masked_softmax(ss, dist_s >= 0)
        o_slc = jnp.einsum('bghqk,bgqkd->bghqd', ps.astype(v_g.dtype), v_g)
        kwb = lax.dynamic_slice_in_dim(kw, qs, Q_BLOCK + WINDOW, axis=2)
        vwb = lax.dynamic_slice_in_dim(vw, qs, Q_BLOCK + WINDOW, axis=2)
        s_pos = qs - WINDOW + win_off
        dist_w = t[:, None] - s_pos[None, :]
        mask_w = (dist_w >= 0) & (dist_w < WINDOW) & (s_pos[None, :] >= 0)
        sw = jnp.einsum('bghqd,bgkd->bghqk', qb, kwb).astype(f32) * scale - slopes * dist_w.astype(f32)
        pw = masked_softmax(sw, mask_w)
        o_win = jnp.einsum('bghqk,bgkd->bghqd', pw.astype(vwb.dtype), vwb)
        return gb[..., 0:1] * o_cmp + gb[..., 1:2] * o_slc + gb[..., 2:3] * o_win

    out = lax.map(query_block, jnp.arange(s // Q_BLOCK))
    return out.transpose(1, 0, 4, 2, 3, 5).reshape(bsz, s, NSA_WIDTH)


def setup_inputs(seed: int = 0) -> dict:
    key = jax.random.key(seed)
    ks = jax.random.split(key, 32)
    L = DEPTH
    f32 = jnp.float32

    def nrm(k, shape, scale):
        return jax.random.normal(k, shape, f32) * scale

    return {
        'x': nrm(ks[0], (BATCH, SEQ, D_MODEL), 1.0),
        'norm1_g': 1.0 + nrm(ks[1], (L, D_MODEL), 0.02),
        'w_in': nrm(ks[2], (L, D_MODEL, IN_WIDTH), D_MODEL ** -0.5),
        'ssm_a_re': -0.5 + nrm(ks[3], (L, SSM_GROUPS, SSM_STATE), 0.01),
        'ssm_a_im': np.pi * jnp.arange(SSM_STATE, dtype=f32) + nrm(ks[4], (L, SSM_GROUPS, SSM_STATE), 0.01),
        'ssm_log_dt': jax.random.uniform(ks[5], (L, SSM_GROUPS), f32, np.log(1e-3), np.log(1e-1)),
        'ssm_b_re': nrm(ks[6], (L, SSM_GROUPS, SSM_STATE, SSM_GROUP), (2 * SSM_GROUP) ** -0.5),
        'ssm_b_im': nrm(ks[7], (L, SSM_GROUPS, SSM_STATE, SSM_GROUP), (2 * SSM_GROUP) ** -0.5),
        'ssm_c_re': nrm(ks[8], (L, SSM_GROUPS, SSM_GROUP, SSM_STATE), SSM_STATE ** -0.5),
        'ssm_c_im': nrm(ks[9], (L, SSM_GROUPS, SSM_GROUP, SSM_STATE), SSM_STATE ** -0.5),
        'ssm_d': nrm(ks[10], (L, SSM_WIDTH), 1.0),
        'ssm_w_glu': nrm(ks[11], (L, SSM_WIDTH, SSM_WIDTH), SSM_WIDTH ** -0.5),
        'ssm_b_glu': nrm(ks[12], (L, SSM_WIDTH), 0.01),
        'cmp_pe_k': nrm(ks[13], (L, CMP_BLOCK, HEAD_DIM), 0.1),
        'cmp_pe_v': nrm(ks[14], (L, CMP_BLOCK, HEAD_DIM), 0.1),
        'cmp_wk1': nrm(ks[15], (L, CMP_BLOCK * HEAD_DIM, CMP_HIDDEN), (CMP_BLOCK * HEAD_DIM) ** -0.5),
        'cmp_wk2': nrm(ks[16], (L, CMP_HIDDEN, HEAD_DIM), CMP_HIDDEN ** -0.5),
        'cmp_wv1': nrm(ks[17], (L, CMP_BLOCK * HEAD_DIM, CMP_HIDDEN), (CMP_BLOCK * HEAD_DIM) ** -0.5),
        'cmp_wv2': nrm(ks[18], (L, CMP_HIDDEN, HEAD_DIM), CMP_HIDDEN ** -0.5),
        'q_norm_g': 1.0 + nrm(ks[19], (L, HEAD_DIM), 0.02),
        'k_norm_g': 1.0 + nrm(ks[20], (L, 3, HEAD_DIM), 0.02),
        'w_proj_ssm': nrm(ks[21], (L, SSM_WIDTH, D_MODEL), SSM_WIDTH ** -0.5),
        'w_proj_nsa': nrm(ks[22], (L, NSA_WIDTH, D_MODEL), NSA_WIDTH ** -0.5),
        'w_out': nrm(ks[23], (L, D_MODEL, D_MODEL), D_MODEL ** -0.5),
        'norm2_g': 1.0 + nrm(ks[24], (L, D_MODEL), 0.02),
        'w_up': nrm(ks[25], (L, D_MODEL, D_FF), D_MODEL ** -0.5),
        'w_down': nrm(ks[26], (L, D_FF, D_MODEL), D_FF ** -0.5),
    }


def reference(x, norm1_g, w_in, ssm_a_re, ssm_a_im, ssm_log_dt, ssm_b_re, ssm_b_im, ssm_c_re, ssm_c_im,
              ssm_d, ssm_w_glu, ssm_b_glu, cmp_pe_k, cmp_pe_v, cmp_wk1, cmp_wk2, cmp_wv1, cmp_wv2,
              q_norm_g, k_norm_g, w_proj_ssm, w_proj_nsa, w_out, norm2_g, w_up, w_down):
    bsz, s, _ = x.shape
    o1 = SSM_WIDTH
    o2 = o1 + NSA_WIDTH
    o3 = o2 + KV_WIDTH
    o4 = o3 + 3 * NSA_HEADS
    for l in range(DEPTH):
        h = rmsnorm(x, norm1_g[l])
        hp = h @ w_in[l]
        y_ssm = s5_mixer(hp[..., :o1], ssm_a_re[l], ssm_a_im[l], ssm_log_dt[l], ssm_b_re[l], ssm_b_im[l],
                         ssm_c_re[l], ssm_c_im[l], ssm_d[l], ssm_w_glu[l], ssm_b_glu[l])
        y_nsa = nsa_mixer(hp[..., o1:o2], hp[..., o2:o3], hp[..., o3:o4], cmp_pe_k[l], cmp_pe_v[l],
                          cmp_wk1[l], cmp_wk2[l], cmp_wv1[l], cmp_wv2[l], q_norm_g[l], k_norm_g[l])
        gate = jax.nn.sigmoid(hp[..., o4:].reshape(bsz, s, N_MIXERS, D_MODEL))
        merged = gate[..., 0, :] * (y_ssm @ w_proj_ssm[l]) + gate[..., 1, :] * (y_nsa @ w_proj_nsa[l])
        x = x + merged @ w_out[l]
        h2 = rmsnorm(x, norm2_g[l])
        x = x + jnp.square(jax.nn.relu(h2 @ w_up[l])) @ w_down[l]
    return x
```

```python
import functools

import jax
import jax.numpy as jnp
from jax import lax
from jax.experimental import pallas as pl
from jax.experimental.pallas import tpu as pltpu

D_MODEL = 1024
SSM_WIDTH = D_MODEL // 2
SSM_GROUP = 16
SSM_GROUPS = SSM_WIDTH // SSM_GROUP
SSM_STATE = 64
SSM_STATES = SSM_GROUPS * SSM_STATE
NSA_HEADS = 8
NSA_KV_GROUPS = 2
HEADS_PER_GROUP = NSA_HEADS // NSA_KV_GROUPS
HEAD_DIM = 64
NSA_WIDTH = NSA_HEADS * HEAD_DIM
CMP_STRIDE = 16
CMP_BLOCK = 2 * CMP_STRIDE
CMP_HIDDEN = 2 * HEAD_DIM
SEL_BLOCK = 64
SEL_TOP_K = 16
WINDOW = 512
D_FF = 4 * D_MODEL
N_MIXERS = 2
KV_WIDTH = 3 * 2 * NSA_KV_GROUPS * HEAD_DIM
EPS = 1e-6
NEG_INF = -1e30
FORCE_BONUS = 1e4

LANES = 128
VMEM_LIMIT_BYTES = 56 * 1024 * 1024

GATE_ROWS = 16

F32 = jnp.float32
BF16 = jnp.bfloat16


def _dot(a, b):
    return jnp.dot(a, b, preferred_element_type=F32)


def _dot_nt(a, b):
    return lax.dot_general(a, b, (((1,), (1,)), ((), ())), preferred_element_type=F32)


def _dot_tn(a, b):
    return lax.dot_general(a, b, (((0,), (0,)), ((), ())), preferred_element_type=F32)


def _rms_rows(x, g):
    ms = jnp.mean(x * x, axis=-1, keepdims=True)
    return x * lax.rsqrt(ms + EPS) * g


def _inproj_kernel(x_ref, g_ref, wa_ref, wbt_ref, u_ref, cp_ref, kp_ref, qt_ref, vt_ref, glt_ref):
    h = _rms_rows(x_ref[...], g_ref[...]).astype(BF16)
    a = _dot(h, wa_ref[...])
    u_ref[...] = a[:, :SSM_WIDTH]
    cp_ref[...] = a[:, SSM_WIDTH:SSM_WIDTH + 4 * HEAD_DIM]
    kp_ref[...] = a[:, SSM_WIDTH + 4 * HEAD_DIM:]
    bt = _dot_nt(wbt_ref[...], h)
    qt_ref[...] = bt[:NSA_WIDTH]
    vt_ref[...] = bt[NSA_WIDTH:NSA_WIDTH + 4 * HEAD_DIM]
    glt_ref[...] = bt[NSA_WIDTH + 4 * HEAD_DIM:]


def _in_proj(x2, g1, wa, wbt, *, tm):
    n = x2.shape[0]
    nb = wbt.shape[0]
    const = lambda i: (0, 0)
    return pl.pallas_call(
        _inproj_kernel,
        grid=(n // tm,),
        in_specs=[
            pl.BlockSpec((tm, D_MODEL), lambda i: (i, 0)),
            pl.BlockSpec((1, D_MODEL), const),
            pl.BlockSpec(wa.shape, const),
            pl.BlockSpec(wbt.shape, const),
        ],
        out_specs=[
            pl.BlockSpec((tm, SSM_WIDTH), lambda i: (i, 0)),
            pl.BlockSpec((tm, 4 * HEAD_DIM), lambda i: (i, 0)),
            pl.BlockSpec((tm, 4 * HEAD_DIM), lambda i: (i, 0)),
            pl.BlockSpec((NSA_WIDTH, tm), lambda i: (0, i)),
            pl.BlockSpec((4 * HEAD_DIM, tm), lambda i: (0, i)),
            pl.BlockSpec((NSA_KV_GROUPS * GATE_ROWS, tm), lambda i: (0, i)),
        ],
        out_shape=[
            jax.ShapeDtypeStruct((n, SSM_WIDTH), F32),
            jax.ShapeDtypeStruct((n, 4 * HEAD_DIM), F32),
            jax.ShapeDtypeStruct((n, 4 * HEAD_DIM), F32),
            jax.ShapeDtypeStruct((NSA_WIDTH, n), F32),
            jax.ShapeDtypeStruct((4 * HEAD_DIM, n), F32),
            jax.ShapeDtypeStruct((NSA_KV_GROUPS * GATE_ROWS, n), F32),
        ],
        compiler_params=pltpu.CompilerParams(
            dimension_semantics=("arbitrary",), vmem_limit_bytes=VMEM_LIMIT_BYTES),
        name="in_proj",
    )(x2, g1, wa, wbt)


SCAN_SLAB = 512
SCAN_UNROLL = 8


def _s5_kernel(u_ref, bblk_ref, cblk_ref, abar_ref, d_ref, wglu_ref, bglu_ref, y_ref, bx_s, carry_s, *, tc):
    @pl.when(pl.program_id(1) == 0)
    def _reset():
        carry_s[...] = jnp.zeros_like(carry_s)

    u = u_ref[...]
    bx_s[...] = _dot(u.astype(BF16), bblk_ref[...])

    for slab in range(SSM_STATES // SCAN_SLAB):
        re_cols = pl.ds(slab * SCAN_SLAB, SCAN_SLAB)
        im_cols = pl.ds(SSM_STATES + slab * SCAN_SLAB, SCAN_SLAB)
        a_re = abar_ref[0:1, re_cols]
        a_im = abar_ref[1:2, re_cols]

        def step(t8, carry):
            xr, xi = carry
            for k in range(SCAN_UNROLL):
                row = pl.ds(t8 * SCAN_UNROLL + k, 1)
                nr = a_re * xr - a_im * xi + bx_s[row, re_cols]
                ni = a_re * xi + a_im * xr + bx_s[row, im_cols]
                bx_s[row, re_cols] = nr
                bx_s[row, im_cols] = ni
                xr, xi = nr, ni
            return xr, xi

        xr, xi = lax.fori_loop(0, tc // SCAN_UNROLL, step, (carry_s[0:1, re_cols], carry_s[0:1, im_cols]))
        carry_s[0:1, re_cols] = xr
        carry_s[0:1, im_cols] = xi

    y = _dot(bx_s[...].astype(BF16), cblk_ref[...]) + d_ref[...] * u
    yg = jax.nn.gelu(y)
    z = _dot(yg.astype(BF16), wglu_ref[...]) + bglu_ref[...]
    y_ref[...] = (yg * jax.nn.sigmoid(z)).astype(y_ref.dtype)


def _s5(u, bblk, cblk, abar, d_row, wglu, bglu, *, batch, seq, tc):
    nchunk = seq // tc
    const = lambda b, c: (0, 0)
    return pl.pallas_call(
        functools.partial(_s5_kernel, tc=tc),
        grid=(batch, nchunk),
        in_specs=[
            pl.BlockSpec((tc, SSM_WIDTH), lambda b, c: (b * nchunk + c, 0)),
            pl.BlockSpec(bblk.shape, const),
            pl.BlockSpec(cblk.shape, const),
            pl.BlockSpec(abar.shape, const),
            pl.BlockSpec(d_row.shape, const),
            pl.BlockSpec(wglu.shape, const),
            pl.BlockSpec(bglu.shape, const),
        ],
        out_specs=pl.BlockSpec((tc, SSM_WIDTH), lambda b, c: (b * nchunk + c, 0)),
        out_shape=jax.ShapeDtypeStruct((batch * seq, SSM_WIDTH), BF16),
        scratch_shapes=[
            pltpu.VMEM((tc, 2 * SSM_STATES), F32),
            pltpu.VMEM((8, 2 * SSM_STATES), F32),
        ],
        compiler_params=pltpu.CompilerParams(
            dimension_semantics=("arbitrary", "arbitrary"), vmem_limit_bytes=VMEM_LIMIT_BYTES),
        name="s5",
    )(u, bblk, cblk, abar, d_row, wglu, bglu)


KEY_TILE = 128


def _nsa_kernel(qt_ref, cp_ref, kp_ref, vt_ref, glt_ref, slope_ref, qg_ref, kg_ref, pe_ref,
                w1_ref, w2k_ref, w2vt_ref, out_ref,
                ks_s, kw_s, kc_s, vct_s, vst_s, vwt_s, sval_s, bias_s, *, seq, tq):
    i = pl.program_id(2)
    n_cmp = seq // CMP_STRIDE
    n_sel = seq // SEL_BLOCK
    k_top = min(SEL_TOP_K, n_sel)
    hq = HEADS_PER_GROUP * tq

    @pl.when(i == 0)
    def _prepare_keys_and_values():
        kg = kg_ref[...]
        ks_s[...] = _rms_rows(kp_ref[:, 0:HEAD_DIM], kg[1:2]).astype(BF16)
        kw_s[...] = _rms_rows(kp_ref[:, HEAD_DIM:2 * HEAD_DIM], kg[2:3]).astype(BF16)
        vst_s[...] = vt_ref[0:HEAD_DIM, :].astype(BF16)
        vwt_s[...] = vt_ref[HEAD_DIM:2 * HEAD_DIM, :].astype(BF16)

        lo = jnp.zeros((n_cmp, 2 * CMP_HIDDEN), F32)
        hi = jnp.zeros((n_cmp, 2 * CMP_HIDDEN), F32)
        for tt in range(CMP_STRIDE):
            tok = cp_ref[pl.ds(tt, n_cmp, stride=CMP_STRIDE), :]
            lo = lo + _dot((tok + pe_ref[tt:tt + 1, :]).astype(BF16), w1_ref[tt])
            hi = hi + _dot((tok + pe_ref[CMP_STRIDE + tt:CMP_STRIDE + tt + 1, :]).astype(BF16),
                           w1_ref[CMP_STRIDE + tt])
        hidden = jax.nn.gelu(lo + pltpu.roll(hi, n_cmp - 1, 0)).astype(BF16)
        kc_s[...] = _rms_rows(_dot(hidden[:, :CMP_HIDDEN], w2k_ref[...]), kg[0:1]).astype(BF16)
        vct_s[...] = _dot_nt(w2vt_ref[...], hidden[:, CMP_HIDDEN:]).astype(BF16)

    qs = i * tq
    lane = lax.broadcasted_iota(jnp.int32, (1, hq), 1)
    t_row = qs + (lane & (tq - 1))
    slope_row = slope_ref[0, 0:1, :]

    qg = qg_ref[...]
    heads = []
    for h in range(HEADS_PER_GROUP):
        qh = qt_ref[h * HEAD_DIM:(h + 1) * HEAD_DIM, :]
        ms = jnp.mean(qh * qh, axis=0, keepdims=True)
        heads.append((qh * lax.rsqrt(ms + EPS) * qg * (HEAD_DIM ** -0.5)).astype(BF16))
    qst = jnp.concatenate(heads, axis=1)

    sc = _dot(kc_s[...], qst)
    jrow = lax.broadcasted_iota(jnp.int32, (n_cmp, hq), 0)
    dist_c = t_row - (jrow * CMP_STRIDE + (CMP_BLOCK - 1))
    mask_c = dist_c >= 0
    sc = jnp.where(mask_c, sc - slope_row * dist_c.astype(F32), NEG_INF)
    pc = jnp.exp(sc - jnp.max(sc, axis=0, keepdims=True))
    pc = jnp.where(mask_c, pc * (1.0 / jnp.sum(pc, axis=0, keepdims=True)), 0.0)
    o_cmp = _dot(vct_s[...], pc.astype(BF16))

    imp = pc[:, 0:tq]
    for h in range(1, HEADS_PER_GROUP):
        imp = imp + pc[:, h * tq:(h + 1) * tq]
    nsel_r = lax.broadcasted_iota(jnp.int32, (n_sel, n_cmp), 0)
    jcmp_c = lax.broadcasted_iota(jnp.int32, (n_sel, n_cmp), 1)
    per_block = SEL_BLOCK // CMP_STRIDE
    gather = jnp.where((jcmp_c >= per_block * nsel_r - 1) & (jcmp_c <= per_block * nsel_r + per_block - 1), 1.0, 0.0)
    imp_sel = jnp.dot(gather, imp, preferred_element_type=F32, precision=lax.Precision.HIGHEST)
    blk = lax.broadcasted_iota(jnp.int32, (n_sel, tq), 0)
    cur = jnp.right_shift(t_row[:, 0:tq], SEL_BLOCK.bit_length() - 1)
    forced = (blk == 0) | (blk == cur) | (blk == cur - 1)
    sval = jnp.where(blk <= cur, imp_sel + jnp.where(forced, FORCE_BONUS, 0.0), NEG_INF)
    sval_s[...] = sval

    def rank_step(m, rank):
        row = sval_s[pl.ds(m, 1), :]
        ahead = jnp.where(row > sval, 1.0, jnp.where(row == sval, jnp.where(blk > m, 1.0, 0.0), 0.0))
        return rank + ahead

    n_live = jnp.minimum(n_sel, (qs + tq) // SEL_BLOCK)
    rank = lax.fori_loop(0, n_live, rank_step, jnp.zeros((n_sel, tq), F32))
    bias = jnp.where(rank < k_top, 0.0, NEG_INF)
    bias_s[...] = jnp.concatenate([bias] * HEADS_PER_GROUP, axis=1)

    sub = lax.broadcasted_iota(jnp.int32, (KEY_TILE, hq), 0)

    def attend(k_s, vt_s, kt_lo, kt_hi, selected):
        def body(kt, carry):
            m, l, acc = carry
            k0 = pl.multiple_of(kt * KEY_TILE, KEY_TILE)
            s = _dot(k_s[pl.ds(k0, KEY_TILE), :], qst)
            dist = t_row - (k0 + sub)
            s = s - slope_row * dist.astype(F32)
            if selected:
                b0 = bias_s[pl.ds(2 * kt, 1), :]
                b1 = bias_s[pl.ds(2 * kt + 1, 1), :]
                s = s + jnp.where(sub < SEL_BLOCK, b0, b1)
                ok = dist >= 0
            else:
                ok = (dist >= 0) & (dist < WINDOW)
            s = jnp.where(ok, s, NEG_INF)
            m_new = jnp.maximum(m, jnp.max(s, axis=0, keepdims=True))
            alpha = jnp.exp(m - m_new)
            p = jnp.exp(s - m_new)
            l = alpha * l + jnp.sum(p, axis=0, keepdims=True)
            acc = alpha * acc + _dot(vt_s[:, pl.ds(k0, KEY_TILE)], p.astype(BF16))
            return m_new, l, acc

        init = (jnp.full((1, hq), NEG_INF, F32), jnp.zeros((1, hq), F32), jnp.zeros((HEAD_DIM, hq), F32))
        _, l, acc = lax.fori_loop(kt_lo, kt_hi, body, init)
        return acc * (1.0 / l)

    n_kt = (qs + tq) // KEY_TILE
    o_sel = attend(ks_s, vst_s, 0, n_kt, True)
    o_win = attend(kw_s, vwt_s, jnp.maximum(0, n_kt - 1 - WINDOW // KEY_TILE), n_kt, False)

    gl = glt_ref[...]

    def gate_row(branch):
        rows = [jax.nn.sigmoid(gl[branch * HEADS_PER_GROUP + h:branch * HEADS_PER_GROUP + h + 1, :])
                for h in range(HEADS_PER_GROUP)]
        return jnp.concatenate(rows, axis=1)

    ot = gate_row(0) * o_cmp + gate_row(1) * o_sel + gate_row(2) * o_win
    out_ref[...] = jnp.concatenate(
        [ot[:, h * tq:(h + 1) * tq] for h in range(HEADS_PER_GROUP)], axis=0).astype(out_ref.dtype)


def _nsa(qt, cp, kp, vt, glt, slopes, qg, kg, pe, w1, w2k, w2vt, *, batch, seq, tq):
    nq = seq // tq
    n_cmp = seq // CMP_STRIDE
    n_sel = seq // SEL_BLOCK
    hq = HEADS_PER_GROUP * tq
    const2 = lambda b, g, i: (0, 0)
    const3 = lambda b, g, i: (0, 0, 0)
    return pl.pallas_call(
        functools.partial(_nsa_kernel, seq=seq, tq=tq),
        grid=(batch, NSA_KV_GROUPS, nq),
        in_specs=[
            pl.BlockSpec((HEADS_PER_GROUP * HEAD_DIM, tq), lambda b, g, i: (g, b * nq + i)),
            pl.BlockSpec((seq, 2 * HEAD_DIM), lambda b, g, i: (b, g)),
            pl.BlockSpec((seq, 2 * HEAD_DIM), lambda b, g, i: (b, g)),
            pl.BlockSpec((2 * HEAD_DIM, seq), lambda b, g, i: (g, b)),
            pl.BlockSpec((GATE_ROWS, tq), lambda b, g, i: (g, b * nq + i)),
            pl.BlockSpec((1, 8, hq), lambda b, g, i: (g, 0, 0)),
            pl.BlockSpec(qg.shape, const2),
            pl.BlockSpec(kg.shape, const2),
            pl.BlockSpec(pe.shape, const2),
            pl.BlockSpec(w1.shape, const3),
            pl.BlockSpec(w2k.shape, const2),
            pl.BlockSpec(w2vt.shape, const2),
        ],
        out_specs=pl.BlockSpec((HEADS_PER_GROUP * HEAD_DIM, tq), lambda b, g, i: (g, b * nq + i)),
        out_shape=jax.ShapeDtypeStruct((NSA_WIDTH, batch * seq), BF16),
        scratch_shapes=[
            pltpu.VMEM((seq, HEAD_DIM), BF16),
            pltpu.VMEM((seq, HEAD_DIM), BF16),
            pltpu.VMEM((n_cmp, HEAD_DIM), BF16),
            pltpu.VMEM((HEAD_DIM, n_cmp), BF16),
            pltpu.VMEM((HEAD_DIM, seq), BF16),
            pltpu.VMEM((HEAD_DIM, seq), BF16),
            pltpu.VMEM((n_sel, tq), F32),
            pltpu.VMEM((n_sel, hq), F32),
        ],
        compiler_params=pltpu.CompilerParams(
            dimension_semantics=("arbitrary", "arbitrary", "arbitrary"), vmem_limit_bytes=VMEM_LIMIT_BYTES),
        name="nsa",
    )(qt, cp, kp, vt, glt, slopes, qg, kg, pe, w1, w2k, w2vt)


def _final_kernel(x_ref, ys_ref, ynt_ref, g1_ref, wmg_ref, wps_ref, wpn_ref, wout_ref, g2_ref, wup_ref, wdn_ref,
                  o_ref):
    x = x_ref[...]
    h = _rms_rows(x, g1_ref[...]).astype(BF16)
    mg = _dot(h, wmg_ref[...])
    ps = _dot(ys_ref[...], wps_ref[...])
    pn = _dot_tn(ynt_ref[...], wpn_ref[...])
    merged = jax.nn.sigmoid(mg[:, :D_MODEL]) * ps + jax.nn.sigmoid(mg[:, D_MODEL:]) * pn
    x1 = x + _dot(merged.astype(BF16), wout_ref[...])
    h2 = _rms_rows(x1, g2_ref[...]).astype(BF16)
    a = jnp.maximum(_dot(h2, wup_ref[...]), 0.0)
    o_ref[...] = x1 + _dot((a * a).astype(BF16), wdn_ref[...])


def _final(x2, ys, ynt, g1, wmg, wps, wpn, wout, g2, wup, wdn, *, tm):
    n = x2.shape[0]
    const = lambda i: (0, 0)
    resident = lambda w: pl.BlockSpec(w.shape, const, pipeline_mode=pl.Buffered(1))
    return pl.pallas_call(
        _final_kernel,
        grid=(n // tm,),
        in_specs=[
            pl.BlockSpec((tm, D_MODEL), lambda i: (i, 0)),
            pl.BlockSpec((tm, SSM_WIDTH), lambda i: (i, 0)),
            pl.BlockSpec((NSA_WIDTH, tm), lambda i: (0, i)),
            resident(g1), resident(wmg), resident(wps), resident(wpn), resident(wout),
            resident(g2), resident(wup), resident(wdn),
        ],
        out_specs=pl.BlockSpec((tm, D_MODEL), lambda i: (i, 0)),
        out_shape=jax.ShapeDtypeStruct((n, D_MODEL), F32),
        compiler_params=pltpu.CompilerParams(
            dimension_semantics=("arbitrary",), vmem_limit_bytes=VMEM_LIMIT_BYTES),
        name="final",
    )(x2, ys, ynt, g1, wmg, wps, wpn, wout, g2, wup, wdn)


def _pack_in_proj(w_in):
    o1 = SSM_WIDTH
    o2 = o1 + NSA_WIDTH
    o3 = o2 + KV_WIDTH
    o4 = o3 + 3 * NSA_HEADS
    w_u, w_q, w_kv, w_gl = w_in[:, :o1], w_in[:, o1:o2], w_in[:, o2:o3], w_in[:, o3:o4]
    w_mg = w_in[:, o4:]
    kv = w_kv.reshape(D_MODEL, 3, 2, NSA_KV_GROUPS, HEAD_DIM)
    groups = range(NSA_KV_GROUPS)
    cpack = [jnp.concatenate([kv[:, 0, 0, g], kv[:, 0, 1, g]], axis=1) for g in groups]
    kpack = [jnp.concatenate([kv[:, 1, 0, g], kv[:, 2, 0, g]], axis=1) for g in groups]
    wa = jnp.concatenate([w_u] + cpack + kpack, axis=1)
    vrows = [jnp.concatenate([kv[:, 1, 1, g], kv[:, 2, 1, g]], axis=1) for g in range(NSA_KV_GROUPS)]
    gl = w_gl.reshape(D_MODEL, 3, NSA_KV_GROUPS, HEADS_PER_GROUP)
    pad = jnp.zeros((D_MODEL, GATE_ROWS - 3 * HEADS_PER_GROUP), w_in.dtype)
    glrows = [jnp.concatenate([gl[:, 0, g], gl[:, 1, g], gl[:, 2, g], pad], axis=1) for g in range(NSA_KV_GROUPS)]
    wbt = jnp.concatenate([w_q] + vrows + glrows, axis=1).T
    return wa.astype(BF16), wbt.astype(BF16), w_mg.astype(BF16)


def _pack_compress(pe_k, pe_v, w1k, w1v):
    pe = jnp.concatenate([pe_k, pe_v], axis=1)
    w1k = w1k.reshape(CMP_BLOCK, HEAD_DIM, CMP_HIDDEN)
    w1v = w1v.reshape(CMP_BLOCK, HEAD_DIM, CMP_HIDDEN)
    zero = jnp.zeros_like(w1k)
    w1 = jnp.concatenate([jnp.concatenate([w1k, zero], axis=2), jnp.concatenate([zero, w1v], axis=2)], axis=1)
    return pe, w1.astype(BF16)


def _pack_s5(a_re, a_im, log_dt, b_re, b_im, c_re, c_im):
    dt = jnp.exp(log_dt)[:, None]
    mag = jnp.exp(a_re * dt)
    abar_re = mag * jnp.cos(a_im * dt)
    abar_im = mag * jnp.sin(a_im * dt)
    den = a_re * a_re + a_im * a_im
    nr = abar_re - 1.0
    fr = (nr * a_re + abar_im * a_im) / den
    fi = (abar_im * a_re - nr * a_im) / den
    bbar_re = fr[..., None] * b_re - fi[..., None] * b_im
    bbar_im = fr[..., None] * b_im + fi[..., None] * b_re
    eye = jnp.eye(SSM_GROUPS, dtype=F32)

    def in_blocks(m):
        return (m.transpose(0, 2, 1)[:, :, None, :] * eye[:, None, :, None]).reshape(SSM_WIDTH, SSM_STATES)

    def out_blocks(m):
        return (m.transpose(0, 2, 1)[:, :, None, :] * eye[:, None, :, None]).reshape(SSM_STATES, SSM_WIDTH)

    bblk = jnp.concatenate([in_blocks(bbar_re), in_blocks(bbar_im)], axis=1).astype(BF16)
    cblk = jnp.concatenate([out_blocks(c_re), -out_blocks(c_im)], axis=0).astype(BF16)
    abar = jnp.zeros((8, SSM_STATES), F32)
    abar = abar.at[0].set(abar_re.reshape(-1)).at[1].set(abar_im.reshape(-1))
    return bblk, cblk, abar


def kernel(x, norm1_g, w_in, ssm_a_re, ssm_a_im, ssm_log_dt, ssm_b_re, ssm_b_im, ssm_c_re, ssm_c_im, ssm_d, ssm_w_glu, ssm_b_glu, cmp_pe_k, cmp_pe_v, cmp_wk1, cmp_wk2, cmp_wv1, cmp_wv2, q_norm_g, k_norm_g, w_proj_ssm, w_proj_nsa, w_out, norm2_g, w_up, w_down):
    batch, seq, d_model = x.shape
    assert d_model == D_MODEL and seq % 512 == 0
    n = batch * seq
    tq = 128
    depth = w_in.shape[0]
    head = jnp.arange(1, NSA_HEADS + 1, dtype=F32)
    slopes = jnp.exp2(-8.0 * head / NSA_HEADS).reshape(NSA_KV_GROUPS, 1, HEADS_PER_GROUP, 1)
    slopes = jnp.broadcast_to(slopes, (NSA_KV_GROUPS, 8, HEADS_PER_GROUP, tq)).reshape(NSA_KV_GROUPS, 8, -1)

    x2 = x.reshape(n, D_MODEL)
    for l in range(depth):
        wa, wbt, wmg = _pack_in_proj(w_in[l])
        g1 = norm1_g[l].reshape(1, D_MODEL)
        u, cp, kp, qt, vt, glt = _in_proj(x2, g1, wa, wbt, tm=512)

        bblk, cblk, abar = _pack_s5(ssm_a_re[l], ssm_a_im[l], ssm_log_dt[l], ssm_b_re[l], ssm_b_im[l],
                                    ssm_c_re[l], ssm_c_im[l])
        ys = _s5(u, bblk, cblk, abar, ssm_d[l].reshape(1, SSM_WIDTH), ssm_w_glu[l].astype(BF16),
                 ssm_b_glu[l].reshape(1, SSM_WIDTH), batch=batch, seq=seq, tc=256)

        qg = jnp.broadcast_to(q_norm_g[l].reshape(HEAD_DIM, 1), (HEAD_DIM, tq))
        kg = jnp.zeros((8, HEAD_DIM), F32).at[0:3].set(k_norm_g[l])
        pe, w1 = _pack_compress(cmp_pe_k[l], cmp_pe_v[l], cmp_wk1[l], cmp_wv1[l])
        ynt = _nsa(qt, cp, kp, vt, glt, slopes, qg, kg, pe, w1,
                   cmp_wk2[l].astype(BF16), cmp_wv2[l].T.astype(BF16), batch=batch, seq=seq, tq=tq)

        x2 = _final(x2, ys, ynt, g1, wmg, w_proj_ssm[l].astype(BF16), w_proj_nsa[l].astype(BF16),
                    w_out[l].astype(BF16), norm2_g[l].reshape(1, D_MODEL), w_up[l].astype(BF16),
                    w_down[l].astype(BF16), tm=256)
    return x2.reshape(batch, seq, D_MODEL)
```

```python
import functools

import jax
import jax.numpy as jnp
from jax import lax
from jax.experimental import pallas as pl
from jax.experimental.pallas import tpu as pltpu

D_MODEL = 1024
SSM_WIDTH = D_MODEL // 2
SSM_GROUP = 16
SSM_GROUPS = SSM_WIDTH // SSM_GROUP
SSM_STATE = 64
SSM_STATES = SSM_GROUPS * SSM_STATE
NSA_HEADS = 8
NSA_KV_GROUPS = 2
HEADS_PER_GROUP = NSA_HEADS // NSA_KV_GROUPS
HEAD_DIM = 64
NSA_WIDTH = NSA_HEADS * HEAD_DIM
CMP_STRIDE = 16
CMP_BLOCK = 2 * CMP_STRIDE
CMP_HIDDEN = 2 * HEAD_DIM
SEL_BLOCK = 64
SEL_TOP_K = 16
WINDOW = 512
D_FF = 4 * D_MODEL
N_MIXERS = 2
KV_WIDTH = 3 * 2 * NSA_KV_GROUPS * HEAD_DIM
EPS = 1e-6
NEG_INF = -1e30
FORCE_BONUS = 1e4

LANES = 128
VMEM_LIMIT_BYTES = 56 * 1024 * 1024

GATE_ROWS = 16

F32 = jnp.float32
BF16 = jnp.bfloat16


def _dot(a, b):
    return jnp.dot(a, b, preferred_element_type=F32)


def _dot_nt(a, b):
    return lax.dot_general(a, b, (((1,), (1,)), ((), ())), preferred_element_type=F32)


def _dot_tn(a, b):
    return lax.dot_general(a, b, (((0,), (0,)), ((), ())), preferred_element_type=F32)


def _rms_rows(x, g):
    ms = jnp.mean(x * x, axis=-1, keepdims=True)
    return x * lax.rsqrt(ms + EPS) * g


def _inproj_kernel(x_ref, g_ref, wa_ref, wbt_ref, u_ref, cp_ref, kp_ref, qt_ref, vt_ref, glt_ref):
    h = _rms_rows(x_ref[...], g_ref[...]).astype(BF16)
    a = _dot(h, wa_ref[...])
    u_ref[...] = a[:, :SSM_WIDTH]
    cp_ref[...] = a[:, SSM_WIDTH:SSM_WIDTH + 4 * HEAD_DIM]
    kp_ref[...] = a[:, SSM_WIDTH + 4 * HEAD_DIM:]
    bt = _dot_nt(wbt_ref[...], h)
    qt_ref[...] = bt[:NSA_WIDTH]
    vt_ref[...] = bt[NSA_WIDTH:NSA_WIDTH + 4 * HEAD_DIM]
    glt_ref[...] = bt[NSA_WIDTH + 4 * HEAD_DIM:]


def _in_proj(x2, g1, wa, wbt, *, tm):
    n = x2.shape[0]
    nb = wbt.shape[0]
    const = lambda i: (0, 0)
    return pl.pallas_call(
        _inproj_kernel,
        grid=(n // tm,),
        in_specs=[
            pl.BlockSpec((tm, D_MODEL), lambda i: (i, 0)),
            pl.BlockSpec((1, D_MODEL), const),
            pl.BlockSpec(wa.shape, const),
            pl.BlockSpec(wbt.shape, const),
        ],
        out_specs=[
            pl.BlockSpec((tm, SSM_WIDTH), lambda i: (i, 0)),
            pl.BlockSpec((tm, 4 * HEAD_DIM), lambda i: (i, 0)),
            pl.BlockSpec((tm, 4 * HEAD_DIM), lambda i: (i, 0)),
            pl.BlockSpec((NSA_WIDTH, tm), lambda i: (0, i)),
            pl.BlockSpec((4 * HEAD_DIM, tm), lambda i: (0, i)),
            pl.BlockSpec((NSA_KV_GROUPS * GATE_ROWS, tm), lambda i: (0, i)),
        ],
        out_shape=[
            jax.ShapeDtypeStruct((n, SSM_WIDTH), F32),
            jax.ShapeDtypeStruct((n, 4 * HEAD_DIM), F32),
            jax.ShapeDtypeStruct((n, 4 * HEAD_DIM), F32),
            jax.ShapeDtypeStruct((NSA_WIDTH, n), F32),
            jax.ShapeDtypeStruct((4 * HEAD_DIM, n), F32),
            jax.ShapeDtypeStruct((NSA_KV_GROUPS * GATE_ROWS, n), F32),
        ],
        compiler_params=pltpu.CompilerParams(
            dimension_semantics=("arbitrary",), vmem_limit_bytes=VMEM_LIMIT_BYTES),
        name="in_proj",
    )(x2, g1, wa, wbt)


SCAN_SLAB = 512
SCAN_UNROLL = 8


def _s5_kernel(u_ref, bblk_ref, cblk_ref, abar_ref, d_ref, wglu_ref, bglu_ref, y_ref, bx_s, carry_s, *, tc):
    @pl.when(pl.program_id(1) == 0)
    def _reset():
        carry_s[...] = jnp.zeros_like(carry_s)

    u = u_ref[...]
    bx_s[...] = _dot(u.astype(BF16), bblk_ref[...])

    for slab in range(SSM_STATES // SCAN_SLAB):
        re_cols = pl.ds(slab * SCAN_SLAB, SCAN_SLAB)
        im_cols = pl.ds(SSM_STATES + slab * SCAN_SLAB, SCAN_SLAB)
        a_re = abar_ref[0:1, re_cols]
        a_im = abar_ref[1:2, re_cols]

        def step(t8, carry):
            xr, xi = carry
            for k in range(SCAN_UNROLL):
                row = pl.ds(t8 * SCAN_UNROLL + k, 1)
                nr = a_re * xr - a_im * xi + bx_s[row, re_cols]
                ni = a_re * xi + a_im * xr + bx_s[row, im_cols]
                bx_s[row, re_cols] = nr
                bx_s[row, im_cols] = ni
                xr, xi = nr, ni
            return xr, xi

        xr, xi = lax.fori_loop(0, tc // SCAN_UNROLL, step, (carry_s[0:1, re_cols], carry_s[0:1, im_cols]))
        carry_s[0:1, re_cols] = xr
        carry_s[0:1, im_cols] = xi

    y = _dot(bx_s[...].astype(BF16), cblk_ref[...]) + d_ref[...] * u
    yg = jax.nn.gelu(y)
    z = _dot(yg.astype(BF16), wglu_ref[...]) + bglu_ref[...]
    y_ref[...] = (yg * jax.nn.sigmoid(z)).astype(y_ref.dtype)


def _s5(u, bblk, cblk, abar, d_row, wglu, bglu, *, batch, seq, tc):
    nchunk = seq // tc
    const = lambda b, c: (0, 0)
    return pl.pallas_call(
        functools.partial(_s5_kernel, tc=tc),
        grid=(batch, nchunk),
        in_specs=[
            pl.BlockSpec((tc, SSM_WIDTH), lambda b, c: (b * nchunk + c, 0)),
            pl.BlockSpec(bblk.shape, const),
            pl.BlockSpec(cblk.shape, const),
            pl.BlockSpec(abar.shape, const),
            pl.BlockSpec(d_row.shape, const),
            pl.BlockSpec(wglu.shape, const),
            pl.BlockSpec(bglu.shape, const),
        ],
        out_specs=pl.BlockSpec((tc, SSM_WIDTH), lambda b, c: (b * nchunk + c, 0)),
        out_shape=jax.ShapeDtypeStruct((batch * seq, SSM_WIDTH), BF16),
        scratch_shapes=[
            pltpu.VMEM((tc, 2 * SSM_STATES), F32),
            pltpu.VMEM((8, 2 * SSM_STATES), F32),
        ],
        compiler_params=pltpu.CompilerParams(
            dimension_semantics=("arbitrary", "arbitrary"), vmem_limit_bytes=VMEM_LIMIT_BYTES),
        name="s5",
    )(u, bblk, cblk, abar, d_row, wglu, bglu)


KEY_TILE = 128
SEL_CHAINS = 8


def _nsa_kernel(qt_ref, cp_ref, kp_ref, vt_ref, glt_ref, slope_ref, qg_ref, kg_ref, pe_ref,
                w1_ref, w2k_ref, w2vt_ref, out_ref,
                ks_s, kw_s, kc_s, vct_s, vst_s, vwt_s, sval_s, bias_s, cmat_s, acc_s, *, seq, tq):
    assert tq == KEY_TILE
    i = pl.program_id(2)
    n_cmp = seq // CMP_STRIDE
    n_sel = seq // SEL_BLOCK
    k_top = min(SEL_TOP_K, n_sel)
    hq = HEADS_PER_GROUP * tq

    @pl.when(i == 0)
    def _prepare_keys_and_values():
        kg = kg_ref[...]
        ks_s[...] = _rms_rows(kp_ref[:, 0:HEAD_DIM], kg[1:2]).astype(BF16)
        kw_s[...] = _rms_rows(kp_ref[:, HEAD_DIM:2 * HEAD_DIM], kg[2:3]).astype(BF16)
        vst_s[...] = vt_ref[0:HEAD_DIM, :].astype(BF16)
        vwt_s[...] = vt_ref[HEAD_DIM:2 * HEAD_DIM, :].astype(BF16)

        lo = jnp.zeros((n_cmp, 2 * CMP_HIDDEN), F32)
        hi = jnp.zeros((n_cmp, 2 * CMP_HIDDEN), F32)
        for tt in range(CMP_STRIDE):
            tok = cp_ref[pl.ds(tt, n_cmp, stride=CMP_STRIDE), :]
            lo = lo + _dot((tok + pe_ref[tt:tt + 1, :]).astype(BF16), w1_ref[tt])
            hi = hi + _dot((tok + pe_ref[CMP_STRIDE + tt:CMP_STRIDE + tt + 1, :]).astype(BF16),
                           w1_ref[CMP_STRIDE + tt])
        hidden = jax.nn.gelu(lo + pltpu.roll(hi, n_cmp - 1, 0)).astype(BF16)
        kc_s[...] = _rms_rows(_dot(hidden[:, :CMP_HIDDEN], w2k_ref[...]), kg[0:1]).astype(BF16)
        vct_s[...] = _dot_nt(w2vt_ref[...], hidden[:, CMP_HIDDEN:]).astype(BF16)

        r = lax.broadcasted_iota(jnp.int32, (KEY_TILE, hq), 0)
        a = lax.broadcasted_iota(jnp.int32, (KEY_TILE, hq), 1) & (tq - 1)
        alibi = slope_ref[0, 0:1, :] * r.astype(F32)
        cmat_s[0] = alibi
        cmat_s[1] = jnp.where(r <= a, alibi, NEG_INF)
        cmat_s[2] = jnp.where(r > a, alibi, NEG_INF)

    qs = i * tq
    lane = lax.broadcasted_iota(jnp.int32, (1, hq), 1)
    t_row = qs + (lane & (tq - 1))
    slope_row = slope_ref[0, 0:1, :]

    qg = qg_ref[...]
    heads = []
    for h in range(HEADS_PER_GROUP):
        qh = qt_ref[h * HEAD_DIM:(h + 1) * HEAD_DIM, :]
        ms = jnp.mean(qh * qh, axis=0, keepdims=True)
        heads.append((qh * lax.rsqrt(ms + EPS) * qg * (HEAD_DIM ** -0.5)).astype(BF16))
    qst = jnp.concatenate(heads, axis=1)

    sc = _dot(kc_s[...], qst)
    jrow = lax.broadcasted_iota(jnp.int32, (n_cmp, hq), 0)
    dist_c = t_row - (jrow * CMP_STRIDE + (CMP_BLOCK - 1))
    mask_c = dist_c >= 0
    sc = jnp.where(mask_c, sc - slope_row * dist_c.astype(F32), NEG_INF)
    pc = jnp.exp(sc - jnp.max(sc, axis=0, keepdims=True))
    pc = jnp.where(mask_c, pc * (1.0 / jnp.sum(pc, axis=0, keepdims=True)), 0.0)
    o_cmp = _dot(vct_s[...], pc.astype(BF16))

    imp = pc[:, 0:tq]
    for h in range(1, HEADS_PER_GROUP):
        imp = imp + pc[:, h * tq:(h + 1) * tq]
    nsel_r = lax.broadcasted_iota(jnp.int32, (n_sel, n_cmp), 0)
    jcmp_c = lax.broadcasted_iota(jnp.int32, (n_sel, n_cmp), 1)
    per_block = SEL_BLOCK // CMP_STRIDE
    gather = jnp.where((jcmp_c >= per_block * nsel_r - 1) & (jcmp_c <= per_block * nsel_r + per_block - 1), 1.0, 0.0)
    imp_sel = jnp.dot(gather, imp, preferred_element_type=F32, precision=lax.Precision.HIGHEST)
    blk = lax.broadcasted_iota(jnp.int32, (n_sel, tq), 0)
    cur = jnp.right_shift(t_row[:, 0:tq], SEL_BLOCK.bit_length() - 1)
    forced = (blk == 0) | (blk == cur) | (blk == cur - 1)
    sval = jnp.where(blk <= cur, imp_sel + jnp.where(forced, FORCE_BONUS, 0.0), NEG_INF)
    sval_s[...] = sval

    def rank_step(m, rank):
        row = sval_s[pl.ds(m, 1), :]
        ahead = jnp.where(row > sval, 1.0, jnp.where(row == sval, jnp.where(blk > m, 1.0, 0.0), 0.0))
        return rank + ahead

    n_live = jnp.minimum(n_sel, (qs + tq) // SEL_BLOCK)
    rank = lax.fori_loop(0, n_live, rank_step, jnp.zeros((n_sel, tq), F32))
    bias = jnp.where(rank < k_top, 0.0, NEG_INF)
    bias_s[...] = jnp.concatenate([bias] * HEADS_PER_GROUP, axis=1)

    def tiles_update(kts, k_s, vt_s, cidxs, biases_per_tile, ms, ls, accs):
        k0s = [pl.multiple_of(kt * KEY_TILE, KEY_TILE) for kt in kts]
        scores = [_dot(k_s[pl.ds(k0, KEY_TILE), :], qst) + cmat_s[cidx] for k0, cidx in zip(k0s, cidxs)]
        new_ms, new_ls, alphas, probs = [], [], [], []
        for s, k0, biases, m, l in zip(scores, k0s, biases_per_tile, ms, ls):
            rv = slope_row * (t_row - k0).astype(F32)
            rows = KEY_TILE // len(biases)
            parts = [s[b * rows:(b + 1) * rows] for b in range(len(biases))]
            m_new = m
            for part, bias_row in zip(parts, biases):
                m_new = jnp.maximum(m_new, jnp.max(part, axis=0, keepdims=True) + (bias_row - rv))
            seen = m_new > 0.5 * NEG_INF
            ps = []
            for part, bias_row in zip(parts, biases):
                shift = jnp.where(seen, m_new + (rv - bias_row), -NEG_INF)
                ps.append(jnp.exp(part - shift))
            p = jnp.concatenate(ps, axis=0) if len(ps) > 1 else ps[0]
            alpha = jnp.exp(m - m_new)
            new_ms.append(m_new)
            new_ls.append(alpha * l + jnp.sum(p, axis=0, keepdims=True))
            alphas.append(alpha)
            probs.append(p.astype(BF16))
        new_accs = [alpha * acc + _dot(vt_s[:, pl.ds(k0, KEY_TILE)], p)
                    for alpha, acc, k0, p in zip(alphas, accs, k0s, probs)]
        return new_ms, new_ls, new_accs

    def merge(ms, ls, accs):
        m = functools.reduce(jnp.maximum, ms)
        ws = [jnp.exp(mc - m) for mc in ms]
        l = functools.reduce(lambda a, b: a + b, [w * lc for w, lc in zip(ws, ls)])
        acc = functools.reduce(lambda a, b: a + b, [w * ac for w, ac in zip(ws, accs)])
        return acc * (1.0 / l)

    m_init = jnp.full((1, hq), NEG_INF, F32)
    l_init = jnp.zeros((1, hq), F32)
    dead_row = jnp.full((1, hq), NEG_INF, F32)
    live_row = jnp.zeros((1, hq), F32)

    blocks_per_tile = KEY_TILE // SEL_BLOCK
    for c in range(SEL_CHAINS):
        acc_s[c] = jnp.zeros((HEAD_DIM, hq), F32)

    def sel_step(j, carry):
        ms, ls = carry
        kts = [j * SEL_CHAINS + c for c in range(SEL_CHAINS)]
        biases = [[jnp.where(kt > i, dead_row, bias_s[pl.ds(blocks_per_tile * kt + b, 1), :])
                   for b in range(blocks_per_tile)] for kt in kts]
        cidxs = [jnp.where(kt == i, 1, 0) for kt in kts]
        new_ms, new_ls, new_accs = tiles_update(kts, ks_s, vst_s, cidxs, biases, ms, ls,
                                                [acc_s[c] for c in range(SEL_CHAINS)])
        for c in range(SEL_CHAINS):
            acc_s[c] = new_accs[c]
        return tuple(new_ms), tuple(new_ls)

    ms, ls = lax.fori_loop(0, (i + SEL_CHAINS) // SEL_CHAINS, sel_step,
                           ((m_init,) * SEL_CHAINS, (l_init,) * SEL_CHAINS))
    o_sel = merge(ms, ls, [acc_s[c] for c in range(SEL_CHAINS)])

    far = WINDOW // KEY_TILE
    kts = [i - d for d in range(far + 1)]
    o_win = merge(*tiles_update(
        [jnp.maximum(kt, 0) for kt in kts], kw_s, vwt_s,
        [1 if d == 0 else (2 if d == far else 0) for d in range(far + 1)],
        [[jnp.where(kt < 0, dead_row, live_row)] for kt in kts],
        [m_init] * (far + 1), [l_init] * (far + 1), [jnp.zeros((HEAD_DIM, hq), F32)] * (far + 1)))

    gl = glt_ref[...]

    def gate_row(branch):
        rows = [jax.nn.sigmoid(gl[branch * HEADS_PER_GROUP + h:branch * HEADS_PER_GROUP + h + 1, :])
                for h in range(HEADS_PER_GROUP)]
        return jnp.concatenate(rows, axis=1)

    ot = gate_row(0) * o_cmp + gate_row(1) * o_sel + gate_row(2) * o_win
    out_ref[...] = jnp.concatenate(
        [ot[:, h * tq:(h + 1) * tq] for h in range(HEADS_PER_GROUP)], axis=0).astype(out_ref.dtype)


def _nsa(qt, cp, kp, vt, glt, slopes, qg, kg, pe, w1, w2k, w2vt, *, batch, seq, tq):
    nq = seq // tq
    n_cmp = seq // CMP_STRIDE
    n_sel = seq // SEL_BLOCK
    hq = HEADS_PER_GROUP * tq
    const2 = lambda b, g, i: (0, 0)
    const3 = lambda b, g, i: (0, 0, 0)
    return pl.pallas_call(
        functools.partial(_nsa_kernel, seq=seq, tq=tq),
        grid=(batch, NSA_KV_GROUPS, nq),
        in_specs=[
            pl.BlockSpec((HEADS_PER_GROUP * HEAD_DIM, tq), lambda b, g, i: (g, b * nq + i)),
            pl.BlockSpec((seq, 2 * HEAD_DIM), lambda b, g, i: (b, g)),
            pl.BlockSpec((seq, 2 * HEAD_DIM), lambda b, g, i: (b, g)),
            pl.BlockSpec((2 * HEAD_DIM, seq), lambda b, g, i: (g, b)),
            pl.BlockSpec((GATE_ROWS, tq), lambda b, g, i: (g, b * nq + i)),
            pl.BlockSpec((1, 8, hq), lambda b, g, i: (g, 0, 0)),
            pl.BlockSpec(qg.shape, const2),
            pl.BlockSpec(kg.shape, const2),
            pl.BlockSpec(pe.shape, const2),
            pl.BlockSpec(w1.shape, const3),
            pl.BlockSpec(w2k.shape, const2),
            pl.BlockSpec(w2vt.shape, const2),
        ],
        out_specs=pl.BlockSpec((HEADS_PER_GROUP * HEAD_DIM, tq), lambda b, g, i: (g, b * nq + i)),
        out_shape=jax.ShapeDtypeStruct((NSA_WIDTH, batch * seq), BF16),
        scratch_shapes=[
            pltpu.VMEM((seq, HEAD_DIM), BF16),
            pltpu.VMEM((seq, HEAD_DIM), BF16),
            pltpu.VMEM((n_cmp, HEAD_DIM), BF16),
            pltpu.VMEM((HEAD_DIM, n_cmp), BF16),
            pltpu.VMEM((HEAD_DIM, seq), BF16),
            pltpu.VMEM((HEAD_DIM, seq), BF16),
            pltpu.VMEM((n_sel, tq), F32),
            pltpu.VMEM((n_sel, hq), F32),
            pltpu.VMEM((3, KEY_TILE, hq), F32),
            pltpu.VMEM((SEL_CHAINS, HEAD_DIM, hq), F32),
        ],
        compiler_params=pltpu.CompilerParams(
            dimension_semantics=("arbitrary", "arbitrary", "arbitrary"), vmem_limit_bytes=VMEM_LIMIT_BYTES),
        name="nsa",
    )(qt, cp, kp, vt, glt, slopes, qg, kg, pe, w1, w2k, w2vt)


def _final_kernel(x_ref, ys_ref, ynt_ref, g1_ref, wmg_ref, wps_ref, wpn_ref, wout_ref, g2_ref, wup_ref, wdn_ref,
                  o_ref):
    x = x_ref[...]
    h = _rms_rows(x, g1_ref[...]).astype(BF16)
    mg = _dot(h, wmg_ref[...])
    ps = _dot(ys_ref[...], wps_ref[...])
    pn = _dot_tn(ynt_ref[...], wpn_ref[...])
    merged = jax.nn.sigmoid(mg[:, :D_MODEL]) * ps + jax.nn.sigmoid(mg[:, D_MODEL:]) * pn
    x1 = x + _dot(merged.astype(BF16), wout_ref[...])
    h2 = _rms_rows(x1, g2_ref[...]).astype(BF16)
    a = jnp.maximum(_dot(h2, wup_ref[...]), 0.0)
    o_ref[...] = x1 + _dot((a * a).astype(BF16), wdn_ref[...])


def _final(x2, ys, ynt, g1, wmg, wps, wpn, wout, g2, wup, wdn, *, tm):
    n = x2.shape[0]
    const = lambda i: (0, 0)
    resident = lambda w: pl.BlockSpec(w.shape, const, pipeline_mode=pl.Buffered(1))
    return pl.pallas_call(
        _final_kernel,
        grid=(n // tm,),
        in_specs=[
            pl.BlockSpec((tm, D_MODEL), lambda i: (i, 0)),
            pl.BlockSpec((tm, SSM_WIDTH), lambda i: (i, 0)),
            pl.BlockSpec((NSA_WIDTH, tm), lambda i: (0, i)),
            resident(g1), resident(wmg), resident(wps), resident(wpn), resident(wout),
            resident(g2), resident(wup), resident(wdn),
        ],
        out_specs=pl.BlockSpec((tm, D_MODEL), lambda i: (i, 0)),
        out_shape=jax.ShapeDtypeStruct((n, D_MODEL), F32),
        compiler_params=pltpu.CompilerParams(
            dimension_semantics=("arbitrary",), vmem_limit_bytes=VMEM_LIMIT_BYTES),
        name="final",
    )(x2, ys, ynt, g1, wmg, wps, wpn, wout, g2, wup, wdn)


def _pack_in_proj(w_in):
    o1 = SSM_WIDTH
    o2 = o1 + NSA_WIDTH
    o3 = o2 + KV_WIDTH
    o4 = o3 + 3 * NSA_HEADS
    w_u, w_q, w_kv, w_gl = w_in[:, :o1], w_in[:, o1:o2], w_in[:, o2:o3], w_in[:, o3:o4]
    w_mg = w_in[:, o4:]
    kv = w_kv.reshape(D_MODEL, 3, 2, NSA_KV_GROUPS, HEAD_DIM)
    groups = range(NSA_KV_GROUPS)
    cpack = [jnp.concatenate([kv[:, 0, 0, g], kv[:, 0, 1, g]], axis=1) for g in groups]
    kpack = [jnp.concatenate([kv[:, 1, 0, g], kv[:, 2, 0, g]], axis=1) for g in groups]
    wa = jnp.concatenate([w_u] + cpack + kpack, axis=1)
    vrows = [jnp.concatenate([kv[:, 1, 1, g], kv[:, 2, 1, g]], axis=1) for g in range(NSA_KV_GROUPS)]
    gl = w_gl.reshape(D_MODEL, 3, NSA_KV_GROUPS, HEADS_PER_GROUP)
    pad = jnp.zeros((D_MODEL, GATE_ROWS - 3 * HEADS_PER_GROUP), w_in.dtype)
    glrows = [jnp.concatenate([gl[:, 0, g], gl[:, 1, g], gl[:, 2, g], pad], axis=1) for g in range(NSA_KV_GROUPS)]
    wbt = jnp.concatenate([w_q] + vrows + glrows, axis=1).T
    return wa.astype(BF16), wbt.astype(BF16), w_mg.astype(BF16)


def _pack_compress(pe_k, pe_v, w1k, w1v):
    pe = jnp.concatenate([pe_k, pe_v], axis=1)
    w1k = w1k.reshape(CMP_BLOCK, HEAD_DIM, CMP_HIDDEN)
    w1v = w1v.reshape(CMP_BLOCK, HEAD_DIM, CMP_HIDDEN)
    zero = jnp.zeros_like(w1k)
    w1 = jnp.concatenate([jnp.concatenate([w1k, zero], axis=2), jnp.concatenate([zero, w1v], axis=2)], axis=1)
    return pe, w1.astype(BF16)


def _pack_s5(a_re, a_im, log_dt, b_re, b_im, c_re, c_im):
    dt = jnp.exp(log_dt)[:, None]
    mag = jnp.exp(a_re * dt)
    abar_re = mag * jnp.cos(a_im * dt)
    abar_im = mag * jnp.sin(a_im * dt)
    den = a_re * a_re + a_im * a_im
    nr = abar_re - 1.0
    fr = (nr * a_re + abar_im * a_im) / den
    fi = (abar_im * a_re - nr * a_im) / den
    bbar_re = fr[..., None] * b_re - fi[..., None] * b_im
    bbar_im = fr[..., None] * b_im + fi[..., None] * b_re
    eye = jnp.eye(SSM_GROUPS, dtype=F32)

    def in_blocks(m):
        return (m.transpose(0, 2, 1)[:, :, None, :] * eye[:, None, :, None]).reshape(SSM_WIDTH, SSM_STATES)

    def out_blocks(m):
        return (m.transpose(0, 2, 1)[:, :, None, :] * eye[:, None, :, None]).reshape(SSM_STATES, SSM_WIDTH)

    bblk = jnp.concatenate([in_blocks(bbar_re), in_blocks(bbar_im)], axis=1).astype(BF16)
    cblk = jnp.concatenate([out_blocks(c_re), -out_blocks(c_im)], axis=0).astype(BF16)
    abar = jnp.zeros((8, SSM_STATES), F32)
    abar = abar.at[0].set(abar_re.reshape(-1)).at[1].set(abar_im.reshape(-1))
    return bblk, cblk, abar


def kernel(x, norm1_g, w_in, ssm_a_re, ssm_a_im, ssm_log_dt, ssm_b_re, ssm_b_im, ssm_c_re, ssm_c_im, ssm_d, ssm_w_glu, ssm_b_glu, cmp_pe_k, cmp_pe_v, cmp_wk1, cmp_wk2, cmp_wv1, cmp_wv2, q_norm_g, k_norm_g, w_proj_ssm, w_proj_nsa, w_out, norm2_g, w_up, w_down):
    batch, seq, d_model = x.shape
    assert d_model == D_MODEL and seq % 512 == 0
    n = batch * seq
    tq = 128
    depth = w_in.shape[0]
    head = jnp.arange(1, NSA_HEADS + 1, dtype=F32)
    slopes = jnp.exp2(-8.0 * head / NSA_HEADS).reshape(NSA_KV_GROUPS, 1, HEADS_PER_GROUP, 1)
    slopes = jnp.broadcast_to(slopes, (NSA_KV_GROUPS, 8, HEADS_PER_GROUP, tq)).reshape(NSA_KV_GROUPS, 8, -1)

    x2 = x.reshape(n, D_MODEL)
    for l in range(depth):
        wa, wbt, wmg = _pack_in_proj(w_in[l])
        g1 = norm1_g[l].reshape(1, D_MODEL)
        u, cp, kp, qt, vt, glt = _in_proj(x2, g1, wa, wbt, tm=512)

        bblk, cblk, abar = _pack_s5(ssm_a_re[l], ssm_a_im[l], ssm_log_dt[l], ssm_b_re[l], ssm_b_im[l],
                                    ssm_c_re[l], ssm_c_im[l])
        ys = _s5(u, bblk, cblk, abar, ssm_d[l].reshape(1, SSM_WIDTH), ssm_w_glu[l].astype(BF16),
                 ssm_b_glu[l].reshape(1, SSM_WIDTH), batch=batch, seq=seq, tc=256)

        qg = jnp.broadcast_to(q_norm_g[l].reshape(HEAD_DIM, 1), (HEAD_DIM, tq))
        kg = jnp.zeros((8, HEAD_DIM), F32).at[0:3].set(k_norm_g[l])
        pe, w1 = _pack_compress(cmp_pe_k[l], cmp_pe_v[l], cmp_wk1[l], cmp_wv1[l])
        ynt = _nsa(qt, cp, kp, vt, glt, slopes, qg, kg, pe, w1,
                   cmp_wk2[l].astype(BF16), cmp_wv2[l].T.astype(BF16), batch=batch, seq=seq, tq=tq)

        x2 = _final(x2, ys, ynt, g1, wmg, w_proj_ssm[l].astype(BF16), w_proj_nsa[l].astype(BF16),
                    w_out[l].astype(BF16), norm2_g[l].reshape(1, D_MODEL), w_up[l].astype(BF16),
                    w_down[l].astype(BF16), tm=256)
    return x2.reshape(batch, seq, D_MODEL)
```

```python
import functools

import jax
import jax.numpy as jnp
from jax import lax
from jax.experimental import pallas as pl
from jax.experimental.pallas import tpu as pltpu

D_MODEL = 1024
SSM_WIDTH = D_MODEL // 2
SSM_GROUP = 16
SSM_GROUPS = SSM_WIDTH // SSM_GROUP
SSM_STATE = 64
SSM_STATES = SSM_GROUPS * SSM_STATE
NSA_HEADS = 8
NSA_KV_GROUPS = 2
HEADS_PER_GROUP = NSA_HEADS // NSA_KV_GROUPS
HEAD_DIM = 64
NSA_WIDTH = NSA_HEADS * HEAD_DIM
CMP_STRIDE = 16
CMP_BLOCK = 2 * CMP_STRIDE
CMP_HIDDEN = 2 * HEAD_DIM
SEL_BLOCK = 64
SEL_TOP_K = 16
WINDOW = 512
D_FF = 4 * D_MODEL
N_MIXERS = 2
KV_WIDTH = 3 * 2 * NSA_KV_GROUPS * HEAD_DIM
EPS = 1e-6
NEG_INF = -1e30
FORCE_BONUS = 1e4

LANES = 128
VMEM_LIMIT_BYTES = 56 * 1024 * 1024

GATE_ROWS = 16

F32 = jnp.float32
BF16 = jnp.bfloat16


def _dot(a, b):
    return jnp.dot(a, b, preferred_element_type=F32)


def _dot_nt(a, b):
    return lax.dot_general(a, b, (((1,), (1,)), ((), ())), preferred_element_type=F32)


def _dot_tn(a, b):
    return lax.dot_general(a, b, (((0,), (0,)), ((), ())), preferred_element_type=F32)


def _rms_rows(x, g):
    ms = jnp.mean(x * x, axis=-1, keepdims=True)
    return x * lax.rsqrt(ms + EPS) * g


def _inproj_kernel(x_ref, g_ref, wa_ref, wbt_ref, u_ref, cp_ref, kp_ref, qt_ref, vt_ref, glt_ref):
    h = _rms_rows(x_ref[...], g_ref[...]).astype(BF16)
    a = _dot(h, wa_ref[...])
    u_ref[...] = a[:, :SSM_WIDTH]
    cp_ref[...] = a[:, SSM_WIDTH:SSM_WIDTH + 4 * HEAD_DIM]
    kp_ref[...] = a[:, SSM_WIDTH + 4 * HEAD_DIM:]
    bt = _dot_nt(wbt_ref[...], h)
    qt_ref[...] = bt[:NSA_WIDTH]
    vt_ref[...] = bt[NSA_WIDTH:NSA_WIDTH + 4 * HEAD_DIM]
    glt_ref[...] = bt[NSA_WIDTH + 4 * HEAD_DIM:]


def _in_proj(x2, g1, wa, wbt, *, tm):
    n = x2.shape[0]
    nb = wbt.shape[0]
    const = lambda i: (0, 0)
    return pl.pallas_call(
        _inproj_kernel,
        grid=(n // tm,),
        in_specs=[
            pl.BlockSpec((tm, D_MODEL), lambda i: (i, 0)),
            pl.BlockSpec((1, D_MODEL), const),
            pl.BlockSpec(wa.shape, const),
            pl.BlockSpec(wbt.shape, const),
        ],
        out_specs=[
            pl.BlockSpec((tm, SSM_WIDTH), lambda i: (i, 0)),
            pl.BlockSpec((tm, 4 * HEAD_DIM), lambda i: (i, 0)),
            pl.BlockSpec((tm, 4 * HEAD_DIM), lambda i: (i, 0)),
            pl.BlockSpec((NSA_WIDTH, tm), lambda i: (0, i)),
            pl.BlockSpec((4 * HEAD_DIM, tm), lambda i: (0, i)),
            pl.BlockSpec((NSA_KV_GROUPS * GATE_ROWS, tm), lambda i: (0, i)),
        ],
        out_shape=[
            jax.ShapeDtypeStruct((n, SSM_WIDTH), F32),
            jax.ShapeDtypeStruct((n, 4 * HEAD_DIM), F32),
            jax.ShapeDtypeStruct((n, 4 * HEAD_DIM), F32),
            jax.ShapeDtypeStruct((NSA_WIDTH, n), F32),
            jax.ShapeDtypeStruct((4 * HEAD_DIM, n), F32),
            jax.ShapeDtypeStruct((NSA_KV_GROUPS * GATE_ROWS, n), F32),
        ],
        compiler_params=pltpu.CompilerParams(
            dimension_semantics=("arbitrary",), vmem_limit_bytes=VMEM_LIMIT_BYTES),
        name="in_proj",
    )(x2, g1, wa, wbt)


SCAN_SLAB = 512
SCAN_UNROLL = 8


def _s5_kernel(u_ref, bblk_ref, cblk_ref, abar_ref, d_ref, wglu_ref, bglu_ref, y_ref, bx_s, carry_s, *, tc):
    @pl.when(pl.program_id(1) == 0)
    def _reset():
        carry_s[...] = jnp.zeros_like(carry_s)

    u = u_ref[...]
    bx_s[...] = _dot(u.astype(BF16), bblk_ref[...])

    for slab in range(SSM_STATES // SCAN_SLAB):
        re_cols = pl.ds(slab * SCAN_SLAB, SCAN_SLAB)
        im_cols = pl.ds(SSM_STATES + slab * SCAN_SLAB, SCAN_SLAB)
        a_re = abar_ref[0:1, re_cols]
        a_im = abar_ref[1:2, re_cols]

        def step(t8, carry):
            xr, xi = carry
            for k in range(SCAN_UNROLL):
                row = pl.ds(t8 * SCAN_UNROLL + k, 1)
                nr = a_re * xr - a_im * xi + bx_s[row, re_cols]
                ni = a_re * xi + a_im * xr + bx_s[row, im_cols]
                bx_s[row, re_cols] = nr
                bx_s[row, im_cols] = ni
                xr, xi = nr, ni
            return xr, xi

        xr, xi = lax.fori_loop(0, tc // SCAN_UNROLL, step, (carry_s[0:1, re_cols], carry_s[0:1, im_cols]))
        carry_s[0:1, re_cols] = xr
        carry_s[0:1, im_cols] = xi

    y = _dot(bx_s[...].astype(BF16), cblk_ref[...]) + d_ref[...] * u
    yg = jax.nn.gelu(y)
    z = _dot(yg.astype(BF16), wglu_ref[...]) + bglu_ref[...]
    y_ref[...] = (yg * jax.nn.sigmoid(z)).astype(y_ref.dtype)


def _s5(u, bblk, cblk, abar, d_row, wglu, bglu, *, batch, seq, tc):
    nchunk = seq // tc
    const = lambda b, c: (0, 0)
    return pl.pallas_call(
        functools.partial(_s5_kernel, tc=tc),
        grid=(batch, nchunk),
        in_specs=[
            pl.BlockSpec((tc, SSM_WIDTH), lambda b, c: (b * nchunk + c, 0)),
            pl.BlockSpec(bblk.shape, const),
            pl.BlockSpec(cblk.shape, const),
            pl.BlockSpec(abar.shape, const),
            pl.BlockSpec(d_row.shape, const),
            pl.BlockSpec(wglu.shape, const),
            pl.BlockSpec(bglu.shape, const),
        ],
        out_specs=pl.BlockSpec((tc, SSM_WIDTH), lambda b, c: (b * nchunk + c, 0)),
        out_shape=jax.ShapeDtypeStruct((batch * seq, SSM_WIDTH), BF16),
        scratch_shapes=[
            pltpu.VMEM((tc, 2 * SSM_STATES), F32),
            pltpu.VMEM((8, 2 * SSM_STATES), F32),
        ],
        compiler_params=pltpu.CompilerParams(
            dimension_semantics=("arbitrary", "arbitrary"), vmem_limit_bytes=VMEM_LIMIT_BYTES),
        name="s5",
    )(u, bblk, cblk, abar, d_row, wglu, bglu)


KEY_TILE = 128
SEL_CHAINS = 4
WORD_BITS = 16


def _nsa_kernel(qt_ref, cp_ref, kp_ref, vt_ref, glt_ref, slope_ref, qg_ref, kg_ref, pe_ref,
                w1_ref, w2k_ref, w2vt_ref, out_ref,
                ks_s, kw_s, kc_s, vct_s, vst_s, vwt_s, sval_s, bias_s, cmat_s, acc_s, words_s, tiles_s, *, seq, tq):
    assert tq == KEY_TILE
    i = pl.program_id(2)
    n_cmp = seq // CMP_STRIDE
    n_sel = seq // SEL_BLOCK
    k_top = min(SEL_TOP_K, n_sel)
    hq = HEADS_PER_GROUP * tq

    @pl.when(i == 0)
    def _prepare_keys_and_values():
        kg = kg_ref[...]
        ks_s[...] = _rms_rows(kp_ref[:, 0:HEAD_DIM], kg[1:2]).astype(BF16)
        kw_s[...] = _rms_rows(kp_ref[:, HEAD_DIM:2 * HEAD_DIM], kg[2:3]).astype(BF16)
        vst_s[...] = vt_ref[0:HEAD_DIM, :].astype(BF16)
        vwt_s[...] = vt_ref[HEAD_DIM:2 * HEAD_DIM, :].astype(BF16)

        lo = jnp.zeros((n_cmp, 2 * CMP_HIDDEN), F32)
        hi = jnp.zeros((n_cmp, 2 * CMP_HIDDEN), F32)
        for tt in range(CMP_STRIDE):
            tok = cp_ref[pl.ds(tt, n_cmp, stride=CMP_STRIDE), :]
            lo = lo + _dot((tok + pe_ref[tt:tt + 1, :]).astype(BF16), w1_ref[tt])
            hi = hi + _dot((tok + pe_ref[CMP_STRIDE + tt:CMP_STRIDE + tt + 1, :]).astype(BF16),
                           w1_ref[CMP_STRIDE + tt])
        hidden = jax.nn.gelu(lo + pltpu.roll(hi, n_cmp - 1, 0)).astype(BF16)
        kc_s[...] = _rms_rows(_dot(hidden[:, :CMP_HIDDEN], w2k_ref[...]), kg[0:1]).astype(BF16)
        vct_s[...] = _dot_nt(w2vt_ref[...], hidden[:, CMP_HIDDEN:]).astype(BF16)

        r = lax.broadcasted_iota(jnp.int32, (KEY_TILE, hq), 0)
        a = lax.broadcasted_iota(jnp.int32, (KEY_TILE, hq), 1) & (tq - 1)
        alibi = slope_ref[0, 0:1, :] * r.astype(F32)
        cmat_s[0] = alibi
        cmat_s[1] = jnp.where(r <= a, alibi, NEG_INF)
        cmat_s[2] = jnp.where(r > a, alibi, NEG_INF)

    qs = i * tq
    lane = lax.broadcasted_iota(jnp.int32, (1, hq), 1)
    t_row = qs + (lane & (tq - 1))
    slope_row = slope_ref[0, 0:1, :]

    qg = qg_ref[...]
    heads = []
    for h in range(HEADS_PER_GROUP):
        qh = qt_ref[h * HEAD_DIM:(h + 1) * HEAD_DIM, :]
        ms = jnp.mean(qh * qh, axis=0, keepdims=True)
        heads.append((qh * lax.rsqrt(ms + EPS) * qg * (HEAD_DIM ** -0.5)).astype(BF16))
    qst = jnp.concatenate(heads, axis=1)

    sc = _dot(kc_s[...], qst)
    jrow = lax.broadcasted_iota(jnp.int32, (n_cmp, hq), 0)
    dist_c = t_row - (jrow * CMP_STRIDE + (CMP_BLOCK - 1))
    mask_c = dist_c >= 0
    sc = jnp.where(mask_c, sc - slope_row * dist_c.astype(F32), NEG_INF)
    pc = jnp.exp(sc - jnp.max(sc, axis=0, keepdims=True))
    pc = jnp.where(mask_c, pc * (1.0 / jnp.sum(pc, axis=0, keepdims=True)), 0.0)
    o_cmp = _dot(vct_s[...], pc.astype(BF16))

    imp = pc[:, 0:tq]
    for h in range(1, HEADS_PER_GROUP):
        imp = imp + pc[:, h * tq:(h + 1) * tq]
    nsel_r = lax.broadcasted_iota(jnp.int32, (n_sel, n_cmp), 0)
    jcmp_c = lax.broadcasted_iota(jnp.int32, (n_sel, n_cmp), 1)
    per_block = SEL_BLOCK // CMP_STRIDE
    gather = jnp.where((jcmp_c >= per_block * nsel_r - 1) & (jcmp_c <= per_block * nsel_r + per_block - 1), 1.0, 0.0)
    imp_sel = jnp.dot(gather, imp, preferred_element_type=F32, precision=lax.Precision.HIGHEST)
    blk = lax.broadcasted_iota(jnp.int32, (n_sel, tq), 0)
    cur = jnp.right_shift(t_row[:, 0:tq], SEL_BLOCK.bit_length() - 1)
    forced = (blk == 0) | (blk == cur) | (blk == cur - 1)
    sval = jnp.where(blk <= cur, imp_sel + jnp.where(forced, FORCE_BONUS, 0.0), NEG_INF)
    sval_s[...] = sval

    def rank_step(m, rank):
        row = sval_s[pl.ds(m, 1), :]
        ahead = jnp.where(row > sval, 1.0, jnp.where(row == sval, jnp.where(blk > m, 1.0, 0.0), 0.0))
        return rank + ahead

    n_live = jnp.minimum(n_sel, (qs + tq) // SEL_BLOCK)
    rank = lax.fori_loop(0, n_live, rank_step, jnp.zeros((n_sel, tq), F32))
    chosen = rank < k_top
    bias = jnp.where(chosen, 0.0, NEG_INF)
    bias_s[...] = jnp.concatenate([bias] * HEADS_PER_GROUP, axis=1)

    blocks_per_tile = KEY_TILE // SEL_BLOCK
    hit = jnp.where(chosen & (blk < blocks_per_tile * (i + 1)), 1.0, 0.0)
    blk_hit = jnp.max(hit, axis=1, keepdims=True)
    weight = jnp.left_shift(1, lax.broadcasted_iota(jnp.int32, (n_sel, 1), 0) & (WORD_BITS - 1)).astype(F32)
    packed = blk_hit * weight
    for w in range(n_sel // WORD_BITS):
        word = jnp.sum(packed[w * WORD_BITS:(w + 1) * WORD_BITS], axis=0, keepdims=True).astype(jnp.int32)
        words_s[w] = word[0, 0]

    tiles_per_word = WORD_BITS // blocks_per_tile

    def list_step(kt, count):
        word = words_s[kt // tiles_per_word]
        bits = jnp.right_shift(word, blocks_per_tile * (kt % tiles_per_word)) & ((1 << blocks_per_tile) - 1)
        tiles_s[count] = kt
        return count + jnp.where(bits != 0, 1, 0)

    n_active = lax.fori_loop(0, i + 1, list_step, 0)
    for c in range(SEL_CHAINS):
        tiles_s[n_active + c] = -1

    def tiles_update(kts, k_s, vt_s, cidxs, biases_per_tile, ms, ls, accs):
        k0s = [pl.multiple_of(kt * KEY_TILE, KEY_TILE) for kt in kts]
        scores = [_dot(k_s[pl.ds(k0, KEY_TILE), :], qst) + cmat_s[cidx] for k0, cidx in zip(k0s, cidxs)]
        new_ms, new_ls, alphas, probs = [], [], [], []
        for s, k0, biases, m, l in zip(scores, k0s, biases_per_tile, ms, ls):
            rv = slope_row * (t_row - k0).astype(F32)
            rows = KEY_TILE // len(biases)
            parts = [s[b * rows:(b + 1) * rows] for b in range(len(biases))]
            m_new = m
            for part, bias_row in zip(parts, biases):
                m_new = jnp.maximum(m_new, jnp.max(part, axis=0, keepdims=True) + (bias_row - rv))
            seen = m_new > 0.5 * NEG_INF
            ps = []
            for part, bias_row in zip(parts, biases):
                shift = jnp.where(seen, m_new + (rv - bias_row), -NEG_INF)
                ps.append(jnp.exp(part - shift))
            p = jnp.concatenate(ps, axis=0) if len(ps) > 1 else ps[0]
            alpha = jnp.exp(m - m_new)
            new_ms.append(m_new)
            new_ls.append(alpha * l + jnp.sum(p, axis=0, keepdims=True))
            alphas.append(alpha)
            probs.append(p.astype(BF16))
        new_accs = [alpha * acc + _dot(vt_s[:, pl.ds(k0, KEY_TILE)], p)
                    for alpha, acc, k0, p in zip(alphas, accs, k0s, probs)]
        return new_ms, new_ls, new_accs

    def merge(ms, ls, accs):
        m = functools.reduce(jnp.maximum, ms)
        ws = [jnp.exp(mc - m) for mc in ms]
        l = functools.reduce(lambda a, b: a + b, [w * lc for w, lc in zip(ws, ls)])
        acc = functools.reduce(lambda a, b: a + b, [w * ac for w, ac in zip(ws, accs)])
        return acc * (1.0 / l)

    m_init = jnp.full((1, hq), NEG_INF, F32)
    l_init = jnp.zeros((1, hq), F32)
    dead_row = jnp.full((1, hq), NEG_INF, F32)
    live_row = jnp.zeros((1, hq), F32)

    for c in range(SEL_CHAINS):
        acc_s[c] = jnp.zeros((HEAD_DIM, hq), F32)

    def sel_step(j, carry):
        ms, ls = carry
        entries = [tiles_s[j * SEL_CHAINS + c] for c in range(SEL_CHAINS)]
        kts = [jnp.maximum(e, 0) for e in entries]
        biases = [[jnp.where(e < 0, dead_row, bias_s[pl.ds(blocks_per_tile * kt + b, 1), :])
                   for b in range(blocks_per_tile)] for e, kt in zip(entries, kts)]
        cidxs = [jnp.where(e == i, 1, 0) for e in entries]
        new_ms, new_ls, new_accs = tiles_update(kts, ks_s, vst_s, cidxs, biases, ms, ls,
                                                [acc_s[c] for c in range(SEL_CHAINS)])
        for c in range(SEL_CHAINS):
            acc_s[c] = new_accs[c]
        return tuple(new_ms), tuple(new_ls)

    ms, ls = lax.fori_loop(0, (n_active + SEL_CHAINS - 1) // SEL_CHAINS, sel_step,
                           ((m_init,) * SEL_CHAINS, (l_init,) * SEL_CHAINS))
    o_sel = merge(ms, ls, [acc_s[c] for c in range(SEL_CHAINS)])

    far = WINDOW // KEY_TILE
    kts = [i - d for d in range(far + 1)]
    o_win = merge(*tiles_update(
        [jnp.maximum(kt, 0) for kt in kts], kw_s, vwt_s,
        [1 if d == 0 else (2 if d == far else 0) for d in range(far + 1)],
        [[jnp.where(kt < 0, dead_row, live_row)] for kt in kts],
        [m_init] * (far + 1), [l_init] * (far + 1), [jnp.zeros((HEAD_DIM, hq), F32)] * (far + 1)))

    gl = glt_ref[...]

    def gate_row(branch):
        rows = [jax.nn.sigmoid(gl[branch * HEADS_PER_GROUP + h:branch * HEADS_PER_GROUP + h + 1, :])
                for h in range(HEADS_PER_GROUP)]
        return jnp.concatenate(rows, axis=1)

    ot = gate_row(0) * o_cmp + gate_row(1) * o_sel + gate_row(2) * o_win
    out_ref[...] = jnp.concatenate(
        [ot[:, h * tq:(h + 1) * tq] for h in range(HEADS_PER_GROUP)], axis=0).astype(out_ref.dtype)


def _nsa(qt, cp, kp, vt, glt, slopes, qg, kg, pe, w1, w2k, w2vt, *, batch, seq, tq):
    nq = seq // tq
    n_cmp = seq // CMP_STRIDE
    n_sel = seq // SEL_BLOCK
    hq = HEADS_PER_GROUP * tq
    const2 = lambda b, g, i: (0, 0)
    const3 = lambda b, g, i: (0, 0, 0)
    return pl.pallas_call(
        functools.partial(_nsa_kernel, seq=seq, tq=tq),
        grid=(batch, NSA_KV_GROUPS, nq),
        in_specs=[
            pl.BlockSpec((HEADS_PER_GROUP * HEAD_DIM, tq), lambda b, g, i: (g, b * nq + i)),
            pl.BlockSpec((seq, 2 * HEAD_DIM), lambda b, g, i: (b, g)),
            pl.BlockSpec((seq, 2 * HEAD_DIM), lambda b, g, i: (b, g)),
            pl.BlockSpec((2 * HEAD_DIM, seq), lambda b, g, i: (g, b)),
            pl.BlockSpec((GATE_ROWS, tq), lambda b, g, i: (g, b * nq + i)),
            pl.BlockSpec((1, 8, hq), lambda b, g, i: (g, 0, 0)),
            pl.BlockSpec(qg.shape, const2),
            pl.BlockSpec(kg.shape, const2),
            pl.BlockSpec(pe.shape, const2),
            pl.BlockSpec(w1.shape, const3),
            pl.BlockSpec(w2k.shape, const2),
            pl.BlockSpec(w2vt.shape, const2),
        ],
        out_specs=pl.BlockSpec((HEADS_PER_GROUP * HEAD_DIM, tq), lambda b, g, i: (g, b * nq + i)),
        out_shape=jax.ShapeDtypeStruct((NSA_WIDTH, batch * seq), BF16),
        scratch_shapes=[
            pltpu.VMEM((seq, HEAD_DIM), BF16),
            pltpu.VMEM((seq, HEAD_DIM), BF16),
            pltpu.VMEM((n_cmp, HEAD_DIM), BF16),
            pltpu.VMEM((HEAD_DIM, n_cmp), BF16),
            pltpu.VMEM((HEAD_DIM, seq), BF16),
            pltpu.VMEM((HEAD_DIM, seq), BF16),
            pltpu.VMEM((n_sel, tq), F32),
            pltpu.VMEM((n_sel, hq), F32),
            pltpu.VMEM((3, KEY_TILE, hq), F32),
            pltpu.VMEM((SEL_CHAINS, HEAD_DIM, hq), F32),
            pltpu.SMEM((n_sel // WORD_BITS,), jnp.int32),
            pltpu.SMEM((seq // KEY_TILE + SEL_CHAINS,), jnp.int32),
        ],
        compiler_params=pltpu.CompilerParams(
            dimension_semantics=("arbitrary", "arbitrary", "arbitrary"), vmem_limit_bytes=VMEM_LIMIT_BYTES),
        name="nsa",
    )(qt, cp, kp, vt, glt, slopes, qg, kg, pe, w1, w2k, w2vt)


def _final_kernel(x_ref, ys_ref, ynt_ref, g1_ref, wmg_ref, wps_ref, wpn_ref, wout_ref, g2_ref, wup_ref, wdn_ref,
                  o_ref):
    x = x_ref[...]
    h = _rms_rows(x, g1_ref[...]).astype(BF16)
    mg = _dot(h, wmg_ref[...])
    ps = _dot(ys_ref[...], wps_ref[...])
    pn = _dot_tn(ynt_ref[...], wpn_ref[...])
    merged = jax.nn.sigmoid(mg[:, :D_MODEL]) * ps + jax.nn.sigmoid(mg[:, D_MODEL:]) * pn
    x1 = x + _dot(merged.astype(BF16), wout_ref[...])
    h2 = _rms_rows(x1, g2_ref[...]).astype(BF16)
    a = jnp.maximum(_dot(h2, wup_ref[...]), 0.0)
    o_ref[...] = x1 + _dot((a * a).astype(BF16), wdn_ref[...])


def _final(x2, ys, ynt, g1, wmg, wps, wpn, wout, g2, wup, wdn, *, tm):
    n = x2.shape[0]
    const = lambda i: (0, 0)
    resident = lambda w: pl.BlockSpec(w.shape, const, pipeline_mode=pl.Buffered(1))
    return pl.pallas_call(
        _final_kernel,
        grid=(n // tm,),
        in_specs=[
            pl.BlockSpec((tm, D_MODEL), lambda i: (i, 0)),
            pl.BlockSpec((tm, SSM_WIDTH), lambda i: (i, 0)),
            pl.BlockSpec((NSA_WIDTH, tm), lambda i: (0, i)),
            resident(g1), resident(wmg), resident(wps), resident(wpn), resident(wout),
            resident(g2), resident(wup), resident(wdn),
        ],
        out_specs=pl.BlockSpec((tm, D_MODEL), lambda i: (i, 0)),
        out_shape=jax.ShapeDtypeStruct((n, D_MODEL), F32),
        compiler_params=pltpu.CompilerParams(
            dimension_semantics=("arbitrary",), vmem_limit_bytes=VMEM_LIMIT_BYTES),
        name="final",
    )(x2, ys, ynt, g1, wmg, wps, wpn, wout, g2, wup, wdn)


def _pack_in_proj(w_in):
    o1 = SSM_WIDTH
    o2 = o1 + NSA_WIDTH
    o3 = o2 + KV_WIDTH
    o4 = o3 + 3 * NSA_HEADS
    w_u, w_q, w_kv, w_gl = w_in[:, :o1], w_in[:, o1:o2], w_in[:, o2:o3], w_in[:, o3:o4]
    w_mg = w_in[:, o4:]
    kv = w_kv.reshape(D_MODEL, 3, 2, NSA_KV_GROUPS, HEAD_DIM)
    groups = range(NSA_KV_GROUPS)
    cpack = [jnp.concatenate([kv[:, 0, 0, g], kv[:, 0, 1, g]], axis=1) for g in groups]
    kpack = [jnp.concatenate([kv[:, 1, 0, g], kv[:, 2, 0, g]], axis=1) for g in groups]
    wa = jnp.concatenate([w_u] + cpack + kpack, axis=1)
    vrows = [jnp.concatenate([kv[:, 1, 1, g], kv[:, 2, 1, g]], axis=1) for g in range(NSA_KV_GROUPS)]
    gl = w_gl.reshape(D_MODEL, 3, NSA_KV_GROUPS, HEADS_PER_GROUP)
    pad = jnp.zeros((D_MODEL, GATE_ROWS - 3 * HEADS_PER_GROUP), w_in.dtype)
    glrows = [jnp.concatenate([gl[:, 0, g], gl[:, 1, g], gl[:, 2, g], pad], axis=1) for g in range(NSA_KV_GROUPS)]
    wbt = jnp.concatenate([w_q] + vrows + glrows, axis=1).T
    return wa.astype(BF16), wbt.astype(BF16), w_mg.astype(BF16)


def _pack_compress(pe_k, pe_v, w1k, w1v):
    pe = jnp.concatenate([pe_k, pe_v], axis=1)
    w1k = w1k.reshape(CMP_BLOCK, HEAD_DIM, CMP_HIDDEN)
    w1v = w1v.reshape(CMP_BLOCK, HEAD_DIM, CMP_HIDDEN)
    zero = jnp.zeros_like(w1k)
    w1 = jnp.concatenate([jnp.concatenate([w1k, zero], axis=2), jnp.concatenate([zero, w1v], axis=2)], axis=1)
    return pe, w1.astype(BF16)


def _pack_s5(a_re, a_im, log_dt, b_re, b_im, c_re, c_im):
    dt = jnp.exp(log_dt)[:, None]
    mag = jnp.exp(a_re * dt)
    abar_re = mag * jnp.cos(a_im * dt)
    abar_im = mag * jnp.sin(a_im * dt)
    den = a_re * a_re + a_im * a_im
    nr = abar_re - 1.0
    fr = (nr * a_re + abar_im * a_im) / den
    fi = (abar_im * a_re - nr * a_im) / den
    bbar_re = fr[..., None] * b_re - fi[..., None] * b_im
    bbar_im = fr[..., None] * b_im + fi[..., None] * b_re
    eye = jnp.eye(SSM_GROUPS, dtype=F32)

    def in_blocks(m):
        return (m.transpose(0, 2, 1)[:, :, None, :] * eye[:, None, :, None]).reshape(SSM_WIDTH, SSM_STATES)

    def out_blocks(m):
        return (m.transpose(0, 2, 1)[:, :, None, :] * eye[:, None, :, None]).reshape(SSM_STATES, SSM_WIDTH)

    bblk = jnp.concatenate([in_blocks(bbar_re), in_blocks(bbar_im)], axis=1).astype(BF16)
    cblk = jnp.concatenate([out_blocks(c_re), -out_blocks(c_im)], axis=0).astype(BF16)
    abar = jnp.zeros((8, SSM_STATES), F32)
    abar = abar.at[0].set(abar_re.reshape(-1)).at[1].set(abar_im.reshape(-1))
    return bblk, cblk, abar


def kernel(x, norm1_g, w_in, ssm_a_re, ssm_a_im, ssm_log_dt, ssm_b_re, ssm_b_im, ssm_c_re, ssm_c_im, ssm_d, ssm_w_glu, ssm_b_glu, cmp_pe_k, cmp_pe_v, cmp_wk1, cmp_wk2, cmp_wv1, cmp_wv2, q_norm_g, k_norm_g, w_proj_ssm, w_proj_nsa, w_out, norm2_g, w_up, w_down):
    batch, seq, d_model = x.shape
    assert d_model == D_MODEL and seq % 512 == 0
    n = batch * seq
    tq = 128
    depth = w_in.shape[0]
    head = jnp.arange(1, NSA_HEADS + 1, dtype=F32)
    slopes = jnp.exp2(-8.0 * head / NSA_HEADS).reshape(NSA_KV_GROUPS, 1, HEADS_PER_GROUP, 1)
    slopes = jnp.broadcast_to(slopes, (NSA_KV_GROUPS, 8, HEADS_PER_GROUP, tq)).reshape(NSA_KV_GROUPS, 8, -1)

    x2 = x.reshape(n, D_MODEL)
    for l in range(depth):
        wa, wbt, wmg = _pack_in_proj(w_in[l])
        g1 = norm1_g[l].reshape(1, D_MODEL)
        u, cp, kp, qt, vt, glt = _in_proj(x2, g1, wa, wbt, tm=512)

        bblk, cblk, abar = _pack_s5(ssm_a_re[l], ssm_a_im[l], ssm_log_dt[l], ssm_b_re[l], ssm_b_im[l],
                                    ssm_c_re[l], ssm_c_im[l])
        ys = _s5(u, bblk, cblk, abar, ssm_d[l].reshape(1, SSM_WIDTH), ssm_w_glu[l].astype(BF16),
                 ssm_b_glu[l].reshape(1, SSM_WIDTH), batch=batch, seq=seq, tc=256)

        qg = jnp.broadcast_to(q_norm_g[l].reshape(HEAD_DIM, 1), (HEAD_DIM, tq))
        kg = jnp.zeros((8, HEAD_DIM), F32).at[0:3].set(k_norm_g[l])
        pe, w1 = _pack_compress(cmp_pe_k[l], cmp_pe_v[l], cmp_wk1[l], cmp_wv1[l])
        ynt = _nsa(qt, cp, kp, vt, glt, slopes, qg, kg, pe, w1,
                   cmp_wk2[l].astype(BF16), cmp_wv2[l].T.astype(BF16), batch=batch, seq=seq, tq=tq)

        x2 = _final(x2, ys, ynt, g1, wmg, w_proj_ssm[l].astype(BF16), w_proj_nsa[l].astype(BF16),
                    w_out[l].astype(BF16), norm2_g[l].reshape(1, D_MODEL), w_up[l].astype(BF16),
                    w_down[l].astype(BF16), tm=256)
    return x2.reshape(batch, seq, D_MODEL)
```

```python
import functools

import jax
import jax.numpy as jnp
from jax import lax
from jax.experimental import pallas as pl
from jax.experimental.pallas import tpu as pltpu

D_MODEL = 1024
SSM_WIDTH = D_MODEL // 2
SSM_GROUP = 16
SSM_GROUPS = SSM_WIDTH // SSM_GROUP
SSM_STATE = 64
SSM_STATES = SSM_GROUPS * SSM_STATE
NSA_HEADS = 8
NSA_KV_GROUPS = 2
HEADS_PER_GROUP = NSA_HEADS // NSA_KV_GROUPS
HEAD_DIM = 64
NSA_WIDTH = NSA_HEADS * HEAD_DIM
CMP_STRIDE = 16
CMP_BLOCK = 2 * CMP_STRIDE
CMP_HIDDEN = 2 * HEAD_DIM
SEL_BLOCK = 64
SEL_TOP_K = 16
WINDOW = 512
D_FF = 4 * D_MODEL
N_MIXERS = 2
KV_WIDTH = 3 * 2 * NSA_KV_GROUPS * HEAD_DIM
EPS = 1e-6
NEG_INF = -1e30
FORCE_BONUS = 1e4

LANES = 128
VMEM_LIMIT_BYTES = 56 * 1024 * 1024

GATE_ROWS = 16

F32 = jnp.float32
BF16 = jnp.bfloat16


def _dot(a, b):
    return jnp.dot(a, b, preferred_element_type=F32)


def _dot_nt(a, b):
    return lax.dot_general(a, b, (((1,), (1,)), ((), ())), preferred_element_type=F32)


def _dot_tn(a, b):
    return lax.dot_general(a, b, (((0,), (0,)), ((), ())), preferred_element_type=F32)


def _rms_rows(x, g):
    ms = jnp.mean(x * x, axis=-1, keepdims=True)
    return x * lax.rsqrt(ms + EPS) * g


def _inproj_kernel(x_ref, g_ref, wa_ref, wbt_ref, u_ref, cp_ref, kp_ref, qt_ref, vt_ref, glt_ref):
    h = _rms_rows(x_ref[...], g_ref[...]).astype(BF16)
    a = _dot(h, wa_ref[...])
    u_ref[...] = a[:, :SSM_WIDTH]
    cp_ref[...] = a[:, SSM_WIDTH:SSM_WIDTH + 4 * HEAD_DIM]
    kp_ref[...] = a[:, SSM_WIDTH + 4 * HEAD_DIM:]
    bt = _dot_nt(wbt_ref[...], h)
    qt_ref[...] = bt[:NSA_WIDTH]
    vt_ref[...] = bt[NSA_WIDTH:NSA_WIDTH + 4 * HEAD_DIM]
    glt_ref[...] = bt[NSA_WIDTH + 4 * HEAD_DIM:]


def _in_proj(x2, g1, wa, wbt, *, tm):
    n = x2.shape[0]
    nb = wbt.shape[0]
    const = lambda i: (0, 0)
    return pl.pallas_call(
        _inproj_kernel,
        grid=(n // tm,),
        in_specs=[
            pl.BlockSpec((tm, D_MODEL), lambda i: (i, 0)),
            pl.BlockSpec((1, D_MODEL), const),
            pl.BlockSpec(wa.shape, const),
            pl.BlockSpec(wbt.shape, const),
        ],
        out_specs=[
            pl.BlockSpec((tm, SSM_WIDTH), lambda i: (i, 0)),
            pl.BlockSpec((tm, 4 * HEAD_DIM), lambda i: (i, 0)),
            pl.BlockSpec((tm, 4 * HEAD_DIM), lambda i: (i, 0)),
            pl.BlockSpec((NSA_WIDTH, tm), lambda i: (0, i)),
            pl.BlockSpec((4 * HEAD_DIM, tm), lambda i: (0, i)),
            pl.BlockSpec((NSA_KV_GROUPS * GATE_ROWS, tm), lambda i: (0, i)),
        ],
        out_shape=[
            jax.ShapeDtypeStruct((n, SSM_WIDTH), F32),
            jax.ShapeDtypeStruct((n, 4 * HEAD_DIM), F32),
            jax.ShapeDtypeStruct((n, 4 * HEAD_DIM), F32),
            jax.ShapeDtypeStruct((NSA_WIDTH, n), F32),
            jax.ShapeDtypeStruct((4 * HEAD_DIM, n), F32),
            jax.ShapeDtypeStruct((NSA_KV_GROUPS * GATE_ROWS, n), F32),
        ],
        compiler_params=pltpu.CompilerParams(
            dimension_semantics=("arbitrary",), vmem_limit_bytes=VMEM_LIMIT_BYTES),
        name="in_proj",
    )(x2, g1, wa, wbt)


SCAN_SLAB = 512
SCAN_UNROLL = 8


def _s5_kernel(u_ref, bblk_ref, cblk_ref, abar_ref, d_ref, wglu_ref, bglu_ref, y_ref, bx_s, carry_s, *, tc):
    @pl.when(pl.program_id(1) == 0)
    def _reset():
        carry_s[...] = jnp.zeros_like(carry_s)

    u = u_ref[...]
    bx_s[...] = _dot(u.astype(BF16), bblk_ref[...])

    for slab in range(SSM_STATES // SCAN_SLAB):
        re_cols = pl.ds(slab * SCAN_SLAB, SCAN_SLAB)
        im_cols = pl.ds(SSM_STATES + slab * SCAN_SLAB, SCAN_SLAB)
        a_re = abar_ref[0:1, re_cols]
        a_im = abar_ref[1:2, re_cols]

        def step(t8, carry):
            xr, xi = carry
            for k in range(SCAN_UNROLL):
                row = pl.ds(t8 * SCAN_UNROLL + k, 1)
                nr = a_re * xr - a_im * xi + bx_s[row, re_cols]
                ni = a_re * xi + a_im * xr + bx_s[row, im_cols]
                bx_s[row, re_cols] = nr
                bx_s[row, im_cols] = ni
                xr, xi = nr, ni
            return xr, xi

        xr, xi = lax.fori_loop(0, tc // SCAN_UNROLL, step, (carry_s[0:1, re_cols], carry_s[0:1, im_cols]))
        carry_s[0:1, re_cols] = xr
        carry_s[0:1, im_cols] = xi

    y = _dot(bx_s[...].astype(BF16), cblk_ref[...]) + d_ref[...] * u
    yg = jax.nn.gelu(y)
    z = _dot(yg.astype(BF16), wglu_ref[...]) + bglu_ref[...]
    y_ref[...] = (yg * jax.nn.sigmoid(z)).astype(y_ref.dtype)


def _s5(u, bblk, cblk, abar, d_row, wglu, bglu, *, batch, seq, tc):
    nchunk = seq // tc
    const = lambda b, c: (0, 0)
    return pl.pallas_call(
        functools.partial(_s5_kernel, tc=tc),
        grid=(batch, nchunk),
        in_specs=[
            pl.BlockSpec((tc, SSM_WIDTH), lambda b, c: (b * nchunk + c, 0)),
            pl.BlockSpec(bblk.shape, const),
            pl.BlockSpec(cblk.shape, const),
            pl.BlockSpec(abar.shape, const),
            pl.BlockSpec(d_row.shape, const),
            pl.BlockSpec(wglu.shape, const),
            pl.BlockSpec(bglu.shape, const),
        ],
        out_specs=pl.BlockSpec((tc, SSM_WIDTH), lambda b, c: (b * nchunk + c, 0)),
        out_shape=jax.ShapeDtypeStruct((batch * seq, SSM_WIDTH), BF16),
        scratch_shapes=[
            pltpu.VMEM((tc, 2 * SSM_STATES), F32),
            pltpu.VMEM((8, 2 * SSM_STATES), F32),
        ],
        compiler_params=pltpu.CompilerParams(
            dimension_semantics=("arbitrary", "arbitrary"), vmem_limit_bytes=VMEM_LIMIT_BYTES),
        name="s5",
    )(u, bblk, cblk, abar, d_row, wglu, bglu)


KEY_TILE = 128
CMP_TILE = 128
SEL_CHAINS = 4
WORD_BITS = 16


def _nsa_kernel(qt_ref, cp_ref, kp_ref, vt_ref, glt_ref, slope_ref, qg_ref, kg_ref, pe_ref,
                w1_ref, w2k_ref, w2vt_ref, out_ref,
                ks_s, kw_s, kc_s, vct_s, vst_s, vwt_s, sval_s, bias_s, cmat_s, acc_s, words_s, tiles_s, count_s, amat_s, cvis_s, imp_s, *, seq, tq):
    assert tq == KEY_TILE
    i = pl.program_id(2)
    n_cmp = seq // CMP_STRIDE
    n_sel = seq // SEL_BLOCK
    k_top = min(SEL_TOP_K, n_sel)
    hq = HEADS_PER_GROUP * tq

    @pl.when(i == 0)
    def _prepare_keys_and_values():
        kg = kg_ref[...]
        ks_s[...] = _rms_rows(kp_ref[:, 0:HEAD_DIM], kg[1:2]).astype(BF16)
        kw_s[...] = _rms_rows(kp_ref[:, HEAD_DIM:2 * HEAD_DIM], kg[2:3]).astype(BF16)
        vst_s[...] = vt_ref[0:HEAD_DIM, :].astype(BF16)
        vwt_s[...] = vt_ref[HEAD_DIM:2 * HEAD_DIM, :].astype(BF16)

        lo = jnp.zeros((n_cmp, 2 * CMP_HIDDEN), F32)
        hi = jnp.zeros((n_cmp, 2 * CMP_HIDDEN), F32)
        for tt in range(CMP_STRIDE):
            tok = cp_ref[pl.ds(tt, n_cmp, stride=CMP_STRIDE), :]
            lo = lo + _dot((tok + pe_ref[tt:tt + 1, :]).astype(BF16), w1_ref[tt])
            hi = hi + _dot((tok + pe_ref[CMP_STRIDE + tt:CMP_STRIDE + tt + 1, :]).astype(BF16),
                           w1_ref[CMP_STRIDE + tt])
        hidden = jax.nn.gelu(lo + pltpu.roll(hi, n_cmp - 1, 0)).astype(BF16)
        kc_s[...] = _rms_rows(_dot(hidden[:, :CMP_HIDDEN], w2k_ref[...]), kg[0:1]).astype(BF16)
        vct_s[...] = _dot_nt(w2vt_ref[...], hidden[:, CMP_HIDDEN:]).astype(BF16)

        r = lax.broadcasted_iota(jnp.int32, (KEY_TILE, hq), 0)
        a = lax.broadcasted_iota(jnp.int32, (KEY_TILE, hq), 1) & (tq - 1)
        alibi = slope_ref[0, 0:1, :] * r.astype(F32)
        cmat_s[0] = alibi
        cmat_s[1] = jnp.where(r <= a, alibi, NEG_INF)
        cmat_s[2] = jnp.where(r > a, alibi, NEG_INF)

        jrow = lax.broadcasted_iota(jnp.int32, (n_cmp, hq), 0)
        amat_s[...] = slope_ref[0, 0:1, :] * (jrow * CMP_STRIDE + (CMP_BLOCK - 1)).astype(F32)
        cvis_s[...] = (r * CMP_STRIDE + (CMP_BLOCK - 1)) - a
        imp_s[...] = jnp.zeros_like(imp_s)

    qs = i * tq
    lane = lax.broadcasted_iota(jnp.int32, (1, hq), 1)
    t_row = qs + (lane & (tq - 1))
    slope_row = slope_ref[0, 0:1, :]

    qg = qg_ref[...]
    heads = []
    for h in range(HEADS_PER_GROUP):
        qh = qt_ref[h * HEAD_DIM:(h + 1) * HEAD_DIM, :]
        ms = jnp.mean(qh * qh, axis=0, keepdims=True)
        heads.append((qh * lax.rsqrt(ms + EPS) * qg * (HEAD_DIM ** -0.5)).astype(BF16))
    qst = jnp.concatenate(heads, axis=1)

    def cmp_branch(n_tiles):
        def run():
            scores = []
            for jt in range(n_tiles):
                rows = pl.ds(jt * CMP_TILE, CMP_TILE)
                s = _dot(kc_s[rows, :], qst) + amat_s[rows, :]
                visible = cvis_s[...] <= qs - CMP_STRIDE * CMP_TILE * jt
                scores.append(jnp.where(visible, s, NEG_INF))
            m = functools.reduce(jnp.maximum, [jnp.max(s, axis=0, keepdims=True) for s in scores])
            shift = jnp.where(m > 0.5 * NEG_INF, m, -NEG_INF)
            exps = [jnp.exp(s - shift) for s in scores]
            l = functools.reduce(lambda x, y: x + y, [jnp.sum(e, axis=0, keepdims=True) for e in exps])
            r = jnp.where(l > 0.0, 1.0 / l, 0.0)
            acc = jnp.zeros((HEAD_DIM, hq), F32)
            for jt, e in enumerate(exps):
                p = e * r
                imp = p[:, 0:tq]
                for h in range(1, HEADS_PER_GROUP):
                    imp = imp + p[:, h * tq:(h + 1) * tq]
                imp_s[pl.ds(jt * CMP_TILE, CMP_TILE), :] = imp
                acc = acc + _dot(vct_s[:, pl.ds(jt * CMP_TILE, CMP_TILE)], p.astype(BF16))
            return acc
        return run

    n_cmp_live = (qs + tq - CMP_STRIDE) // CMP_STRIDE
    cmp_tiles = n_cmp // CMP_TILE
    if cmp_tiles >= 2:
        o_cmp = lax.cond(n_cmp_live <= (cmp_tiles // 2) * CMP_TILE, cmp_branch(cmp_tiles // 2), cmp_branch(cmp_tiles))
    else:
        o_cmp = cmp_branch(cmp_tiles)()

    per_block = SEL_BLOCK // CMP_STRIDE
    chunk = [imp_s[pl.ds(c, n_sel, stride=per_block), :] for c in range(per_block)]
    imp_sel = chunk[0]
    for c in range(1, per_block):
        imp_sel = imp_sel + chunk[c]
    blk = lax.broadcasted_iota(jnp.int32, (n_sel, tq), 0)
    imp_sel = imp_sel + jnp.where(blk == 0, 0.0, pltpu.roll(chunk[per_block - 1], 1, 0))
    cur = jnp.right_shift(t_row[:, 0:tq], SEL_BLOCK.bit_length() - 1)
    forced = (blk == 0) | (blk == cur) | (blk == cur - 1)
    sval = jnp.where(blk <= cur, imp_sel + jnp.where(forced, FORCE_BONUS, 0.0), NEG_INF)
    sval_s[...] = sval

    def rank_step(m, rank):
        row = sval_s[pl.ds(m, 1), :]
        ahead = jnp.where(row > sval, 1.0, jnp.where(row == sval, jnp.where(blk > m, 1.0, 0.0), 0.0))
        return rank + ahead

    n_live = jnp.minimum(n_sel, (qs + tq) // SEL_BLOCK)
    rank = lax.fori_loop(0, n_live, rank_step, jnp.zeros((n_sel, tq), F32))
    chosen = rank < k_top
    bias = jnp.where(chosen, 0.0, NEG_INF)
    bias_s[...] = jnp.concatenate([bias] * HEADS_PER_GROUP, axis=1)

    blocks_per_tile = KEY_TILE // SEL_BLOCK
    hit = jnp.where(chosen & (blk < blocks_per_tile * (i + 1)), 1.0, 0.0)
    blk_hit = jnp.max(hit, axis=1, keepdims=True)
    weight = jnp.left_shift(1, lax.broadcasted_iota(jnp.int32, (n_sel, 1), 0) & (WORD_BITS - 1)).astype(F32)
    packed = blk_hit * weight
    for w in range(n_sel // WORD_BITS):
        word = jnp.sum(packed[w * WORD_BITS:(w + 1) * WORD_BITS], axis=0, keepdims=True).astype(jnp.int32)
        words_s[w] = word[0, 0]

    tiles_per_word = WORD_BITS // blocks_per_tile
    count_s[0] = 0
    for w in range(n_sel // WORD_BITS):
        @pl.when(words_s[w] != 0)
        def _append_tiles_of_word():
            word = words_s[w]
            count = count_s[0]
            for k in range(tiles_per_word):
                bits = jnp.right_shift(word, blocks_per_tile * k) & ((1 << blocks_per_tile) - 1)
                tiles_s[count] = w * tiles_per_word + k
                count = count + jnp.where(bits != 0, 1, 0)
            count_s[0] = count

    n_active = count_s[0]
    for c in range(SEL_CHAINS):
        tiles_s[n_active + c] = -1

    def tiles_update(kts, k_s, vt_s, cidxs, biases_per_tile, ms, ls, accs):
        k0s = [pl.multiple_of(kt * KEY_TILE, KEY_TILE) for kt in kts]
        scores = [_dot(k_s[pl.ds(k0, KEY_TILE), :], qst) + cmat_s[cidx] for k0, cidx in zip(k0s, cidxs)]
        new_ms, new_ls, alphas, probs = [], [], [], []
        for s, k0, biases, m, l in zip(scores, k0s, biases_per_tile, ms, ls):
            rv = slope_row * (t_row - k0).astype(F32)
            rows = KEY_TILE // len(biases)
            parts = [s[b * rows:(b + 1) * rows] for b in range(len(biases))]
            m_new = m
            for part, bias_row in zip(parts, biases):
                m_new = jnp.maximum(m_new, jnp.max(part, axis=0, keepdims=True) + (bias_row - rv))
            seen = m_new > 0.5 * NEG_INF
            ps = []
            for part, bias_row in zip(parts, biases):
                shift = jnp.where(seen, m_new + (rv - bias_row), -NEG_INF)
                ps.append(jnp.exp(part - shift))
            p = jnp.concatenate(ps, axis=0) if len(ps) > 1 else ps[0]
            alpha = jnp.exp(m - m_new)
            new_ms.append(m_new)
            new_ls.append(alpha * l + jnp.sum(p, axis=0, keepdims=True))
            alphas.append(alpha)
            probs.append(p.astype(BF16))
        new_accs = [alpha * acc + _dot(vt_s[:, pl.ds(k0, KEY_TILE)], p)
                    for alpha, acc, k0, p in zip(alphas, accs, k0s, probs)]
        return new_ms, new_ls, new_accs

    def merge(ms, ls, accs):
        m = functools.reduce(jnp.maximum, ms)
        ws = [jnp.exp(mc - m) for mc in ms]
        l = functools.reduce(lambda a, b: a + b, [w * lc for w, lc in zip(ws, ls)])
        acc = functools.reduce(lambda a, b: a + b, [w * ac for w, ac in zip(ws, accs)])
        return acc * (1.0 / l)

    m_init = jnp.full((1, hq), NEG_INF, F32)
    l_init = jnp.zeros((1, hq), F32)
    dead_row = jnp.full((1, hq), NEG_INF, F32)
    live_row = jnp.zeros((1, hq), F32)

    for c in range(SEL_CHAINS):
        acc_s[c] = jnp.zeros((HEAD_DIM, hq), F32)

    def sel_step(j, carry):
        ms, ls = carry
        entries = [tiles_s[j * SEL_CHAINS + c] for c in range(SEL_CHAINS)]
        kts = [jnp.maximum(e, 0) for e in entries]
        biases = [[jnp.where(e < 0, dead_row, bias_s[pl.ds(blocks_per_tile * kt + b, 1), :])
                   for b in range(blocks_per_tile)] for e, kt in zip(entries, kts)]
        cidxs = [jnp.where(e == i, 1, 0) for e in entries]
        new_ms, new_ls, new_accs = tiles_update(kts, ks_s, vst_s, cidxs, biases, ms, ls,
                                                [acc_s[c] for c in range(SEL_CHAINS)])
        for c in range(SEL_CHAINS):
            acc_s[c] = new_accs[c]
        return tuple(new_ms), tuple(new_ls)

    ms, ls = lax.fori_loop(0, (n_active + SEL_CHAINS - 1) // SEL_CHAINS, sel_step,
                           ((m_init,) * SEL_CHAINS, (l_init,) * SEL_CHAINS))
    o_sel = merge(ms, ls, [acc_s[c] for c in range(SEL_CHAINS)])

    far = WINDOW // KEY_TILE
    kts = [i - d for d in range(far + 1)]
    o_win = merge(*tiles_update(
        [jnp.maximum(kt, 0) for kt in kts], kw_s, vwt_s,
        [1 if d == 0 else (2 if d == far else 0) for d in range(far + 1)],
        [[jnp.where(kt < 0, dead_row, live_row)] for kt in kts],
        [m_init] * (far + 1), [l_init] * (far + 1), [jnp.zeros((HEAD_DIM, hq), F32)] * (far + 1)))

    gl = glt_ref[...]

    def gate_row(branch):
        rows = [jax.nn.sigmoid(gl[branch * HEADS_PER_GROUP + h:branch * HEADS_PER_GROUP + h + 1, :])
                for h in range(HEADS_PER_GROUP)]
        return jnp.concatenate(rows, axis=1)

    ot = gate_row(0) * o_cmp + gate_row(1) * o_sel + gate_row(2) * o_win
    out_ref[...] = jnp.concatenate(
        [ot[:, h * tq:(h + 1) * tq] for h in range(HEADS_PER_GROUP)], axis=0).astype(out_ref.dtype)


def _nsa(qt, cp, kp, vt, glt, slopes, qg, kg, pe, w1, w2k, w2vt, *, batch, seq, tq):
    nq = seq // tq
    n_cmp = seq // CMP_STRIDE
    n_sel = seq // SEL_BLOCK
    hq = HEADS_PER_GROUP * tq
    const2 = lambda b, g, i: (0, 0)
    const3 = lambda b, g, i: (0, 0, 0)
    return pl.pallas_call(
        functools.partial(_nsa_kernel, seq=seq, tq=tq),
        grid=(batch, NSA_KV_GROUPS, nq),
        in_specs=[
            pl.BlockSpec((HEADS_PER_GROUP * HEAD_DIM, tq), lambda b, g, i: (g, b * nq + i)),
            pl.BlockSpec((seq, 2 * HEAD_DIM), lambda b, g, i: (b, g)),
            pl.BlockSpec((seq, 2 * HEAD_DIM), lambda b, g, i: (b, g)),
            pl.BlockSpec((2 * HEAD_DIM, seq), lambda b, g, i: (g, b)),
            pl.BlockSpec((GATE_ROWS, tq), lambda b, g, i: (g, b * nq + i)),
            pl.BlockSpec((1, 8, hq), lambda b, g, i: (g, 0, 0)),
            pl.BlockSpec(qg.shape, const2),
            pl.BlockSpec(kg.shape, const2),
            pl.BlockSpec(pe.shape, const2),
            pl.BlockSpec(w1.shape, const3),
            pl.BlockSpec(w2k.shape, const2),
            pl.BlockSpec(w2vt.shape, const2),
        ],
        out_specs=pl.BlockSpec((HEADS_PER_GROUP * HEAD_DIM, tq), lambda b, g, i: (g, b * nq + i)),
        out_shape=jax.ShapeDtypeStruct((NSA_WIDTH, batch * seq), BF16),
        scratch_shapes=[
            pltpu.VMEM((seq, HEAD_DIM), BF16),
            pltpu.VMEM((seq, HEAD_DIM), BF16),
            pltpu.VMEM((n_cmp, HEAD_DIM), BF16),
            pltpu.VMEM((HEAD_DIM, n_cmp), BF16),
            pltpu.VMEM((HEAD_DIM, seq), BF16),
            pltpu.VMEM((HEAD_DIM, seq), BF16),
            pltpu.VMEM((n_sel, tq), F32),
            pltpu.VMEM((n_sel, hq), F32),
            pltpu.VMEM((3, KEY_TILE, hq), F32),
            pltpu.VMEM((SEL_CHAINS, HEAD_DIM, hq), F32),
            pltpu.SMEM((n_sel // WORD_BITS,), jnp.int32),
            pltpu.SMEM((seq // KEY_TILE + SEL_CHAINS,), jnp.int32),
            pltpu.SMEM((1,), jnp.int32),
            pltpu.VMEM((n_cmp, hq), F32),
            pltpu.VMEM((CMP_TILE, hq), jnp.int32),
            pltpu.VMEM((n_cmp, tq), F32),
        ],
        compiler_params=pltpu.CompilerParams(
            dimension_semantics=("arbitrary", "arbitrary", "arbitrary"), vmem_limit_bytes=VMEM_LIMIT_BYTES),
        name="nsa",
    )(qt, cp, kp, vt, glt, slopes, qg, kg, pe, w1, w2k, w2vt)


def _final_kernel(x_ref, ys_ref, ynt_ref, g1_ref, wmg_ref, wps_ref, wpn_ref, wout_ref, g2_ref, wup_ref, wdn_ref,
                  o_ref):
    x = x_ref[...]
    h = _rms_rows(x, g1_ref[...]).astype(BF16)
    mg = _dot(h, wmg_ref[...])
    ps = _dot(ys_ref[...], wps_ref[...])
    pn = _dot_tn(ynt_ref[...], wpn_ref[...])
    merged = jax.nn.sigmoid(mg[:, :D_MODEL]) * ps + jax.nn.sigmoid(mg[:, D_MODEL:]) * pn
    x1 = x + _dot(merged.astype(BF16), wout_ref[...])
    h2 = _rms_rows(x1, g2_ref[...]).astype(BF16)
    a = jnp.maximum(_dot(h2, wup_ref[...]), 0.0)
    o_ref[...] = x1 + _dot((a * a).astype(BF16), wdn_ref[...])


def _final(x2, ys, ynt, g1, wmg, wps, wpn, wout, g2, wup, wdn, *, tm):
    n = x2.shape[0]
    const = lambda i: (0, 0)
    resident = lambda w: pl.BlockSpec(w.shape, const, pipeline_mode=pl.Buffered(1))
    return pl.pallas_call(
        _final_kernel,
        grid=(n // tm,),
        in_specs=[
            pl.BlockSpec((tm, D_MODEL), lambda i: (i, 0)),
            pl.BlockSpec((tm, SSM_WIDTH), lambda i: (i, 0)),
            pl.BlockSpec((NSA_WIDTH, tm), lambda i: (0, i)),
            resident(g1), resident(wmg), resident(wps), resident(wpn), resident(wout),
            resident(g2), resident(wup), resident(wdn),
        ],
        out_specs=pl.BlockSpec((tm, D_MODEL), lambda i: (i, 0)),
        out_shape=jax.ShapeDtypeStruct((n, D_MODEL), F32),
        compiler_params=pltpu.CompilerParams(
            dimension_semantics=("arbitrary",), vmem_limit_bytes=VMEM_LIMIT_BYTES),
        name="final",
    )(x2, ys, ynt, g1, wmg, wps, wpn, wout, g2, wup, wdn)


def _pack_in_proj(w_in):
    o1 = SSM_WIDTH
    o2 = o1 + NSA_WIDTH
    o3 = o2 + KV_WIDTH
    o4 = o3 + 3 * NSA_HEADS
    w_u, w_q, w_kv, w_gl = w_in[:, :o1], w_in[:, o1:o2], w_in[:, o2:o3], w_in[:, o3:o4]
    w_mg = w_in[:, o4:]
    kv = w_kv.reshape(D_MODEL, 3, 2, NSA_KV_GROUPS, HEAD_DIM)
    groups = range(NSA_KV_GROUPS)
    cpack = [jnp.concatenate([kv[:, 0, 0, g], kv[:, 0, 1, g]], axis=1) for g in groups]
    kpack = [jnp.concatenate([kv[:, 1, 0, g], kv[:, 2, 0, g]], axis=1) for g in groups]
    wa = jnp.concatenate([w_u] + cpack + kpack, axis=1)
    vrows = [jnp.concatenate([kv[:, 1, 1, g], kv[:, 2, 1, g]], axis=1) for g in range(NSA_KV_GROUPS)]
    gl = w_gl.reshape(D_MODEL, 3, NSA_KV_GROUPS, HEADS_PER_GROUP)
    pad = jnp.zeros((D_MODEL, GATE_ROWS - 3 * HEADS_PER_GROUP), w_in.dtype)
    glrows = [jnp.concatenate([gl[:, 0, g], gl[:, 1, g], gl[:, 2, g], pad], axis=1) for g in range(NSA_KV_GROUPS)]
    wbt = jnp.concatenate([w_q] + vrows + glrows, axis=1).T
    return wa.astype(BF16), wbt.astype(BF16), w_mg.astype(BF16)


def _pack_compress(pe_k, pe_v, w1k, w1v):
    pe = jnp.concatenate([pe_k, pe_v], axis=1)
    w1k = w1k.reshape(CMP_BLOCK, HEAD_DIM, CMP_HIDDEN)
    w1v = w1v.reshape(CMP_BLOCK, HEAD_DIM, CMP_HIDDEN)
    zero = jnp.zeros_like(w1k)
    w1 = jnp.concatenate([jnp.concatenate([w1k, zero], axis=2), jnp.concatenate([zero, w1v], axis=2)], axis=1)
    return pe, w1.astype(BF16)


def _pack_s5(a_re, a_im, log_dt, b_re, b_im, c_re, c_im):
    dt = jnp.exp(log_dt)[:, None]
    mag = jnp.exp(a_re * dt)
    abar_re = mag * jnp.cos(a_im * dt)
    abar_im = mag * jnp.sin(a_im * dt)
    den = a_re * a_re + a_im * a_im
    nr = abar_re - 1.0
    fr = (nr * a_re + abar_im * a_im) / den
    fi = (abar_im * a_re - nr * a_im) / den
    bbar_re = fr[..., None] * b_re - fi[..., None] * b_im
    bbar_im = fr[..., None] * b_im + fi[..., None] * b_re
    eye = jnp.eye(SSM_GROUPS, dtype=F32)

    def in_blocks(m):
        return (m.transpose(0, 2, 1)[:, :, None, :] * eye[:, None, :, None]).reshape(SSM_WIDTH, SSM_STATES)

    def out_blocks(m):
        return (m.transpose(0, 2, 1)[:, :, None, :] * eye[:, None, :, None]).reshape(SSM_STATES, SSM_WIDTH)

    bblk = jnp.concatenate([in_blocks(bbar_re), in_blocks(bbar_im)], axis=1).astype(BF16)
    cblk = jnp.concatenate([out_blocks(c_re), -out_blocks(c_im)], axis=0).astype(BF16)
    abar = jnp.zeros((8, SSM_STATES), F32)
    abar = abar.at[0].set(abar_re.reshape(-1)).at[1].set(abar_im.reshape(-1))
    return bblk, cblk, abar


def kernel(x, norm1_g, w_in, ssm_a_re, ssm_a_im, ssm_log_dt, ssm_b_re, ssm_b_im, ssm_c_re, ssm_c_im, ssm_d, ssm_w_glu, ssm_b_glu, cmp_pe_k, cmp_pe_v, cmp_wk1, cmp_wk2, cmp_wv1, cmp_wv2, q_norm_g, k_norm_g, w_proj_ssm, w_proj_nsa, w_out, norm2_g, w_up, w_down):
    batch, seq, d_model = x.shape
    assert d_model == D_MODEL and seq % 512 == 0
    n = batch * seq
    tq = 128
    depth = w_in.shape[0]
    head = jnp.arange(1, NSA_HEADS + 1, dtype=F32)
    slopes = jnp.exp2(-8.0 * head / NSA_HEADS).reshape(NSA_KV_GROUPS, 1, HEADS_PER_GROUP, 1)
    slopes = jnp.broadcast_to(slopes, (NSA_KV_GROUPS, 8, HEADS_PER_GROUP, tq)).reshape(NSA_KV_GROUPS, 8, -1)

    x2 = x.reshape(n, D_MODEL)
    for l in range(depth):
        wa, wbt, wmg = _pack_in_proj(w_in[l])
        g1 = norm1_g[l].reshape(1, D_MODEL)
        u, cp, kp, qt, vt, glt = _in_proj(x2, g1, wa, wbt, tm=512)

        bblk, cblk, abar = _pack_s5(ssm_a_re[l], ssm_a_im[l], ssm_log_dt[l], ssm_b_re[l], ssm_b_im[l],
                                    ssm_c_re[l], ssm_c_im[l])
        ys = _s5(u, bblk, cblk, abar, ssm_d[l].reshape(1, SSM_WIDTH), ssm_w_glu[l].astype(BF16),
                 ssm_b_glu[l].reshape(1, SSM_WIDTH), batch=batch, seq=seq, tc=256)

        qg = jnp.broadcast_to(q_norm_g[l].reshape(HEAD_DIM, 1), (HEAD_DIM, tq))
        kg = jnp.zeros((8, HEAD_DIM), F32).at[0:3].set(k_norm_g[l])
        pe, w1 = _pack_compress(cmp_pe_k[l], cmp_pe_v[l], cmp_wk1[l], cmp_wv1[l])
        ynt = _nsa(qt, cp, kp, vt, glt, slopes, qg, kg, pe, w1,
                   cmp_wk2[l].astype(BF16), cmp_wv2[l].T.astype(BF16), batch=batch, seq=seq, tq=tq)

        x2 = _final(x2, ys, ynt, g1, wmg, w_proj_ssm[l].astype(BF16), w_proj_nsa[l].astype(BF16),
                    w_out[l].astype(BF16), norm2_g[l].reshape(1, D_MODEL), w_up[l].astype(BF16),
                    w_down[l].astype(BF16), tm=256)
    return x2.reshape(batch, seq, D_MODEL)
```

```python
import functools

import jax
import jax.numpy as jnp
from jax import lax
from jax.experimental import pallas as pl
from jax.experimental.pallas import tpu as pltpu

D_MODEL = 1024
SSM_WIDTH = D_MODEL // 2
SSM_GROUP = 16
SSM_GROUPS = SSM_WIDTH // SSM_GROUP
SSM_STATE = 64
SSM_STATES = SSM_GROUPS * SSM_STATE
NSA_HEADS = 8
NSA_KV_GROUPS = 2
HEADS_PER_GROUP = NSA_HEADS // NSA_KV_GROUPS
HEAD_DIM = 64
NSA_WIDTH = NSA_HEADS * HEAD_DIM
CMP_STRIDE = 16
CMP_BLOCK = 2 * CMP_STRIDE
CMP_HIDDEN = 2 * HEAD_DIM
SEL_BLOCK = 64
SEL_TOP_K = 16
WINDOW = 512
D_FF = 4 * D_MODEL
N_MIXERS = 2
KV_WIDTH = 3 * 2 * NSA_KV_GROUPS * HEAD_DIM
EPS = 1e-6
NEG_INF = -1e30
FORCE_BONUS = 1e4

LANES = 128
VMEM_LIMIT_BYTES = 56 * 1024 * 1024

GATE_ROWS = 16

F32 = jnp.float32
BF16 = jnp.bfloat16


def _dot(a, b):
    return jnp.dot(a, b, preferred_element_type=F32)


def _dot_nt(a, b):
    return lax.dot_general(a, b, (((1,), (1,)), ((), ())), preferred_element_type=F32)


def _dot_tn(a, b):
    return lax.dot_general(a, b, (((0,), (0,)), ((), ())), preferred_element_type=F32)


def _rms_rows(x, g):
    ms = jnp.mean(x * x, axis=-1, keepdims=True)
    return x * lax.rsqrt(ms + EPS) * g


def _inproj_kernel(x_ref, g_ref, wa_ref, wbt_ref, u_ref, cp_ref, kp_ref, qt_ref, vt_ref, glt_ref):
    h = _rms_rows(x_ref[...], g_ref[...]).astype(BF16)
    a = _dot(h, wa_ref[...])
    for k in range(SSM_WIDTH // LANES):
        u_ref[k] = a[:, k * LANES:(k + 1) * LANES]
    cp_ref[...] = a[:, SSM_WIDTH:SSM_WIDTH + 4 * HEAD_DIM]
    kp_ref[...] = a[:, SSM_WIDTH + 4 * HEAD_DIM:]
    bt = _dot_nt(wbt_ref[...], h)
    qt_ref[...] = bt[:NSA_WIDTH]
    vt_ref[...] = bt[NSA_WIDTH:NSA_WIDTH + 4 * HEAD_DIM]
    glt_ref[...] = bt[NSA_WIDTH + 4 * HEAD_DIM:]


def _in_proj(x2, g1, wa, wbt, *, tm):
    n = x2.shape[0]
    nb = wbt.shape[0]
    const = lambda i: (0, 0)
    return pl.pallas_call(
        _inproj_kernel,
        grid=(n // tm,),
        in_specs=[
            pl.BlockSpec((tm, D_MODEL), lambda i: (i, 0)),
            pl.BlockSpec((1, D_MODEL), const),
            pl.BlockSpec(wa.shape, const),
            pl.BlockSpec(wbt.shape, const),
        ],
        out_specs=[
            pl.BlockSpec((SSM_WIDTH // LANES, tm, LANES), lambda i: (0, i, 0)),
            pl.BlockSpec((tm, 4 * HEAD_DIM), lambda i: (i, 0)),
            pl.BlockSpec((tm, 4 * HEAD_DIM), lambda i: (i, 0)),
            pl.BlockSpec((NSA_WIDTH, tm), lambda i: (0, i)),
            pl.BlockSpec((4 * HEAD_DIM, tm), lambda i: (0, i)),
            pl.BlockSpec((NSA_KV_GROUPS * GATE_ROWS, tm), lambda i: (0, i)),
        ],
        out_shape=[
            jax.ShapeDtypeStruct((SSM_WIDTH // LANES, n, LANES), F32),
            jax.ShapeDtypeStruct((n, 4 * HEAD_DIM), F32),
            jax.ShapeDtypeStruct((n, 4 * HEAD_DIM), F32),
            jax.ShapeDtypeStruct((NSA_WIDTH, n), F32),
            jax.ShapeDtypeStruct((4 * HEAD_DIM, n), F32),
            jax.ShapeDtypeStruct((NSA_KV_GROUPS * GATE_ROWS, n), F32),
        ],
        compiler_params=pltpu.CompilerParams(
            dimension_semantics=("arbitrary",), vmem_limit_bytes=VMEM_LIMIT_BYTES),
        name="in_proj",
    )(x2, g1, wa, wbt)


SCAN_SEGMENTS = 8
SCAN_SLAB = 512
STATE_HALF = SSM_STATES // 2
U_SLABS = SSM_WIDTH // LANES


def _s5_kernel(u_ref, bblk_ref, cblk_ref, acoef_ref, apow_re_ref, apow_im_ref, d_ref, wglu_ref, bglu_ref, y_ref,
               up_s, bx_s, carry_s, cseg_s, yp_s, *, tc):
    seg = tc // SCAN_SEGMENTS

    @pl.when(pl.program_id(1) == 0)
    def _reset():
        carry_s[...] = jnp.zeros_like(carry_s)

    for tl in range(seg):
        for k in range(U_SLABS):
            up_s[tl * SCAN_SEGMENTS:(tl + 1) * SCAN_SEGMENTS, k * LANES:(k + 1) * LANES] = (
                u_ref[k, pl.ds(tl, SCAN_SEGMENTS, stride=seg), :])
    up = up_s[...]
    half_w = SSM_WIDTH // 2
    for h in range(2):
        bx_s[:, h * 2 * STATE_HALF:(h + 1) * 2 * STATE_HALF] = _dot(
            up[:, h * half_w:(h + 1) * half_w].astype(BF16), bblk_ref[h])

    for slab in range(SSM_STATES // SCAN_SLAB):
        base = (slab * SCAN_SLAB // STATE_HALF) * 2 * STATE_HALF + (slab * SCAN_SLAB) % STATE_HALF
        re_cols = pl.ds(base, SCAN_SLAB)
        im_cols = pl.ds(base + STATE_HALF, SCAN_SLAB)
        nat = pl.ds(slab * SCAN_SLAB, SCAN_SLAB)
        a_re = acoef_ref[0:1, nat]
        a_im = acoef_ref[1:2, nat]
        al_re = acoef_ref[2:3, nat]
        al_im = acoef_ref[3:4, nat]

        def scan_step(tl, carry):
            xr, xi = carry
            rows = pl.ds(pl.multiple_of(tl * SCAN_SEGMENTS, SCAN_SEGMENTS), SCAN_SEGMENTS)
            nr = a_re * xr - a_im * xi + bx_s[rows, re_cols]
            ni = a_re * xi + a_im * xr + bx_s[rows, im_cols]
            bx_s[rows, re_cols] = nr
            bx_s[rows, im_cols] = ni
            return nr, ni

        zero = jnp.zeros((SCAN_SEGMENTS, SCAN_SLAB), F32)
        end_r, end_i = lax.fori_loop(0, seg, scan_step, (zero, zero), unroll=4)

        cr = carry_s[0:1, re_cols]
        ci = carry_s[0:1, im_cols]
        for r in range(SCAN_SEGMENTS):
            cseg_s[r:r + 1, re_cols] = cr
            cseg_s[r:r + 1, im_cols] = ci
            nr = al_re * cr - al_im * ci + end_r[r:r + 1]
            ni = al_re * ci + al_im * cr + end_i[r:r + 1]
            cr, ci = nr, ni
        carry_s[0:1, re_cols] = cr
        carry_s[0:1, im_cols] = ci

        cr8 = cseg_s[:, re_cols]
        ci8 = cseg_s[:, im_cols]

        def fix_step(tl, _):
            rows = pl.ds(pl.multiple_of(tl * SCAN_SEGMENTS, SCAN_SEGMENTS), SCAN_SEGMENTS)
            pr = apow_re_ref[pl.ds(tl, 1), nat]
            pi = apow_im_ref[pl.ds(tl, 1), nat]
            bx_s[rows, re_cols] = bx_s[rows, re_cols] + (pr * cr8 - pi * ci8)
            bx_s[rows, im_cols] = bx_s[rows, im_cols] + (pr * ci8 + pi * cr8)
            return 0

        lax.fori_loop(0, seg, fix_step, 0, unroll=4)

    y = jnp.concatenate(
        [_dot(bx_s[:, h * 2 * STATE_HALF:(h + 1) * 2 * STATE_HALF].astype(BF16), cblk_ref[h]) for h in range(2)],
        axis=1) + d_ref[...] * up
    yg = jax.nn.gelu(y)
    z = _dot(yg.astype(BF16), wglu_ref[...]) + bglu_ref[...]
    out = yg * jax.nn.sigmoid(z)
    for k in range(U_SLABS):
        yp_s[k] = out[:, k * LANES:(k + 1) * LANES]
    for r in range(SCAN_SEGMENTS):
        for k in range(U_SLABS):
            y_ref[r * seg:(r + 1) * seg, k * LANES:(k + 1) * LANES] = (
                yp_s[k, pl.ds(r, seg, stride=SCAN_SEGMENTS), :].astype(y_ref.dtype))


def _s5(u4, bblk, cblk, acoef, apow_re, apow_im, d_row, wglu, bglu, *, batch, seq, tc):
    nchunk = seq // tc
    const2 = lambda b, c: (0, 0)
    const3 = lambda b, c: (0, 0, 0)
    return pl.pallas_call(
        functools.partial(_s5_kernel, tc=tc),
        grid=(batch, nchunk),
        in_specs=[
            pl.BlockSpec((U_SLABS, tc, LANES), lambda b, c: (0, b * nchunk + c, 0)),
            pl.BlockSpec(bblk.shape, const3),
            pl.BlockSpec(cblk.shape, const3),
            pl.BlockSpec(acoef.shape, const2),
            pl.BlockSpec(apow_re.shape, const2),
            pl.BlockSpec(apow_im.shape, const2),
            pl.BlockSpec(d_row.shape, const2),
            pl.BlockSpec(wglu.shape, const2),
            pl.BlockSpec(bglu.shape, const2),
        ],
        out_specs=pl.BlockSpec((tc, SSM_WIDTH), lambda b, c: (b * nchunk + c, 0)),
        out_shape=jax.ShapeDtypeStruct((batch * seq, SSM_WIDTH), BF16),
        scratch_shapes=[
            pltpu.VMEM((tc, SSM_WIDTH), F32),
            pltpu.VMEM((tc, 2 * SSM_STATES), F32),
            pltpu.VMEM((8, 2 * SSM_STATES), F32),
            pltpu.VMEM((SCAN_SEGMENTS, 2 * SSM_STATES), F32),
            pltpu.VMEM((U_SLABS, tc, LANES), F32),
        ],
        compiler_params=pltpu.CompilerParams(
            dimension_semantics=("arbitrary", "arbitrary"), vmem_limit_bytes=VMEM_LIMIT_BYTES),
        name="s5",
    )(u4, bblk, cblk, acoef, apow_re, apow_im, d_row, wglu, bglu)


KEY_TILE = 128
CMP_TILE = 128
SEL_CHAINS = 4
WORD_BITS = 16


def _nsa_kernel(qt_ref, cp_ref, kp_ref, vt_ref, glt_ref, slope_ref, qg_ref, kg_ref, pe_ref,
                w1_ref, w2k_ref, w2vt_ref, out_ref,
                ks_s, kw_s, kc_s, vct_s, vst_s, vwt_s, sval_s, bias_s, cmat_s, acc_s, words_s, tiles_s, count_s, amat_s, cvis_s, imp_s, *, seq, tq):
    assert tq == KEY_TILE
    i = pl.program_id(2)
    n_cmp = seq // CMP_STRIDE
    n_sel = seq // SEL_BLOCK
    k_top = min(SEL_TOP_K, n_sel)
    hq = HEADS_PER_GROUP * tq

    @pl.when(i == 0)
    def _prepare_keys_and_values():
        kg = kg_ref[...]
        ks_s[...] = _rms_rows(kp_ref[:, 0:HEAD_DIM], kg[1:2]).astype(BF16)
        kw_s[...] = _rms_rows(kp_ref[:, HEAD_DIM:2 * HEAD_DIM], kg[2:3]).astype(BF16)
        vst_s[...] = vt_ref[0:HEAD_DIM, :].astype(BF16)
        vwt_s[...] = vt_ref[HEAD_DIM:2 * HEAD_DIM, :].astype(BF16)

        lo = jnp.zeros((n_cmp, 2 * CMP_HIDDEN), F32)
        hi = jnp.zeros((n_cmp, 2 * CMP_HIDDEN), F32)
        for tt in range(CMP_STRIDE):
            tok = cp_ref[pl.ds(tt, n_cmp, stride=CMP_STRIDE), :]
            lo = lo + _dot((tok + pe_ref[tt:tt + 1, :]).astype(BF16), w1_ref[tt])
            hi = hi + _dot((tok + pe_ref[CMP_STRIDE + tt:CMP_STRIDE + tt + 1, :]).astype(BF16),
                           w1_ref[CMP_STRIDE + tt])
        hidden = jax.nn.gelu(lo + pltpu.roll(hi, n_cmp - 1, 0)).astype(BF16)
        kc_s[...] = _rms_rows(_dot(hidden[:, :CMP_HIDDEN], w2k_ref[...]), kg[0:1]).astype(BF16)
        vct_s[...] = _dot_nt(w2vt_ref[...], hidden[:, CMP_HIDDEN:]).astype(BF16)

        r = lax.broadcasted_iota(jnp.int32, (KEY_TILE, hq), 0)
        a = lax.broadcasted_iota(jnp.int32, (KEY_TILE, hq), 1) & (tq - 1)
        alibi = slope_ref[0, 0:1, :] * r.astype(F32)
        cmat_s[0] = alibi
        cmat_s[1] = jnp.where(r <= a, alibi, NEG_INF)
        cmat_s[2] = jnp.where(r > a, alibi, NEG_INF)

        jrow = lax.broadcasted_iota(jnp.int32, (n_cmp, hq), 0)
        amat_s[...] = slope_ref[0, 0:1, :] * (jrow * CMP_STRIDE + (CMP_BLOCK - 1)).astype(F32)
        cvis_s[...] = (r * CMP_STRIDE + (CMP_BLOCK - 1)) - a
        imp_s[...] = jnp.zeros_like(imp_s)

    qs = i * tq
    lane = lax.broadcasted_iota(jnp.int32, (1, hq), 1)
    t_row = qs + (lane & (tq - 1))
    slope_row = slope_ref[0, 0:1, :]

    qg = qg_ref[...]
    heads = []
    for h in range(HEADS_PER_GROUP):
        qh = qt_ref[h * HEAD_DIM:(h + 1) * HEAD_DIM, :]
        ms = jnp.mean(qh * qh, axis=0, keepdims=True)
        heads.append((qh * lax.rsqrt(ms + EPS) * qg * (HEAD_DIM ** -0.5)).astype(BF16))
    qst = jnp.concatenate(heads, axis=1)

    def cmp_branch(n_tiles):
        def run():
            scores = []
            for jt in range(n_tiles):
                rows = pl.ds(jt * CMP_TILE, CMP_TILE)
                s = _dot(kc_s[rows, :], qst) + amat_s[rows, :]
                visible = cvis_s[...] <= qs - CMP_STRIDE * CMP_TILE * jt
                scores.append(jnp.where(visible, s, NEG_INF))
            m = functools.reduce(jnp.maximum, [jnp.max(s, axis=0, keepdims=True) for s in scores])
            shift = jnp.where(m > 0.5 * NEG_INF, m, -NEG_INF)
            exps = [jnp.exp(s - shift) for s in scores]
            l = functools.reduce(lambda x, y: x + y, [jnp.sum(e, axis=0, keepdims=True) for e in exps])
            r = jnp.where(l > 0.0, 1.0 / l, 0.0)
            acc = jnp.zeros((HEAD_DIM, hq), F32)
            for jt, e in enumerate(exps):
                p = e * r
                imp = p[:, 0:tq]
                for h in range(1, HEADS_PER_GROUP):
                    imp = imp + p[:, h * tq:(h + 1) * tq]
                imp_s[pl.ds(jt * CMP_TILE, CMP_TILE), :] = imp
                acc = acc + _dot(vct_s[:, pl.ds(jt * CMP_TILE, CMP_TILE)], p.astype(BF16))
            return acc
        return run

    n_cmp_live = (qs + tq - CMP_STRIDE) // CMP_STRIDE
    cmp_tiles = n_cmp // CMP_TILE
    if cmp_tiles >= 2:
        o_cmp = lax.cond(n_cmp_live <= (cmp_tiles // 2) * CMP_TILE, cmp_branch(cmp_tiles // 2), cmp_branch(cmp_tiles))
    else:
        o_cmp = cmp_branch(cmp_tiles)()

    per_block = SEL_BLOCK // CMP_STRIDE
    chunk = [imp_s[pl.ds(c, n_sel, stride=per_block), :] for c in range(per_block)]
    imp_sel = chunk[0]
    for c in range(1, per_block):
        imp_sel = imp_sel + chunk[c]
    blk = lax.broadcasted_iota(jnp.int32, (n_sel, tq), 0)
    imp_sel = imp_sel + jnp.where(blk == 0, 0.0, pltpu.roll(chunk[per_block - 1], 1, 0))
    cur = jnp.right_shift(t_row[:, 0:tq], SEL_BLOCK.bit_length() - 1)
    forced = (blk == 0) | (blk == cur) | (blk == cur - 1)
    sval = jnp.where(blk <= cur, imp_sel + jnp.where(forced, FORCE_BONUS, 0.0), NEG_INF)
    sval_s[...] = sval

    def rank_step(m, rank):
        row = sval_s[pl.ds(m, 1), :]
        ahead = jnp.where(row > sval, 1.0, jnp.where(row == sval, jnp.where(blk > m, 1.0, 0.0), 0.0))
        return rank + ahead

    n_live = jnp.minimum(n_sel, (qs + tq) // SEL_BLOCK)
    rank = lax.fori_loop(0, n_live, rank_step, jnp.zeros((n_sel, tq), F32))
    chosen = rank < k_top
    bias = jnp.where(chosen, 0.0, NEG_INF)
    bias_s[...] = jnp.concatenate([bias] * HEADS_PER_GROUP, axis=1)

    blocks_per_tile = KEY_TILE // SEL_BLOCK
    hit = jnp.where(chosen & (blk < blocks_per_tile * (i + 1)), 1.0, 0.0)
    blk_hit = jnp.max(hit, axis=1, keepdims=True)
    weight = jnp.left_shift(1, lax.broadcasted_iota(jnp.int32, (n_sel, 1), 0) & (WORD_BITS - 1)).astype(F32)
    packed = blk_hit * weight
    for w in range(n_sel // WORD_BITS):
        word = jnp.sum(packed[w * WORD_BITS:(w + 1) * WORD_BITS], axis=0, keepdims=True).astype(jnp.int32)
        words_s[w] = word[0, 0]

    tiles_per_word = WORD_BITS // blocks_per_tile
    count_s[0] = 0
    for w in range(n_sel // WORD_BITS):
        @pl.when(words_s[w] != 0)
        def _append_tiles_of_word():
            word = words_s[w]
            count = count_s[0]
            for k in range(tiles_per_word):
                bits = jnp.right_shift(word, blocks_per_tile * k) & ((1 << blocks_per_tile) - 1)
                tiles_s[count] = w * tiles_per_word + k
                count = count + jnp.where(bits != 0, 1, 0)
            count_s[0] = count

    n_active = count_s[0]
    for c in range(SEL_CHAINS):
        tiles_s[n_active + c] = -1

    def tiles_update(kts, k_s, vt_s, cidxs, biases_per_tile, ms, ls, accs):
        k0s = [pl.multiple_of(kt * KEY_TILE, KEY_TILE) for kt in kts]
        scores = [_dot(k_s[pl.ds(k0, KEY_TILE), :], qst) + cmat_s[cidx] for k0, cidx in zip(k0s, cidxs)]
        new_ms, new_ls, alphas, probs = [], [], [], []
        for s, k0, biases, m, l in zip(scores, k0s, biases_per_tile, ms, ls):
            rv = slope_row * (t_row - k0).astype(F32)
            rows = KEY_TILE // len(biases)
            parts = [s[b * rows:(b + 1) * rows] for b in range(len(biases))]
            m_new = m
            for part, bias_row in zip(parts, biases):
                m_new = jnp.maximum(m_new, jnp.max(part, axis=0, keepdims=True) + (bias_row - rv))
            seen = m_new > 0.5 * NEG_INF
            ps = []
            for part, bias_row in zip(parts, biases):
                shift = jnp.where(seen, m_new + (rv - bias_row), -NEG_INF)
                ps.append(jnp.exp(part - shift))
            p = jnp.concatenate(ps, axis=0) if len(ps) > 1 else ps[0]
            alpha = jnp.exp(m - m_new)
            new_ms.append(m_new)
            new_ls.append(alpha * l + jnp.sum(p, axis=0, keepdims=True))
            alphas.append(alpha)
            probs.append(p.astype(BF16))
        new_accs = [alpha * acc + _dot(vt_s[:, pl.ds(k0, KEY_TILE)], p)
                    for alpha, acc, k0, p in zip(alphas, accs, k0s, probs)]
        return new_ms, new_ls, new_accs

    def merge(ms, ls, accs):
        m = functools.reduce(jnp.maximum, ms)
        ws = [jnp.exp(mc - m) for mc in ms]
        l = functools.reduce(lambda a, b: a + b, [w * lc for w, lc in zip(ws, ls)])
        acc = functools.reduce(lambda a, b: a + b, [w * ac for w, ac in zip(ws, accs)])
        return acc * (1.0 / l)

    m_init = jnp.full((1, hq), NEG_INF, F32)
    l_init = jnp.zeros((1, hq), F32)
    dead_row = jnp.full((1, hq), NEG_INF, F32)
    live_row = jnp.zeros((1, hq), F32)

    for c in range(SEL_CHAINS):
        acc_s[c] = jnp.zeros((HEAD_DIM, hq), F32)

    def sel_step(j, carry):
        ms, ls = carry
        entries = [tiles_s[j * SEL_CHAINS + c] for c in range(SEL_CHAINS)]
        kts = [jnp.maximum(e, 0) for e in entries]
        biases = [[jnp.where(e < 0, dead_row, bias_s[pl.ds(blocks_per_tile * kt + b, 1), :])
                   for b in range(blocks_per_tile)] for e, kt in zip(entries, kts)]
        cidxs = [jnp.where(e == i, 1, 0) for e in entries]
        new_ms, new_ls, new_accs = tiles_update(kts, ks_s, vst_s, cidxs, biases, ms, ls,
                                                [acc_s[c] for c in range(SEL_CHAINS)])
        for c in range(SEL_CHAINS):
            acc_s[c] = new_accs[c]
        return tuple(new_ms), tuple(new_ls)

    ms, ls = lax.fori_loop(0, (n_active + SEL_CHAINS - 1) // SEL_CHAINS, sel_step,
                           ((m_init,) * SEL_CHAINS, (l_init,) * SEL_CHAINS))
    o_sel = merge(ms, ls, [acc_s[c] for c in range(SEL_CHAINS)])

    far = WINDOW // KEY_TILE
    kts = [i - d for d in range(far + 1)]
    o_win = merge(*tiles_update(
        [jnp.maximum(kt, 0) for kt in kts], kw_s, vwt_s,
        [1 if d == 0 else (2 if d == far else 0) for d in range(far + 1)],
        [[jnp.where(kt < 0, dead_row, live_row)] for kt in kts],
        [m_init] * (far + 1), [l_init] * (far + 1), [jnp.zeros((HEAD_DIM, hq), F32)] * (far + 1)))

    gl = glt_ref[...]

    def gate_row(branch):
        rows = [jax.nn.sigmoid(gl[branch * HEADS_PER_GROUP + h:branch * HEADS_PER_GROUP + h + 1, :])
                for h in range(HEADS_PER_GROUP)]
        return jnp.concatenate(rows, axis=1)

    ot = gate_row(0) * o_cmp + gate_row(1) * o_sel + gate_row(2) * o_win
    out_ref[...] = jnp.concatenate(
        [ot[:, h * tq:(h + 1) * tq] for h in range(HEADS_PER_GROUP)], axis=0).astype(out_ref.dtype)


def _nsa(qt, cp, kp, vt, glt, slopes, qg, kg, pe, w1, w2k, w2vt, *, batch, seq, tq):
    nq = seq // tq
    n_cmp = seq // CMP_STRIDE
    n_sel = seq // SEL_BLOCK
    hq = HEADS_PER_GROUP * tq
    const2 = lambda b, g, i: (0, 0)
    const3 = lambda b, g, i: (0, 0, 0)
    return pl.pallas_call(
        functools.partial(_nsa_kernel, seq=seq, tq=tq),
        grid=(batch, NSA_KV_GROUPS, nq),
        in_specs=[
            pl.BlockSpec((HEADS_PER_GROUP * HEAD_DIM, tq), lambda b, g, i: (g, b * nq + i)),
            pl.BlockSpec((seq, 2 * HEAD_DIM), lambda b, g, i: (b, g)),
            pl.BlockSpec((seq, 2 * HEAD_DIM), lambda b, g, i: (b, g)),
            pl.BlockSpec((2 * HEAD_DIM, seq), lambda b, g, i: (g, b)),
            pl.BlockSpec((GATE_ROWS, tq), lambda b, g, i: (g, b * nq + i)),
            pl.BlockSpec((1, 8, hq), lambda b, g, i: (g, 0, 0)),
            pl.BlockSpec(qg.shape, const2),
            pl.BlockSpec(kg.shape, const2),
            pl.BlockSpec(pe.shape, const2),
            pl.BlockSpec(w1.shape, const3),
            pl.BlockSpec(w2k.shape, const2),
            pl.BlockSpec(w2vt.shape, const2),
        ],
        out_specs=pl.BlockSpec((HEADS_PER_GROUP * HEAD_DIM, tq), lambda b, g, i: (g, b * nq + i)),
        out_shape=jax.ShapeDtypeStruct((NSA_WIDTH, batch * seq), BF16),
        scratch_shapes=[
            pltpu.VMEM((seq, HEAD_DIM), BF16),
            pltpu.VMEM((seq, HEAD_DIM), BF16),
            pltpu.VMEM((n_cmp, HEAD_DIM), BF16),
            pltpu.VMEM((HEAD_DIM, n_cmp), BF16),
            pltpu.VMEM((HEAD_DIM, seq), BF16),
            pltpu.VMEM((HEAD_DIM, seq), BF16),
            pltpu.VMEM((n_sel, tq), F32),
            pltpu.VMEM((n_sel, hq), F32),
            pltpu.VMEM((3, KEY_TILE, hq), F32),
            pltpu.VMEM((SEL_CHAINS, HEAD_DIM, hq), F32),
            pltpu.SMEM((n_sel // WORD_BITS,), jnp.int32),
            pltpu.SMEM((seq // KEY_TILE + SEL_CHAINS,), jnp.int32),
            pltpu.SMEM((1,), jnp.int32),
            pltpu.VMEM((n_cmp, hq), F32),
            pltpu.VMEM((CMP_TILE, hq), jnp.int32),
            pltpu.VMEM((n_cmp, tq), F32),
        ],
        compiler_params=pltpu.CompilerParams(
            dimension_semantics=("arbitrary", "arbitrary", "arbitrary"), vmem_limit_bytes=VMEM_LIMIT_BYTES),
        name="nsa",
    )(qt, cp, kp, vt, glt, slopes, qg, kg, pe, w1, w2k, w2vt)


def _final_kernel(x_ref, ys_ref, ynt_ref, g1_ref, wmg_ref, wps_ref, wpn_ref, wout_ref, g2_ref, wup_ref, wdn_ref,
                  o_ref):
    x = x_ref[...]
    h = _rms_rows(x, g1_ref[...]).astype(BF16)
    mg = _dot(h, wmg_ref[...])
    ps = _dot(ys_ref[...], wps_ref[...])
    pn = _dot_tn(ynt_ref[...], wpn_ref[...])
    merged = jax.nn.sigmoid(mg[:, :D_MODEL]) * ps + jax.nn.sigmoid(mg[:, D_MODEL:]) * pn
    x1 = x + _dot(merged.astype(BF16), wout_ref[...])
    h2 = _rms_rows(x1, g2_ref[...]).astype(BF16)
    a = jnp.maximum(_dot(h2, wup_ref[...]), 0.0)
    o_ref[...] = x1 + _dot((a * a).astype(BF16), wdn_ref[...])


def _final(x2, ys, ynt, g1, wmg, wps, wpn, wout, g2, wup, wdn, *, tm):
    n = x2.shape[0]
    const = lambda i: (0, 0)
    resident = lambda w: pl.BlockSpec(w.shape, const, pipeline_mode=pl.Buffered(1))
    return pl.pallas_call(
        _final_kernel,
        grid=(n // tm,),
        in_specs=[
            pl.BlockSpec((tm, D_MODEL), lambda i: (i, 0)),
            pl.BlockSpec((tm, SSM_WIDTH), lambda i: (i, 0)),
            pl.BlockSpec((NSA_WIDTH, tm), lambda i: (0, i)),
            resident(g1), resident(wmg), resident(wps), resident(wpn), resident(wout),
            resident(g2), resident(wup), resident(wdn),
        ],
        out_specs=pl.BlockSpec((tm, D_MODEL), lambda i: (i, 0)),
        out_shape=jax.ShapeDtypeStruct((n, D_MODEL), F32),
        compiler_params=pltpu.CompilerParams(
            dimension_semantics=("arbitrary",), vmem_limit_bytes=VMEM_LIMIT_BYTES),
        name="final",
    )(x2, ys, ynt, g1, wmg, wps, wpn, wout, g2, wup, wdn)


def _pack_in_proj(w_in):
    o1 = SSM_WIDTH
    o2 = o1 + NSA_WIDTH
    o3 = o2 + KV_WIDTH
    o4 = o3 + 3 * NSA_HEADS
    w_u, w_q, w_kv, w_gl = w_in[:, :o1], w_in[:, o1:o2], w_in[:, o2:o3], w_in[:, o3:o4]
    w_mg = w_in[:, o4:]
    kv = w_kv.reshape(D_MODEL, 3, 2, NSA_KV_GROUPS, HEAD_DIM)
    groups = range(NSA_KV_GROUPS)
    cpack = [jnp.concatenate([kv[:, 0, 0, g], kv[:, 0, 1, g]], axis=1) for g in groups]
    kpack = [jnp.concatenate([kv[:, 1, 0, g], kv[:, 2, 0, g]], axis=1) for g in groups]
    wa = jnp.concatenate([w_u] + cpack + kpack, axis=1)
    vrows = [jnp.concatenate([kv[:, 1, 1, g], kv[:, 2, 1, g]], axis=1) for g in range(NSA_KV_GROUPS)]
    gl = w_gl.reshape(D_MODEL, 3, NSA_KV_GROUPS, HEADS_PER_GROUP)
    pad = jnp.zeros((D_MODEL, GATE_ROWS - 3 * HEADS_PER_GROUP), w_in.dtype)
    glrows = [jnp.concatenate([gl[:, 0, g], gl[:, 1, g], gl[:, 2, g], pad], axis=1) for g in range(NSA_KV_GROUPS)]
    wbt = jnp.concatenate([w_q] + vrows + glrows, axis=1).T
    return wa.astype(BF16), wbt.astype(BF16), w_mg.astype(BF16)


def _pack_compress(pe_k, pe_v, w1k, w1v):
    pe = jnp.concatenate([pe_k, pe_v], axis=1)
    w1k = w1k.reshape(CMP_BLOCK, HEAD_DIM, CMP_HIDDEN)
    w1v = w1v.reshape(CMP_BLOCK, HEAD_DIM, CMP_HIDDEN)
    zero = jnp.zeros_like(w1k)
    w1 = jnp.concatenate([jnp.concatenate([w1k, zero], axis=2), jnp.concatenate([zero, w1v], axis=2)], axis=1)
    return pe, w1.astype(BF16)


def _pack_s5(a_re, a_im, log_dt, b_re, b_im, c_re, c_im, seg):
    dt = jnp.exp(log_dt)[:, None]
    mag = jnp.exp(a_re * dt)
    abar_re = mag * jnp.cos(a_im * dt)
    abar_im = mag * jnp.sin(a_im * dt)
    den = a_re * a_re + a_im * a_im
    nr = abar_re - 1.0
    fr = (nr * a_re + abar_im * a_im) / den
    fi = (abar_im * a_re - nr * a_im) / den
    bbar_re = fr[..., None] * b_re - fi[..., None] * b_im
    bbar_im = fr[..., None] * b_im + fi[..., None] * b_re
    gh = SSM_GROUPS // 2
    eye = jnp.eye(gh, dtype=F32)

    def in_blocks(m):
        return (m.transpose(0, 2, 1)[:, :, None, :] * eye[:, None, :, None]).reshape(gh * SSM_GROUP, gh * SSM_STATE)

    def out_blocks(m):
        return (m.transpose(0, 2, 1)[:, :, None, :] * eye[:, None, :, None]).reshape(gh * SSM_STATE, gh * SSM_GROUP)

    halves = [slice(0, gh), slice(gh, SSM_GROUPS)]
    bblk = jnp.stack([jnp.concatenate([in_blocks(bbar_re[h]), in_blocks(bbar_im[h])], axis=1) for h in halves])
    cblk = jnp.stack([jnp.concatenate([out_blocks(c_re[h]), -out_blocks(c_im[h])], axis=0) for h in halves])

    k = jnp.arange(1, seg + 1, dtype=F32)[:, None, None]
    pmag = jnp.exp(k * (a_re * dt))
    apow_re = (pmag * jnp.cos(k * (a_im * dt))).reshape(seg, SSM_STATES)
    apow_im = (pmag * jnp.sin(k * (a_im * dt))).reshape(seg, SSM_STATES)
    acoef = jnp.zeros((8, SSM_STATES), F32)
    acoef = acoef.at[0].set(abar_re.reshape(-1)).at[1].set(abar_im.reshape(-1))
    acoef = acoef.at[2].set(apow_re[seg - 1]).at[3].set(apow_im[seg - 1])
    return bblk.astype(BF16), cblk.astype(BF16), acoef, apow_re, apow_im


def kernel(x, norm1_g, w_in, ssm_a_re, ssm_a_im, ssm_log_dt, ssm_b_re, ssm_b_im, ssm_c_re, ssm_c_im, ssm_d, ssm_w_glu, ssm_b_glu, cmp_pe_k, cmp_pe_v, cmp_wk1, cmp_wk2, cmp_wv1, cmp_wv2, q_norm_g, k_norm_g, w_proj_ssm, w_proj_nsa, w_out, norm2_g, w_up, w_down):
    batch, seq, d_model = x.shape
    assert d_model == D_MODEL and seq % 512 == 0
    n = batch * seq
    tq = 128
    depth = w_in.shape[0]
    head = jnp.arange(1, NSA_HEADS + 1, dtype=F32)
    slopes = jnp.exp2(-8.0 * head / NSA_HEADS).reshape(NSA_KV_GROUPS, 1, HEADS_PER_GROUP, 1)
    slopes = jnp.broadcast_to(slopes, (NSA_KV_GROUPS, 8, HEADS_PER_GROUP, tq)).reshape(NSA_KV_GROUPS, 8, -1)

    x2 = x.reshape(n, D_MODEL)
    for l in range(depth):
        wa, wbt, wmg = _pack_in_proj(w_in[l])
        g1 = norm1_g[l].reshape(1, D_MODEL)
        u, cp, kp, qt, vt, glt = _in_proj(x2, g1, wa, wbt, tm=512)

        tc = 256
        bblk, cblk, acoef, apow_re, apow_im = _pack_s5(ssm_a_re[l], ssm_a_im[l], ssm_log_dt[l], ssm_b_re[l],
                                                       ssm_b_im[l], ssm_c_re[l], ssm_c_im[l], tc // SCAN_SEGMENTS)
        ys = _s5(u, bblk, cblk, acoef, apow_re, apow_im, ssm_d[l].reshape(1, SSM_WIDTH),
                 ssm_w_glu[l].astype(BF16), ssm_b_glu[l].reshape(1, SSM_WIDTH), batch=batch, seq=seq, tc=tc)

        qg = jnp.broadcast_to(q_norm_g[l].reshape(HEAD_DIM, 1), (HEAD_DIM, tq))
        kg = jnp.zeros((8, HEAD_DIM), F32).at[0:3].set(k_norm_g[l])
        pe, w1 = _pack_compress(cmp_pe_k[l], cmp_pe_v[l], cmp_wk1[l], cmp_wv1[l])
        ynt = _nsa(qt, cp, kp, vt, glt, slopes, qg, kg, pe, w1,
                   cmp_wk2[l].astype(BF16), cmp_wv2[l].T.astype(BF16), batch=batch, seq=seq, tq=tq)

        x2 = _final(x2, ys, ynt, g1, wmg, w_proj_ssm[l].astype(BF16), w_proj_nsa[l].astype(BF16),
                    w_out[l].astype(BF16), norm2_g[l].reshape(1, D_MODEL), w_up[l].astype(BF16),
                    w_down[l].astype(BF16), tm=256)
    return x2.reshape(batch, seq, D_MODEL)
```

```python
import functools

import jax
import jax.numpy as jnp
from jax import lax
from jax.experimental import pallas as pl
from jax.experimental.pallas import tpu as pltpu

D_MODEL = 1024
SSM_WIDTH = D_MODEL // 2
SSM_GROUP = 16
SSM_GROUPS = SSM_WIDTH // SSM_GROUP
SSM_STATE = 64
SSM_STATES = SSM_GROUPS * SSM_STATE
NSA_HEADS = 8
NSA_KV_GROUPS = 2
HEADS_PER_GROUP = NSA_HEADS // NSA_KV_GROUPS
HEAD_DIM = 64
NSA_WIDTH = NSA_HEADS * HEAD_DIM
CMP_STRIDE = 16
CMP_BLOCK = 2 * CMP_STRIDE
CMP_HIDDEN = 2 * HEAD_DIM
SEL_BLOCK = 64
SEL_TOP_K = 16
WINDOW = 512
D_FF = 4 * D_MODEL
N_MIXERS = 2
KV_WIDTH = 3 * 2 * NSA_KV_GROUPS * HEAD_DIM
EPS = 1e-6
NEG_INF = -1e30
FORCE_BONUS = 1e4

LANES = 128
VMEM_LIMIT_BYTES = 56 * 1024 * 1024

GATE_ROWS = 16

F32 = jnp.float32
BF16 = jnp.bfloat16


def _dot(a, b):
    return jnp.dot(a, b, preferred_element_type=F32)


def _dot_nt(a, b):
    return lax.dot_general(a, b, (((1,), (1,)), ((), ())), preferred_element_type=F32)


def _dot_tn(a, b):
    return lax.dot_general(a, b, (((0,), (0,)), ((), ())), preferred_element_type=F32)


def _rms_rows(x, g):
    ms = jnp.mean(x * x, axis=-1, keepdims=True)
    return x * lax.rsqrt(ms + EPS) * g


def _inproj_kernel(x_ref, g_ref, wa_ref, wbt_ref, u_ref, cp_ref, kp_ref, qt_ref, vt_ref, glt_ref):
    h = _rms_rows(x_ref[...], g_ref[...]).astype(BF16)
    a = _dot(h, wa_ref[...])
    for k in range(SSM_WIDTH // LANES):
        u_ref[k] = a[:, k * LANES:(k + 1) * LANES]
    cp_ref[...] = a[:, SSM_WIDTH:SSM_WIDTH + 4 * HEAD_DIM]
    kp_ref[...] = a[:, SSM_WIDTH + 4 * HEAD_DIM:]
    bt = _dot_nt(wbt_ref[...], h)
    qt_ref[...] = bt[:NSA_WIDTH]
    vt_ref[...] = bt[NSA_WIDTH:NSA_WIDTH + 4 * HEAD_DIM]
    glt_ref[...] = bt[NSA_WIDTH + 4 * HEAD_DIM:]


def _in_proj(x2, g1, wa, wbt, *, tm):
    n = x2.shape[0]
    nb = wbt.shape[0]
    const = lambda i: (0, 0)
    return pl.pallas_call(
        _inproj_kernel,
        grid=(n // tm,),
        in_specs=[
            pl.BlockSpec((tm, D_MODEL), lambda i: (i, 0)),
            pl.BlockSpec((1, D_MODEL), const),
            pl.BlockSpec(wa.shape, const),
            pl.BlockSpec(wbt.shape, const),
        ],
        out_specs=[
            pl.BlockSpec((SSM_WIDTH // LANES, tm, LANES), lambda i: (0, i, 0)),
            pl.BlockSpec((tm, 4 * HEAD_DIM), lambda i: (i, 0)),
            pl.BlockSpec((tm, 4 * HEAD_DIM), lambda i: (i, 0)),
            pl.BlockSpec((NSA_WIDTH, tm), lambda i: (0, i)),
            pl.BlockSpec((4 * HEAD_DIM, tm), lambda i: (0, i)),
            pl.BlockSpec((NSA_KV_GROUPS * GATE_ROWS, tm), lambda i: (0, i)),
        ],
        out_shape=[
            jax.ShapeDtypeStruct((SSM_WIDTH // LANES, n, LANES), F32),
            jax.ShapeDtypeStruct((n, 4 * HEAD_DIM), F32),
            jax.ShapeDtypeStruct((n, 4 * HEAD_DIM), F32),
            jax.ShapeDtypeStruct((NSA_WIDTH, n), F32),
            jax.ShapeDtypeStruct((4 * HEAD_DIM, n), F32),
            jax.ShapeDtypeStruct((NSA_KV_GROUPS * GATE_ROWS, n), F32),
        ],
        compiler_params=pltpu.CompilerParams(
            dimension_semantics=("arbitrary",), vmem_limit_bytes=VMEM_LIMIT_BYTES),
        name="in_proj",
    )(x2, g1, wa, wbt)


SCAN_SEGMENTS = 8
SCAN_SLAB = 512
STATE_HALF = SSM_STATES // 2
U_SLABS = SSM_WIDTH // LANES


def _s5_kernel(u_ref, bblk_ref, cblk_ref, acoef_ref, apow_re_ref, apow_im_ref, d_ref, wglu_ref, bglu_ref, y_ref,
               up_s, bx_s, carry_s, cseg_s, yp_s, *, tc):
    seg = tc // SCAN_SEGMENTS

    @pl.when(pl.program_id(1) == 0)
    def _reset():
        carry_s[...] = jnp.zeros_like(carry_s)

    for tl in range(seg):
        for k in range(U_SLABS):
            up_s[tl * SCAN_SEGMENTS:(tl + 1) * SCAN_SEGMENTS, k * LANES:(k + 1) * LANES] = (
                u_ref[k, pl.ds(tl, SCAN_SEGMENTS, stride=seg), :])
    up = up_s[...]
    half_w = SSM_WIDTH // 2
    for h in range(2):
        bx_s[:, h * 2 * STATE_HALF:(h + 1) * 2 * STATE_HALF] = _dot(
            up[:, h * half_w:(h + 1) * half_w].astype(BF16), bblk_ref[h])

    for slab in range(SSM_STATES // SCAN_SLAB):
        base = (slab * SCAN_SLAB // STATE_HALF) * 2 * STATE_HALF + (slab * SCAN_SLAB) % STATE_HALF
        re_cols = pl.ds(base, SCAN_SLAB)
        im_cols = pl.ds(base + STATE_HALF, SCAN_SLAB)
        nat = pl.ds(slab * SCAN_SLAB, SCAN_SLAB)
        a_re = acoef_ref[0:1, nat]
        a_im = acoef_ref[1:2, nat]
        al_re = acoef_ref[2:3, nat]
        al_im = acoef_ref[3:4, nat]

        def scan_step(tl, carry):
            xr, xi = carry
            rows = pl.ds(pl.multiple_of(tl * SCAN_SEGMENTS, SCAN_SEGMENTS), SCAN_SEGMENTS)
            nr = a_re * xr - a_im * xi + bx_s[rows, re_cols]
            ni = a_re * xi + a_im * xr + bx_s[rows, im_cols]
            bx_s[rows, re_cols] = nr
            bx_s[rows, im_cols] = ni
            return nr, ni

        zero = jnp.zeros((SCAN_SEGMENTS, SCAN_SLAB), F32)
        end_r, end_i = lax.fori_loop(0, seg, scan_step, (zero, zero), unroll=4)

        cr = carry_s[0:1, re_cols]
        ci = carry_s[0:1, im_cols]
        for r in range(SCAN_SEGMENTS):
            cseg_s[r:r + 1, re_cols] = cr
            cseg_s[r:r + 1, im_cols] = ci
            nr = al_re * cr - al_im * ci + end_r[r:r + 1]
            ni = al_re * ci + al_im * cr + end_i[r:r + 1]
            cr, ci = nr, ni
        carry_s[0:1, re_cols] = cr
        carry_s[0:1, im_cols] = ci

        cr8 = cseg_s[:, re_cols]
        ci8 = cseg_s[:, im_cols]

        def fix_step(tl, _):
            rows = pl.ds(pl.multiple_of(tl * SCAN_SEGMENTS, SCAN_SEGMENTS), SCAN_SEGMENTS)
            pr = apow_re_ref[pl.ds(tl, 1), nat]
            pi = apow_im_ref[pl.ds(tl, 1), nat]
            bx_s[rows, re_cols] = bx_s[rows, re_cols] + (pr * cr8 - pi * ci8)
            bx_s[rows, im_cols] = bx_s[rows, im_cols] + (pr * ci8 + pi * cr8)
            return 0

        lax.fori_loop(0, seg, fix_step, 0, unroll=4)

    y = jnp.concatenate(
        [_dot(bx_s[:, h * 2 * STATE_HALF:(h + 1) * 2 * STATE_HALF].astype(BF16), cblk_ref[h]) for h in range(2)],
        axis=1) + d_ref[...] * up
    yg = jax.nn.gelu(y)
    z = _dot(yg.astype(BF16), wglu_ref[...]) + bglu_ref[...]
    out = yg * jax.nn.sigmoid(z)
    for k in range(U_SLABS):
        yp_s[k] = out[:, k * LANES:(k + 1) * LANES]
    for r in range(SCAN_SEGMENTS):
        for k in range(U_SLABS):
            y_ref[r * seg:(r + 1) * seg, k * LANES:(k + 1) * LANES] = (
                yp_s[k, pl.ds(r, seg, stride=SCAN_SEGMENTS), :].astype(y_ref.dtype))


def _s5(u4, bblk, cblk, acoef, apow_re, apow_im, d_row, wglu, bglu, *, batch, seq, tc):
    nchunk = seq // tc
    const2 = lambda b, c: (0, 0)
    const3 = lambda b, c: (0, 0, 0)
    return pl.pallas_call(
        functools.partial(_s5_kernel, tc=tc),
        grid=(batch, nchunk),
        in_specs=[
            pl.BlockSpec((U_SLABS, tc, LANES), lambda b, c: (0, b * nchunk + c, 0)),
            pl.BlockSpec(bblk.shape, const3),
            pl.BlockSpec(cblk.shape, const3),
            pl.BlockSpec(acoef.shape, const2),
            pl.BlockSpec(apow_re.shape, const2),
            pl.BlockSpec(apow_im.shape, const2),
            pl.BlockSpec(d_row.shape, const2),
            pl.BlockSpec(wglu.shape, const2),
            pl.BlockSpec(bglu.shape, const2),
        ],
        out_specs=pl.BlockSpec((tc, SSM_WIDTH), lambda b, c: (b * nchunk + c, 0)),
        out_shape=jax.ShapeDtypeStruct((batch * seq, SSM_WIDTH), BF16),
        scratch_shapes=[
            pltpu.VMEM((tc, SSM_WIDTH), F32),
            pltpu.VMEM((tc, 2 * SSM_STATES), F32),
            pltpu.VMEM((8, 2 * SSM_STATES), F32),
            pltpu.VMEM((SCAN_SEGMENTS, 2 * SSM_STATES), F32),
            pltpu.VMEM((U_SLABS, tc, LANES), F32),
        ],
        compiler_params=pltpu.CompilerParams(
            dimension_semantics=("arbitrary", "arbitrary"), vmem_limit_bytes=VMEM_LIMIT_BYTES),
        name="s5",
    )(u4, bblk, cblk, acoef, apow_re, apow_im, d_row, wglu, bglu)


KEY_TILE = 256
CMP_TILE = 256
SEL_CHAINS = 4
WORD_BITS = 16


def _nsa_kernel(qt_ref, cp_ref, kp_ref, vt_ref, glt_ref, slope_ref, qg_ref, kg_ref, pe_ref,
                w1_ref, w2k_ref, w2vt_ref, out_ref,
                ks_s, kw_s, kc_s, vct_s, vst_s, vwt_s, sval_s, bias_s, cmat_s, acc_s, words_s, tiles_s, count_s, amat_s, cvis_s, imp_s, *, seq, tq):
    assert tq == KEY_TILE
    i = pl.program_id(2)
    n_cmp = seq // CMP_STRIDE
    n_sel = seq // SEL_BLOCK
    k_top = min(SEL_TOP_K, n_sel)
    hq = HEADS_PER_GROUP * tq

    @pl.when(i == 0)
    def _prepare_keys_and_values():
        kg = kg_ref[...]
        ks_s[...] = _rms_rows(kp_ref[:, 0:HEAD_DIM], kg[1:2]).astype(BF16)
        kw_s[...] = _rms_rows(kp_ref[:, HEAD_DIM:2 * HEAD_DIM], kg[2:3]).astype(BF16)
        vst_s[...] = vt_ref[0:HEAD_DIM, :].astype(BF16)
        vwt_s[...] = vt_ref[HEAD_DIM:2 * HEAD_DIM, :].astype(BF16)

        lo = jnp.zeros((n_cmp, 2 * CMP_HIDDEN), F32)
        hi = jnp.zeros((n_cmp, 2 * CMP_HIDDEN), F32)
        for tt in range(CMP_STRIDE):
            tok = cp_ref[pl.ds(tt, n_cmp, stride=CMP_STRIDE), :]
            lo = lo + _dot((tok + pe_ref[tt:tt + 1, :]).astype(BF16), w1_ref[tt])
            hi = hi + _dot((tok + pe_ref[CMP_STRIDE + tt:CMP_STRIDE + tt + 1, :]).astype(BF16),
                           w1_ref[CMP_STRIDE + tt])
        hidden = jax.nn.gelu(lo + pltpu.roll(hi, n_cmp - 1, 0)).astype(BF16)
        kc_s[...] = _rms_rows(_dot(hidden[:, :CMP_HIDDEN], w2k_ref[...]), kg[0:1]).astype(BF16)
        vct_s[...] = _dot_nt(w2vt_ref[...], hidden[:, CMP_HIDDEN:]).astype(BF16)

        r = lax.broadcasted_iota(jnp.int32, (KEY_TILE, hq), 0)
        a = lax.broadcasted_iota(jnp.int32, (KEY_TILE, hq), 1) & (tq - 1)
        alibi = slope_ref[0, 0:1, :] * r.astype(F32)
        cmat_s[0] = alibi
        cmat_s[1] = jnp.where(r <= a, alibi, NEG_INF)
        cmat_s[2] = jnp.where(r > a, alibi, NEG_INF)

        jrow = lax.broadcasted_iota(jnp.int32, (n_cmp, hq), 0)
        amat_s[...] = slope_ref[0, 0:1, :] * (jrow * CMP_STRIDE + (CMP_BLOCK - 1)).astype(F32)
        cvis_s[...] = (r * CMP_STRIDE + (CMP_BLOCK - 1)) - a
        imp_s[...] = jnp.zeros_like(imp_s)

    qs = i * tq
    lane = lax.broadcasted_iota(jnp.int32, (1, hq), 1)
    t_row = qs + (lane & (tq - 1))
    slope_row = slope_ref[0, 0:1, :]

    qg = qg_ref[...]
    heads = []
    for h in range(HEADS_PER_GROUP):
        qh = qt_ref[h * HEAD_DIM:(h + 1) * HEAD_DIM, :]
        ms = jnp.mean(qh * qh, axis=0, keepdims=True)
        heads.append((qh * lax.rsqrt(ms + EPS) * qg * (HEAD_DIM ** -0.5)).astype(BF16))
    qst = jnp.concatenate(heads, axis=1)

    def cmp_branch(n_tiles):
        def run():
            scores = []
            for jt in range(n_tiles):
                rows = pl.ds(jt * CMP_TILE, CMP_TILE)
                s = _dot(kc_s[rows, :], qst) + amat_s[rows, :]
                visible = cvis_s[...] <= qs - CMP_STRIDE * CMP_TILE * jt
                scores.append(jnp.where(visible, s, NEG_INF))
            m = functools.reduce(jnp.maximum, [jnp.max(s, axis=0, keepdims=True) for s in scores])
            shift = jnp.where(m > 0.5 * NEG_INF, m, -NEG_INF)
            exps = [jnp.exp(s - shift) for s in scores]
            l = functools.reduce(lambda x, y: x + y, [jnp.sum(e, axis=0, keepdims=True) for e in exps])
            r = jnp.where(l > 0.0, 1.0 / l, 0.0)
            acc = jnp.zeros((HEAD_DIM, hq), F32)
            for jt, e in enumerate(exps):
                p = e * r
                imp = p[:, 0:tq]
                for h in range(1, HEADS_PER_GROUP):
                    imp = imp + p[:, h * tq:(h + 1) * tq]
                for k in range(tq // LANES):
                    imp_s[k, pl.ds(jt * CMP_TILE, CMP_TILE), :] = imp[:, k * LANES:(k + 1) * LANES]
                acc = acc + _dot(vct_s[:, pl.ds(jt * CMP_TILE, CMP_TILE)], p.astype(BF16))
            return acc
        return run

    n_cmp_live = (qs + tq - CMP_STRIDE) // CMP_STRIDE
    cmp_tiles = n_cmp // CMP_TILE
    if cmp_tiles >= 2:
        o_cmp = lax.cond(n_cmp_live <= (cmp_tiles // 2) * CMP_TILE, cmp_branch(cmp_tiles // 2), cmp_branch(cmp_tiles))
    else:
        o_cmp = cmp_branch(cmp_tiles)()

    per_block = SEL_BLOCK // CMP_STRIDE
    chunk = [jnp.concatenate([imp_s[k, pl.ds(c, n_sel, stride=per_block), :] for k in range(tq // LANES)], axis=1)
             for c in range(per_block)]
    imp_sel = chunk[0]
    for c in range(1, per_block):
        imp_sel = imp_sel + chunk[c]
    blk = lax.broadcasted_iota(jnp.int32, (n_sel, tq), 0)
    imp_sel = imp_sel + jnp.where(blk == 0, 0.0, pltpu.roll(chunk[per_block - 1], 1, 0))
    cur = jnp.right_shift(t_row[:, 0:tq], SEL_BLOCK.bit_length() - 1)
    forced = (blk == 0) | (blk == cur) | (blk == cur - 1)
    sval = jnp.where(blk <= cur, imp_sel + jnp.where(forced, FORCE_BONUS, 0.0), NEG_INF)
    sval_s[...] = sval

    def rank_step(m, rank):
        row = sval_s[pl.ds(m, 1), :]
        ahead = jnp.where(row > sval, 1.0, jnp.where(row == sval, jnp.where(blk > m, 1.0, 0.0), 0.0))
        return rank + ahead

    n_live = jnp.minimum(n_sel, (qs + tq) // SEL_BLOCK)
    rank = lax.fori_loop(0, n_live, rank_step, jnp.zeros((n_sel, tq), F32))
    chosen = rank < k_top
    bias = jnp.where(chosen, 0.0, NEG_INF)
    bias_s[...] = jnp.concatenate([bias] * HEADS_PER_GROUP, axis=1)

    blocks_per_tile = KEY_TILE // SEL_BLOCK
    hit = jnp.where(chosen & (blk < blocks_per_tile * (i + 1)), 1.0, 0.0)
    blk_hit = jnp.max(hit, axis=1, keepdims=True)
    weight = jnp.left_shift(1, lax.broadcasted_iota(jnp.int32, (n_sel, 1), 0) & (WORD_BITS - 1)).astype(F32)
    packed = blk_hit * weight
    for w in range(n_sel // WORD_BITS):
        word = jnp.sum(packed[w * WORD_BITS:(w + 1) * WORD_BITS], axis=0, keepdims=True).astype(jnp.int32)
        words_s[w] = word[0, 0]

    tiles_per_word = WORD_BITS // blocks_per_tile
    count_s[0] = 0
    for w in range(n_sel // WORD_BITS):
        @pl.when(words_s[w] != 0)
        def _append_tiles_of_word():
            word = words_s[w]
            count = count_s[0]
            for k in range(tiles_per_word):
                bits = jnp.right_shift(word, blocks_per_tile * k) & ((1 << blocks_per_tile) - 1)
                tiles_s[count] = w * tiles_per_word + k
                count = count + jnp.where(bits != 0, 1, 0)
            count_s[0] = count

    n_active = count_s[0]
    for c in range(SEL_CHAINS):
        tiles_s[n_active + c] = -1

    def tiles_update(kts, k_s, vt_s, cidxs, biases_per_tile, ms, ls, accs):
        k0s = [pl.multiple_of(kt * KEY_TILE, KEY_TILE) for kt in kts]
        scores = [_dot(k_s[pl.ds(k0, KEY_TILE), :], qst) + cmat_s[cidx] for k0, cidx in zip(k0s, cidxs)]
        new_ms, new_ls, alphas, probs = [], [], [], []
        for s, k0, biases, m, l in zip(scores, k0s, biases_per_tile, ms, ls):
            rv = slope_row * (t_row - k0).astype(F32)
            rows = KEY_TILE // len(biases)
            parts = [s[b * rows:(b + 1) * rows] for b in range(len(biases))]
            m_new = m
            for part, bias_row in zip(parts, biases):
                m_new = jnp.maximum(m_new, jnp.max(part, axis=0, keepdims=True) + (bias_row - rv))
            seen = m_new > 0.5 * NEG_INF
            ps = []
            for part, bias_row in zip(parts, biases):
                shift = jnp.where(seen, m_new + (rv - bias_row), -NEG_INF)
                ps.append(jnp.exp(part - shift))
            p = jnp.concatenate(ps, axis=0) if len(ps) > 1 else ps[0]
            alpha = jnp.exp(m - m_new)
            new_ms.append(m_new)
            new_ls.append(alpha * l + jnp.sum(p, axis=0, keepdims=True))
            alphas.append(alpha)
            probs.append(p.astype(BF16))
        new_accs = [alpha * acc + _dot(vt_s[:, pl.ds(k0, KEY_TILE)], p)
                    for alpha, acc, k0, p in zip(alphas, accs, k0s, probs)]
        return new_ms, new_ls, new_accs

    def merge(ms, ls, accs):
        m = functools.reduce(jnp.maximum, ms)
        ws = [jnp.exp(mc - m) for mc in ms]
        l = functools.reduce(lambda a, b: a + b, [w * lc for w, lc in zip(ws, ls)])
        acc = functools.reduce(lambda a, b: a + b, [w * ac for w, ac in zip(ws, accs)])
        return acc * (1.0 / l)

    m_init = jnp.full((1, hq), NEG_INF, F32)
    l_init = jnp.zeros((1, hq), F32)
    dead_row = jnp.full((1, hq), NEG_INF, F32)
    live_row = jnp.zeros((1, hq), F32)

    for c in range(SEL_CHAINS):
        acc_s[c] = jnp.zeros((HEAD_DIM, hq), F32)

    def sel_step(j, carry):
        ms, ls = carry
        entries = [tiles_s[j * SEL_CHAINS + c] for c in range(SEL_CHAINS)]
        kts = [jnp.maximum(e, 0) for e in entries]
        biases = [[jnp.where(e < 0, dead_row, bias_s[pl.ds(blocks_per_tile * kt + b, 1), :])
                   for b in range(blocks_per_tile)] for e, kt in zip(entries, kts)]
        cidxs = [jnp.where(e == i, 1, 0) for e in entries]
        new_ms, new_ls, new_accs = tiles_update(kts, ks_s, vst_s, cidxs, biases, ms, ls,
                                                [acc_s[c] for c in range(SEL_CHAINS)])
        for c in range(SEL_CHAINS):
            acc_s[c] = new_accs[c]
        return tuple(new_ms), tuple(new_ls)

    ms, ls = lax.fori_loop(0, (n_active + SEL_CHAINS - 1) // SEL_CHAINS, sel_step,
                           ((m_init,) * SEL_CHAINS, (l_init,) * SEL_CHAINS))
    o_sel = merge(ms, ls, [acc_s[c] for c in range(SEL_CHAINS)])

    far = WINDOW // KEY_TILE
    kts = [i - d for d in range(far + 1)]
    o_win = merge(*tiles_update(
        [jnp.maximum(kt, 0) for kt in kts], kw_s, vwt_s,
        [1 if d == 0 else (2 if d == far else 0) for d in range(far + 1)],
        [[jnp.where(kt < 0, dead_row, live_row)] for kt in kts],
        [m_init] * (far + 1), [l_init] * (far + 1), [jnp.zeros((HEAD_DIM, hq), F32)] * (far + 1)))

    gl = glt_ref[...]

    def gate_row(branch):
        rows = [jax.nn.sigmoid(gl[branch * HEADS_PER_GROUP + h:branch * HEADS_PER_GROUP + h + 1, :])
                for h in range(HEADS_PER_GROUP)]
        return jnp.concatenate(rows, axis=1)

    ot = gate_row(0) * o_cmp + gate_row(1) * o_sel + gate_row(2) * o_win
    out_ref[...] = jnp.concatenate(
        [ot[:, h * tq:(h + 1) * tq] for h in range(HEADS_PER_GROUP)], axis=0).astype(out_ref.dtype)


def _nsa(qt, cp, kp, vt, glt, slopes, qg, kg, pe, w1, w2k, w2vt, *, batch, seq, tq):
    nq = seq // tq
    n_cmp = seq // CMP_STRIDE
    n_sel = seq // SEL_BLOCK
    hq = HEADS_PER_GROUP * tq
    const2 = lambda b, g, i: (0, 0)
    const3 = lambda b, g, i: (0, 0, 0)
    return pl.pallas_call(
        functools.partial(_nsa_kernel, seq=seq, tq=tq),
        grid=(batch, NSA_KV_GROUPS, nq),
        in_specs=[
            pl.BlockSpec((HEADS_PER_GROUP * HEAD_DIM, tq), lambda b, g, i: (g, b * nq + i)),
            pl.BlockSpec((seq, 2 * HEAD_DIM), lambda b, g, i: (b, g)),
            pl.BlockSpec((seq, 2 * HEAD_DIM), lambda b, g, i: (b, g)),
            pl.BlockSpec((2 * HEAD_DIM, seq), lambda b, g, i: (g, b)),
            pl.BlockSpec((GATE_ROWS, tq), lambda b, g, i: (g, b * nq + i)),
            pl.BlockSpec((1, 8, hq), lambda b, g, i: (g, 0, 0)),
            pl.BlockSpec(qg.shape, const2),
            pl.BlockSpec(kg.shape, const2),
            pl.BlockSpec(pe.shape, const2),
            pl.BlockSpec(w1.shape, const3),
            pl.BlockSpec(w2k.shape, const2),
            pl.BlockSpec(w2vt.shape, const2),
        ],
        out_specs=pl.BlockSpec((HEADS_PER_GROUP * HEAD_DIM, tq), lambda b, g, i: (g, b * nq + i)),
        out_shape=jax.ShapeDtypeStruct((NSA_WIDTH, batch * seq), BF16),
        scratch_shapes=[
            pltpu.VMEM((seq, HEAD_DIM), BF16),
            pltpu.VMEM((seq, HEAD_DIM), BF16),
            pltpu.VMEM((n_cmp, HEAD_DIM), BF16),
            pltpu.VMEM((HEAD_DIM, n_cmp), BF16),
            pltpu.VMEM((HEAD_DIM, seq), BF16),
            pltpu.VMEM((HEAD_DIM, seq), BF16),
            pltpu.VMEM((n_sel, tq), F32),
            pltpu.VMEM((n_sel, hq), F32),
            pltpu.VMEM((3, KEY_TILE, hq), F32),
            pltpu.VMEM((SEL_CHAINS, HEAD_DIM, hq), F32),
            pltpu.SMEM((n_sel // WORD_BITS,), jnp.int32),
            pltpu.SMEM((seq // KEY_TILE + SEL_CHAINS,), jnp.int32),
            pltpu.SMEM((1,), jnp.int32),
            pltpu.VMEM((n_cmp, hq), F32),
            pltpu.VMEM((CMP_TILE, hq), jnp.int32),
            pltpu.VMEM((tq // LANES, n_cmp, LANES), F32),
        ],
        compiler_params=pltpu.CompilerParams(
            dimension_semantics=("arbitrary", "arbitrary", "arbitrary"), vmem_limit_bytes=VMEM_LIMIT_BYTES),
        name="nsa",
    )(qt, cp, kp, vt, glt, slopes, qg, kg, pe, w1, w2k, w2vt)


def _final_kernel(x_ref, ys_ref, ynt_ref, g1_ref, wmg_ref, wps_ref, wpn_ref, wout_ref, g2_ref, wup_ref, wdn_ref,
                  o_ref):
    x = x_ref[...]
    h = _rms_rows(x, g1_ref[...]).astype(BF16)
    mg = _dot(h, wmg_ref[...])
    ps = _dot(ys_ref[...], wps_ref[...])
    pn = _dot_tn(ynt_ref[...], wpn_ref[...])
    merged = jax.nn.sigmoid(mg[:, :D_MODEL]) * ps + jax.nn.sigmoid(mg[:, D_MODEL:]) * pn
    x1 = x + _dot(merged.astype(BF16), wout_ref[...])
    h2 = _rms_rows(x1, g2_ref[...]).astype(BF16)
    a = jnp.maximum(_dot(h2, wup_ref[...]), 0.0)
    o_ref[...] = x1 + _dot((a * a).astype(BF16), wdn_ref[...])


def _final(x2, ys, ynt, g1, wmg, wps, wpn, wout, g2, wup, wdn, *, tm):
    n = x2.shape[0]
    const = lambda i: (0, 0)
    resident = lambda w: pl.BlockSpec(w.shape, const, pipeline_mode=pl.Buffered(1))
    return pl.pallas_call(
        _final_kernel,
        grid=(n // tm,),
        in_specs=[
            pl.BlockSpec((tm, D_MODEL), lambda i: (i, 0)),
            pl.BlockSpec((tm, SSM_WIDTH), lambda i: (i, 0)),
            pl.BlockSpec((NSA_WIDTH, tm), lambda i: (0, i)),
            resident(g1), resident(wmg), resident(wps), resident(wpn), resident(wout),
            resident(g2), resident(wup), resident(wdn),
        ],
        out_specs=pl.BlockSpec((tm, D_MODEL), lambda i: (i, 0)),
        out_shape=jax.ShapeDtypeStruct((n, D_MODEL), F32),
        compiler_params=pltpu.CompilerParams(
            dimension_semantics=("arbitrary",), vmem_limit_bytes=VMEM_LIMIT_BYTES),
        name="final",
    )(x2, ys, ynt, g1, wmg, wps, wpn, wout, g2, wup, wdn)


def _pack_in_proj(w_in):
    o1 = SSM_WIDTH
    o2 = o1 + NSA_WIDTH
    o3 = o2 + KV_WIDTH
    o4 = o3 + 3 * NSA_HEADS
    w_u, w_q, w_kv, w_gl = w_in[:, :o1], w_in[:, o1:o2], w_in[:, o2:o3], w_in[:, o3:o4]
    w_mg = w_in[:, o4:]
    kv = w_kv.reshape(D_MODEL, 3, 2, NSA_KV_GROUPS, HEAD_DIM)
    groups = range(NSA_KV_GROUPS)
    cpack = [jnp.concatenate([kv[:, 0, 0, g], kv[:, 0, 1, g]], axis=1) for g in groups]
    kpack = [jnp.concatenate([kv[:, 1, 0, g], kv[:, 2, 0, g]], axis=1) for g in groups]
    wa = jnp.concatenate([w_u] + cpack + kpack, axis=1)
    vrows = [jnp.concatenate([kv[:, 1, 1, g], kv[:, 2, 1, g]], axis=1) for g in range(NSA_KV_GROUPS)]
    gl = w_gl.reshape(D_MODEL, 3, NSA_KV_GROUPS, HEADS_PER_GROUP)
    pad = jnp.zeros((D_MODEL, GATE_ROWS - 3 * HEADS_PER_GROUP), w_in.dtype)
    glrows = [jnp.concatenate([gl[:, 0, g], gl[:, 1, g], gl[:, 2, g], pad], axis=1) for g in range(NSA_KV_GROUPS)]
    wbt = jnp.concatenate([w_q] + vrows + glrows, axis=1).T
    return wa.astype(BF16), wbt.astype(BF16), w_mg.astype(BF16)


def _pack_compress(pe_k, pe_v, w1k, w1v):
    pe = jnp.concatenate([pe_k, pe_v], axis=1)
    w1k = w1k.reshape(CMP_BLOCK, HEAD_DIM, CMP_HIDDEN)
    w1v = w1v.reshape(CMP_BLOCK, HEAD_DIM, CMP_HIDDEN)
    zero = jnp.zeros_like(w1k)
    w1 = jnp.concatenate([jnp.concatenate([w1k, zero], axis=2), jnp.concatenate([zero, w1v], axis=2)], axis=1)
    return pe, w1.astype(BF16)


def _pack_s5(a_re, a_im, log_dt, b_re, b_im, c_re, c_im, seg):
    dt = jnp.exp(log_dt)[:, None]
    mag = jnp.exp(a_re * dt)
    abar_re = mag * jnp.cos(a_im * dt)
    abar_im = mag * jnp.sin(a_im * dt)
    den = a_re * a_re + a_im * a_im
    nr = abar_re - 1.0
    fr = (nr * a_re + abar_im * a_im) / den
    fi = (abar_im * a_re - nr * a_im) / den
    bbar_re = fr[..., None] * b_re - fi[..., None] * b_im
    bbar_im = fr[..., None] * b_im + fi[..., None] * b_re
    gh = SSM_GROUPS // 2
    eye = jnp.eye(gh, dtype=F32)

    def in_blocks(m):
        return (m.transpose(0, 2, 1)[:, :, None, :] * eye[:, None, :, None]).reshape(gh * SSM_GROUP, gh * SSM_STATE)

    def out_blocks(m):
        return (m.transpose(0, 2, 1)[:, :, None, :] * eye[:, None, :, None]).reshape(gh * SSM_STATE, gh * SSM_GROUP)

    halves = [slice(0, gh), slice(gh, SSM_GROUPS)]
    bblk = jnp.stack([jnp.concatenate([in_blocks(bbar_re[h]), in_blocks(bbar_im[h])], axis=1) for h in halves])
    cblk = jnp.stack([jnp.concatenate([out_blocks(c_re[h]), -out_blocks(c_im[h])], axis=0) for h in halves])

    k = jnp.arange(1, seg + 1, dtype=F32)[:, None, None]
    pmag = jnp.exp(k * (a_re * dt))
    apow_re = (pmag * jnp.cos(k * (a_im * dt))).reshape(seg, SSM_STATES)
    apow_im = (pmag * jnp.sin(k * (a_im * dt))).reshape(seg, SSM_STATES)
    acoef = jnp.zeros((8, SSM_STATES), F32)
    acoef = acoef.at[0].set(abar_re.reshape(-1)).at[1].set(abar_im.reshape(-1))
    acoef = acoef.at[2].set(apow_re[seg - 1]).at[3].set(apow_im[seg - 1])
    return bblk.astype(BF16), cblk.astype(BF16), acoef, apow_re, apow_im


def kernel(x, norm1_g, w_in, ssm_a_re, ssm_a_im, ssm_log_dt, ssm_b_re, ssm_b_im, ssm_c_re, ssm_c_im, ssm_d, ssm_w_glu, ssm_b_glu, cmp_pe_k, cmp_pe_v, cmp_wk1, cmp_wk2, cmp_wv1, cmp_wv2, q_norm_g, k_norm_g, w_proj_ssm, w_proj_nsa, w_out, norm2_g, w_up, w_down):
    batch, seq, d_model = x.shape
    assert d_model == D_MODEL and seq % 512 == 0
    n = batch * seq
    tq = KEY_TILE
    depth = w_in.shape[0]
    head = jnp.arange(1, NSA_HEADS + 1, dtype=F32)
    slopes = jnp.exp2(-8.0 * head / NSA_HEADS).reshape(NSA_KV_GROUPS, 1, HEADS_PER_GROUP, 1)
    slopes = jnp.broadcast_to(slopes, (NSA_KV_GROUPS, 8, HEADS_PER_GROUP, tq)).reshape(NSA_KV_GROUPS, 8, -1)

    x2 = x.reshape(n, D_MODEL)
    for l in range(depth):
        wa, wbt, wmg = _pack_in_proj(w_in[l])
        g1 = norm1_g[l].reshape(1, D_MODEL)
        u, cp, kp, qt, vt, glt = _in_proj(x2, g1, wa, wbt, tm=512)

        tc = 256
        bblk, cblk, acoef, apow_re, apow_im = _pack_s5(ssm_a_re[l], ssm_a_im[l], ssm_log_dt[l], ssm_b_re[l],
                                                       ssm_b_im[l], ssm_c_re[l], ssm_c_im[l], tc // SCAN_SEGMENTS)
        ys = _s5(u, bblk, cblk, acoef, apow_re, apow_im, ssm_d[l].reshape(1, SSM_WIDTH),
                 ssm_w_glu[l].astype(BF16), ssm_b_glu[l].reshape(1, SSM_WIDTH), batch=batch, seq=seq, tc=tc)

        qg = jnp.broadcast_to(q_norm_g[l].reshape(HEAD_DIM, 1), (HEAD_DIM, tq))
        kg = jnp.zeros((8, HEAD_DIM), F32).at[0:3].set(k_norm_g[l])
        pe, w1 = _pack_compress(cmp_pe_k[l], cmp_pe_v[l], cmp_wk1[l], cmp_wv1[l])
        ynt = _nsa(qt, cp, kp, vt, glt, slopes, qg, kg, pe, w1,
                   cmp_wk2[l].astype(BF16), cmp_wv2[l].T.astype(BF16), batch=batch, seq=seq, tq=tq)

        x2 = _final(x2, ys, ynt, g1, wmg, w_proj_ssm[l].astype(BF16), w_proj_nsa[l].astype(BF16),
                    w_out[l].astype(BF16), norm2_g[l].reshape(1, D_MODEL), w_up[l].astype(BF16),
                    w_down[l].astype(BF16), tm=256)
    return x2.reshape(batch, seq, D_MODEL)
```

```python
import functools

import jax
import jax.numpy as jnp
from jax import lax
from jax.experimental import pallas as pl
from jax.experimental.pallas import tpu as pltpu

D_MODEL = 1024
SSM_WIDTH = D_MODEL // 2
SSM_GROUP = 16
SSM_GROUPS = SSM_WIDTH // SSM_GROUP
SSM_STATE = 64
SSM_STATES = SSM_GROUPS * SSM_STATE
NSA_HEADS = 8
NSA_KV_GROUPS = 2
HEADS_PER_GROUP = NSA_HEADS // NSA_KV_GROUPS
HEAD_DIM = 64
NSA_WIDTH = NSA_HEADS * HEAD_DIM
CMP_STRIDE = 16
CMP_BLOCK = 2 * CMP_STRIDE
CMP_HIDDEN = 2 * HEAD_DIM
SEL_BLOCK = 64
SEL_TOP_K = 16
WINDOW = 512
D_FF = 4 * D_MODEL
N_MIXERS = 2
KV_WIDTH = 3 * 2 * NSA_KV_GROUPS * HEAD_DIM
EPS = 1e-6
NEG_INF = -1e30
REMOVED = -3e38

LANES = 128
VMEM_LIMIT_BYTES = 56 * 1024 * 1024

GATE_ROWS = 16

F32 = jnp.float32
BF16 = jnp.bfloat16


def _dot(a, b):
    return jnp.dot(a, b, preferred_element_type=F32)


def _dot_nt(a, b):
    return lax.dot_general(a, b, (((1,), (1,)), ((), ())), preferred_element_type=F32)


def _dot_tn(a, b):
    return lax.dot_general(a, b, (((0,), (0,)), ((), ())), preferred_element_type=F32)


def _rms_rows(x, g):
    ms = jnp.mean(x * x, axis=-1, keepdims=True)
    return x * lax.rsqrt(ms + EPS) * g


def _inproj_kernel(x_ref, g_ref, wa_ref, wbt_ref, u_ref, cp_ref, kp_ref, qt_ref, vt_ref, glt_ref):
    h = _rms_rows(x_ref[...], g_ref[...]).astype(BF16)
    a = _dot(h, wa_ref[...])
    for k in range(SSM_WIDTH // LANES):
        u_ref[k] = a[:, k * LANES:(k + 1) * LANES]
    cp_ref[...] = a[:, SSM_WIDTH:SSM_WIDTH + 4 * HEAD_DIM]
    kp_ref[...] = a[:, SSM_WIDTH + 4 * HEAD_DIM:]
    bt = _dot_nt(wbt_ref[...], h)
    qt_ref[...] = bt[:NSA_WIDTH]
    vt_ref[...] = bt[NSA_WIDTH:NSA_WIDTH + 4 * HEAD_DIM]
    glt_ref[...] = bt[NSA_WIDTH + 4 * HEAD_DIM:]


def _in_proj(x2, g1, wa, wbt, *, tm):
    n = x2.shape[0]
    nb = wbt.shape[0]
    const = lambda i: (0, 0)
    return pl.pallas_call(
        _inproj_kernel,
        grid=(n // tm,),
        in_specs=[
            pl.BlockSpec((tm, D_MODEL), lambda i: (i, 0)),
            pl.BlockSpec((1, D_MODEL), const),
            pl.BlockSpec(wa.shape, const),
            pl.BlockSpec(wbt.shape, const),
        ],
        out_specs=[
            pl.BlockSpec((SSM_WIDTH // LANES, tm, LANES), lambda i: (0, i, 0)),
            pl.BlockSpec((tm, 4 * HEAD_DIM), lambda i: (i, 0)),
            pl.BlockSpec((tm, 4 * HEAD_DIM), lambda i: (i, 0)),
            pl.BlockSpec((NSA_WIDTH, tm), lambda i: (0, i)),
            pl.BlockSpec((4 * HEAD_DIM, tm), lambda i: (0, i)),
            pl.BlockSpec((NSA_KV_GROUPS * GATE_ROWS, tm), lambda i: (0, i)),
        ],
        out_shape=[
            jax.ShapeDtypeStruct((SSM_WIDTH // LANES, n, LANES), F32),
            jax.ShapeDtypeStruct((n, 4 * HEAD_DIM), F32),
            jax.ShapeDtypeStruct((n, 4 * HEAD_DIM), F32),
            jax.ShapeDtypeStruct((NSA_WIDTH, n), F32),
            jax.ShapeDtypeStruct((4 * HEAD_DIM, n), F32),
            jax.ShapeDtypeStruct((NSA_KV_GROUPS * GATE_ROWS, n), F32),
        ],
        compiler_params=pltpu.CompilerParams(
            dimension_semantics=("arbitrary",), vmem_limit_bytes=VMEM_LIMIT_BYTES),
        name="in_proj",
    )(x2, g1, wa, wbt)


SCAN_SEGMENTS = 8
SCAN_SLAB = 512
STATE_HALF = SSM_STATES // 2
U_SLABS = SSM_WIDTH // LANES


def _s5_kernel(u_ref, bblk_ref, cblk_ref, acoef_ref, apow_re_ref, apow_im_ref, d_ref, wglu_ref, bglu_ref, y_ref,
               up_s, bx_s, carry_s, cseg_s, yp_s, *, tc):
    seg = tc // SCAN_SEGMENTS

    @pl.when(pl.program_id(1) == 0)
    def _reset():
        carry_s[...] = jnp.zeros_like(carry_s)

    for tl in range(seg):
        for k in range(U_SLABS):
            up_s[tl * SCAN_SEGMENTS:(tl + 1) * SCAN_SEGMENTS, k * LANES:(k + 1) * LANES] = (
                u_ref[k, pl.ds(tl, SCAN_SEGMENTS, stride=seg), :])
    up = up_s[...]
    half_w = SSM_WIDTH // 2
    for h in range(2):
        bx_s[:, h * 2 * STATE_HALF:(h + 1) * 2 * STATE_HALF] = _dot(
            up[:, h * half_w:(h + 1) * half_w].astype(BF16), bblk_ref[h])

    for slab in range(SSM_STATES // SCAN_SLAB):
        base = (slab * SCAN_SLAB // STATE_HALF) * 2 * STATE_HALF + (slab * SCAN_SLAB) % STATE_HALF
        re_cols = pl.ds(base, SCAN_SLAB)
        im_cols = pl.ds(base + STATE_HALF, SCAN_SLAB)
        nat = pl.ds(slab * SCAN_SLAB, SCAN_SLAB)
        a_re = acoef_ref[0:1, nat]
        a_im = acoef_ref[1:2, nat]
        al_re = acoef_ref[2:3, nat]
        al_im = acoef_ref[3:4, nat]

        def scan_step(tl, carry):
            xr, xi = carry
            rows = pl.ds(pl.multiple_of(tl * SCAN_SEGMENTS, SCAN_SEGMENTS), SCAN_SEGMENTS)
            nr = a_re * xr - a_im * xi + bx_s[rows, re_cols]
            ni = a_re * xi + a_im * xr + bx_s[rows, im_cols]
            bx_s[rows, re_cols] = nr
            bx_s[rows, im_cols] = ni
            return nr, ni

        zero = jnp.zeros((SCAN_SEGMENTS, SCAN_SLAB), F32)
        end_r, end_i = lax.fori_loop(0, seg, scan_step, (zero, zero), unroll=4)

        cr = carry_s[0:1, re_cols]
        ci = carry_s[0:1, im_cols]
        for r in range(SCAN_SEGMENTS):
            cseg_s[r:r + 1, re_cols] = cr
            cseg_s[r:r + 1, im_cols] = ci
            nr = al_re * cr - al_im * ci + end_r[r:r + 1]
            ni = al_re * ci + al_im * cr + end_i[r:r + 1]
            cr, ci = nr, ni
        carry_s[0:1, re_cols] = cr
        carry_s[0:1, im_cols] = ci

        cr8 = cseg_s[:, re_cols]
        ci8 = cseg_s[:, im_cols]

        def fix_step(tl, _):
            rows = pl.ds(pl.multiple_of(tl * SCAN_SEGMENTS, SCAN_SEGMENTS), SCAN_SEGMENTS)
            pr = apow_re_ref[pl.ds(tl, 1), nat]
            pi = apow_im_ref[pl.ds(tl, 1), nat]
            bx_s[rows, re_cols] = bx_s[rows, re_cols] + (pr * cr8 - pi * ci8)
            bx_s[rows, im_cols] = bx_s[rows, im_cols] + (pr * ci8 + pi * cr8)
            return 0

        lax.fori_loop(0, seg, fix_step, 0, unroll=4)

    y = jnp.concatenate(
        [_dot(bx_s[:, h * 2 * STATE_HALF:(h + 1) * 2 * STATE_HALF].astype(BF16), cblk_ref[h]) for h in range(2)],
        axis=1) + d_ref[...] * up
    yg = jax.nn.gelu(y)
    z = _dot(yg.astype(BF16), wglu_ref[...]) + bglu_ref[...]
    out = yg * jax.nn.sigmoid(z)
    for k in range(U_SLABS):
        yp_s[k] = out[:, k * LANES:(k + 1) * LANES]
    for r in range(SCAN_SEGMENTS):
        for k in range(U_SLABS):
            y_ref[r * seg:(r + 1) * seg, k * LANES:(k + 1) * LANES] = (
                yp_s[k, pl.ds(r, seg, stride=SCAN_SEGMENTS), :].astype(y_ref.dtype))


def _s5(u4, bblk, cblk, acoef, apow_re, apow_im, d_row, wglu, bglu, *, batch, seq, tc):
    nchunk = seq // tc
    const2 = lambda b, c: (0, 0)
    const3 = lambda b, c: (0, 0, 0)
    return pl.pallas_call(
        functools.partial(_s5_kernel, tc=tc),
        grid=(batch, nchunk),
        in_specs=[
            pl.BlockSpec((U_SLABS, tc, LANES), lambda b, c: (0, b * nchunk + c, 0)),
            pl.BlockSpec(bblk.shape, const3),
            pl.BlockSpec(cblk.shape, const3),
            pl.BlockSpec(acoef.shape, const2),
            pl.BlockSpec(apow_re.shape, const2),
            pl.BlockSpec(apow_im.shape, const2),
            pl.BlockSpec(d_row.shape, const2),
            pl.BlockSpec(wglu.shape, const2),
            pl.BlockSpec(bglu.shape, const2),
        ],
        out_specs=pl.BlockSpec((tc, SSM_WIDTH), lambda b, c: (b * nchunk + c, 0)),
        out_shape=jax.ShapeDtypeStruct((batch * seq, SSM_WIDTH), BF16),
        scratch_shapes=[
            pltpu.VMEM((tc, SSM_WIDTH), F32),
            pltpu.VMEM((tc, 2 * SSM_STATES), F32),
            pltpu.VMEM((8, 2 * SSM_STATES), F32),
            pltpu.VMEM((SCAN_SEGMENTS, 2 * SSM_STATES), F32),
            pltpu.VMEM((U_SLABS, tc, LANES), F32),
        ],
        compiler_params=pltpu.CompilerParams(
            dimension_semantics=("arbitrary", "arbitrary"), vmem_limit_bytes=VMEM_LIMIT_BYTES),
        name="s5",
    )(u4, bblk, cblk, acoef, apow_re, apow_im, d_row, wglu, bglu)


KEY_TILE = 256
CMP_TILE = 256
SEL_CHAINS = 4
WORD_BITS = 16


def _nsa_kernel(qt_ref, cp_ref, kp_ref, vt_ref, glt_ref, slope_ref, qg_ref, kg_ref, pe_ref,
                w1_ref, w2k_ref, w2vt_ref, out_ref,
                ks_s, kw_s, kc_s, vct_s, vst_s, vwt_s, bias_s, cmat_s, acc_s, words_s, tiles_s, count_s, amat_s, cvis_s, imp_s, *, seq, tq):
    assert tq == KEY_TILE
    i = pl.program_id(2)
    n_cmp = seq // CMP_STRIDE
    n_sel = seq // SEL_BLOCK
    k_top = min(SEL_TOP_K, n_sel)
    hq = HEADS_PER_GROUP * tq

    @pl.when(i == 0)
    def _prepare_keys_and_values():
        kg = kg_ref[...]
        ks_s[...] = _rms_rows(kp_ref[:, 0:HEAD_DIM], kg[1:2]).astype(BF16)
        kw_s[...] = _rms_rows(kp_ref[:, HEAD_DIM:2 * HEAD_DIM], kg[2:3]).astype(BF16)
        vst_s[...] = vt_ref[0:HEAD_DIM, :].astype(BF16)
        vwt_s[...] = vt_ref[HEAD_DIM:2 * HEAD_DIM, :].astype(BF16)

        lo = jnp.zeros((n_cmp, 2 * CMP_HIDDEN), F32)
        hi = jnp.zeros((n_cmp, 2 * CMP_HIDDEN), F32)
        for tt in range(CMP_STRIDE):
            tok = cp_ref[pl.ds(tt, n_cmp, stride=CMP_STRIDE), :]
            lo = lo + _dot((tok + pe_ref[tt:tt + 1, :]).astype(BF16), w1_ref[tt])
            hi = hi + _dot((tok + pe_ref[CMP_STRIDE + tt:CMP_STRIDE + tt + 1, :]).astype(BF16),
                           w1_ref[CMP_STRIDE + tt])
        hidden = jax.nn.gelu(lo + pltpu.roll(hi, n_cmp - 1, 0)).astype(BF16)
        kc_s[...] = _rms_rows(_dot(hidden[:, :CMP_HIDDEN], w2k_ref[...]), kg[0:1]).astype(BF16)
        vct_s[...] = _dot_nt(w2vt_ref[...], hidden[:, CMP_HIDDEN:]).astype(BF16)

        r = lax.broadcasted_iota(jnp.int32, (KEY_TILE, hq), 0)
        a = lax.broadcasted_iota(jnp.int32, (KEY_TILE, hq), 1) & (tq - 1)
        alibi = slope_ref[0, 0:1, :] * r.astype(F32)
        cmat_s[0] = alibi
        cmat_s[1] = jnp.where(r <= a, alibi, NEG_INF)
        cmat_s[2] = jnp.where(r > a, alibi, NEG_INF)

        jrow = lax.broadcasted_iota(jnp.int32, (n_cmp, hq), 0)
        amat_s[...] = slope_ref[0, 0:1, :] * (jrow * CMP_STRIDE + (CMP_BLOCK - 1)).astype(F32)
        cvis_s[...] = (r * CMP_STRIDE + (CMP_BLOCK - 1)) - a
        imp_s[...] = jnp.zeros_like(imp_s)

    qs = i * tq
    lane = lax.broadcasted_iota(jnp.int32, (1, hq), 1)
    t_row = qs + (lane & (tq - 1))
    slope_row = slope_ref[0, 0:1, :]

    qg = qg_ref[...]
    heads = []
    for h in range(HEADS_PER_GROUP):
        qh = qt_ref[h * HEAD_DIM:(h + 1) * HEAD_DIM, :]
        ms = jnp.mean(qh * qh, axis=0, keepdims=True)
        heads.append((qh * lax.rsqrt(ms + EPS) * qg * (HEAD_DIM ** -0.5)).astype(BF16))
    qst = jnp.concatenate(heads, axis=1)

    def cmp_branch(n_tiles):
        def run():
            scores = []
            for jt in range(n_tiles):
                rows = pl.ds(jt * CMP_TILE, CMP_TILE)
                s = _dot(kc_s[rows, :], qst) + amat_s[rows, :]
                visible = cvis_s[...] <= qs - CMP_STRIDE * CMP_TILE * jt
                scores.append(jnp.where(visible, s, NEG_INF))
            m = functools.reduce(jnp.maximum, [jnp.max(s, axis=0, keepdims=True) for s in scores])
            shift = jnp.where(m > 0.5 * NEG_INF, m, -NEG_INF)
            exps = [jnp.exp(s - shift) for s in scores]
            l = functools.reduce(lambda x, y: x + y, [jnp.sum(e, axis=0, keepdims=True) for e in exps])
            r = jnp.where(l > 0.0, 1.0 / l, 0.0)
            acc = jnp.zeros((HEAD_DIM, hq), F32)
            for jt, e in enumerate(exps):
                p = e * r
                imp = p[:, 0:tq]
                for h in range(1, HEADS_PER_GROUP):
                    imp = imp + p[:, h * tq:(h + 1) * tq]
                for k in range(tq // LANES):
                    imp_s[k, pl.ds(jt * CMP_TILE, CMP_TILE), :] = imp[:, k * LANES:(k + 1) * LANES]
                acc = acc + _dot(vct_s[:, pl.ds(jt * CMP_TILE, CMP_TILE)], p.astype(BF16))
            return acc
        return run

    n_cmp_live = (qs + tq - CMP_STRIDE) // CMP_STRIDE
    cmp_tiles = n_cmp // CMP_TILE
    if cmp_tiles >= 2:
        o_cmp = lax.cond(n_cmp_live <= (cmp_tiles // 2) * CMP_TILE, cmp_branch(cmp_tiles // 2), cmp_branch(cmp_tiles))
    else:
        o_cmp = cmp_branch(cmp_tiles)()

    per_block = SEL_BLOCK // CMP_STRIDE
    chunk = [jnp.concatenate([imp_s[k, pl.ds(c, n_sel, stride=per_block), :] for k in range(tq // LANES)], axis=1)
             for c in range(per_block)]
    imp_sel = chunk[0]
    for c in range(1, per_block):
        imp_sel = imp_sel + chunk[c]
    blk = lax.broadcasted_iota(jnp.int32, (n_sel, tq), 0)
    imp_sel = imp_sel + jnp.where(blk == 0, 0.0, pltpu.roll(chunk[per_block - 1], 1, 0))
    cur = jnp.right_shift(t_row[:, 0:tq], SEL_BLOCK.bit_length() - 1)
    forced = (blk == 0) | (blk == cur) | (blk == cur - 1)
    valid = blk <= cur
    blk_f = blk.astype(F32)

    def pick_round(_, carry):
        work, taken = carry
        top = jnp.max(work, axis=0, keepdims=True)
        first = jnp.min(jnp.where(work == top, blk_f, float(n_sel)), axis=0, keepdims=True)
        pick = blk_f == first
        return jnp.where(pick, REMOVED, work), jnp.where(pick, 1.0, taken)

    _, taken = lax.fori_loop(
        0, max(k_top - 3, 0), pick_round,
        (jnp.where(valid & jnp.logical_not(forced), imp_sel, NEG_INF), jnp.where(forced & valid, 1.0, 0.0)),
        unroll=True)
    chosen = taken > 0.5
    bias = jnp.where(chosen, 0.0, NEG_INF)
    bias_s[...] = jnp.concatenate([bias] * HEADS_PER_GROUP, axis=1)

    blocks_per_tile = KEY_TILE // SEL_BLOCK
    hit = jnp.where(chosen & (blk < blocks_per_tile * (i + 1)), 1.0, 0.0)
    blk_hit = jnp.max(hit, axis=1, keepdims=True)
    weight = jnp.left_shift(1, lax.broadcasted_iota(jnp.int32, (n_sel, 1), 0) & (WORD_BITS - 1)).astype(F32)
    packed = blk_hit * weight
    for w in range(n_sel // WORD_BITS):
        word = jnp.sum(packed[w * WORD_BITS:(w + 1) * WORD_BITS], axis=0, keepdims=True).astype(jnp.int32)
        words_s[w] = word[0, 0]

    tiles_per_word = WORD_BITS // blocks_per_tile
    count_s[0] = 0
    for w in range(n_sel // WORD_BITS):
        @pl.when(words_s[w] != 0)
        def _append_tiles_of_word():
            word = words_s[w]
            count = count_s[0]
            for k in range(tiles_per_word):
                bits = jnp.right_shift(word, blocks_per_tile * k) & ((1 << blocks_per_tile) - 1)
                tiles_s[count] = w * tiles_per_word + k
                count = count + jnp.where(bits != 0, 1, 0)
            count_s[0] = count

    n_active = count_s[0]
    for c in range(SEL_CHAINS):
        tiles_s[n_active + c] = -1

    def tiles_update(kts, k_s, vt_s, cidxs, biases_per_tile, ms, ls, accs):
        k0s = [pl.multiple_of(kt * KEY_TILE, KEY_TILE) for kt in kts]
        scores = [_dot(k_s[pl.ds(k0, KEY_TILE), :], qst) + cmat_s[cidx] for k0, cidx in zip(k0s, cidxs)]
        new_ms, new_ls, alphas, probs = [], [], [], []
        for s, k0, biases, m, l in zip(scores, k0s, biases_per_tile, ms, ls):
            rv = slope_row * (t_row - k0).astype(F32)
            rows = KEY_TILE // len(biases)
            parts = [s[b * rows:(b + 1) * rows] for b in range(len(biases))]
            m_new = m
            for part, bias_row in zip(parts, biases):
                m_new = jnp.maximum(m_new, jnp.max(part, axis=0, keepdims=True) + (bias_row - rv))
            seen = m_new > 0.5 * NEG_INF
            ps = []
            for part, bias_row in zip(parts, biases):
                shift = jnp.where(seen, m_new + (rv - bias_row), -NEG_INF)
                ps.append(jnp.exp(part - shift))
            p = jnp.concatenate(ps, axis=0) if len(ps) > 1 else ps[0]
            alpha = jnp.exp(m - m_new)
            new_ms.append(m_new)
            new_ls.append(alpha * l + jnp.sum(p, axis=0, keepdims=True))
            alphas.append(alpha)
            probs.append(p.astype(BF16))
        new_accs = [alpha * acc + _dot(vt_s[:, pl.ds(k0, KEY_TILE)], p)
                    for alpha, acc, k0, p in zip(alphas, accs, k0s, probs)]
        return new_ms, new_ls, new_accs

    def merge(ms, ls, accs):
        m = functools.reduce(jnp.maximum, ms)
        ws = [jnp.exp(mc - m) for mc in ms]
        l = functools.reduce(lambda a, b: a + b, [w * lc for w, lc in zip(ws, ls)])
        acc = functools.reduce(lambda a, b: a + b, [w * ac for w, ac in zip(ws, accs)])
        return acc * (1.0 / l)

    m_init = jnp.full((1, hq), NEG_INF, F32)
    l_init = jnp.zeros((1, hq), F32)
    dead_row = jnp.full((1, hq), NEG_INF, F32)
    live_row = jnp.zeros((1, hq), F32)

    for c in range(SEL_CHAINS):
        acc_s[c] = jnp.zeros((HEAD_DIM, hq), F32)

    def sel_step(j, carry):
        ms, ls = carry
        entries = [tiles_s[j * SEL_CHAINS + c] for c in range(SEL_CHAINS)]
        kts = [jnp.maximum(e, 0) for e in entries]
        biases = [[jnp.where(e < 0, dead_row, bias_s[pl.ds(blocks_per_tile * kt + b, 1), :])
                   for b in range(blocks_per_tile)] for e, kt in zip(entries, kts)]
        cidxs = [jnp.where(e == i, 1, 0) for e in entries]
        new_ms, new_ls, new_accs = tiles_update(kts, ks_s, vst_s, cidxs, biases, ms, ls,
                                                [acc_s[c] for c in range(SEL_CHAINS)])
        for c in range(SEL_CHAINS):
            acc_s[c] = new_accs[c]
        return tuple(new_ms), tuple(new_ls)

    ms, ls = lax.fori_loop(0, (n_active + SEL_CHAINS - 1) // SEL_CHAINS, sel_step,
                           ((m_init,) * SEL_CHAINS, (l_init,) * SEL_CHAINS))
    o_sel = merge(ms, ls, [acc_s[c] for c in range(SEL_CHAINS)])

    far = WINDOW // KEY_TILE
    kts = [i - d for d in range(far + 1)]
    o_win = merge(*tiles_update(
        [jnp.maximum(kt, 0) for kt in kts], kw_s, vwt_s,
        [1 if d == 0 else (2 if d == far else 0) for d in range(far + 1)],
        [[jnp.where(kt < 0, dead_row, live_row)] for kt in kts],
        [m_init] * (far + 1), [l_init] * (far + 1), [jnp.zeros((HEAD_DIM, hq), F32)] * (far + 1)))

    gl = glt_ref[...]

    def gate_row(branch):
        rows = [jax.nn.sigmoid(gl[branch * HEADS_PER_GROUP + h:branch * HEADS_PER_GROUP + h + 1, :])
                for h in range(HEADS_PER_GROUP)]
        return jnp.concatenate(rows, axis=1)

    ot = gate_row(0) * o_cmp + gate_row(1) * o_sel + gate_row(2) * o_win
    out_ref[...] = jnp.concatenate(
        [ot[:, h * tq:(h + 1) * tq] for h in range(HEADS_PER_GROUP)], axis=0).astype(out_ref.dtype)


def _nsa(qt, cp, kp, vt, glt, slopes, qg, kg, pe, w1, w2k, w2vt, *, batch, seq, tq):
    nq = seq // tq
    n_cmp = seq // CMP_STRIDE
    n_sel = seq // SEL_BLOCK
    hq = HEADS_PER_GROUP * tq
    const2 = lambda b, g, i: (0, 0)
    const3 = lambda b, g, i: (0, 0, 0)
    return pl.pallas_call(
        functools.partial(_nsa_kernel, seq=seq, tq=tq),
        grid=(batch, NSA_KV_GROUPS, nq),
        in_specs=[
            pl.BlockSpec((HEADS_PER_GROUP * HEAD_DIM, tq), lambda b, g, i: (g, b * nq + i)),
            pl.BlockSpec((seq, 2 * HEAD_DIM), lambda b, g, i: (b, g)),
            pl.BlockSpec((seq, 2 * HEAD_DIM), lambda b, g, i: (b, g)),
            pl.BlockSpec((2 * HEAD_DIM, seq), lambda b, g, i: (g, b)),
            pl.BlockSpec((GATE_ROWS, tq), lambda b, g, i: (g, b * nq + i)),
            pl.BlockSpec((1, 8, hq), lambda b, g, i: (g, 0, 0)),
            pl.BlockSpec(qg.shape, const2),
            pl.BlockSpec(kg.shape, const2),
            pl.BlockSpec(pe.shape, const2),
            pl.BlockSpec(w1.shape, const3),
            pl.BlockSpec(w2k.shape, const2),
            pl.BlockSpec(w2vt.shape, const2),
        ],
        out_specs=pl.BlockSpec((HEADS_PER_GROUP * HEAD_DIM, tq), lambda b, g, i: (g, b * nq + i)),
        out_shape=jax.ShapeDtypeStruct((NSA_WIDTH, batch * seq), BF16),
        scratch_shapes=[
            pltpu.VMEM((seq, HEAD_DIM), BF16),
            pltpu.VMEM((seq, HEAD_DIM), BF16),
            pltpu.VMEM((n_cmp, HEAD_DIM), BF16),
            pltpu.VMEM((HEAD_DIM, n_cmp), BF16),
            pltpu.VMEM((HEAD_DIM, seq), BF16),
            pltpu.VMEM((HEAD_DIM, seq), BF16),
            pltpu.VMEM((n_sel, hq), F32),
            pltpu.VMEM((3, KEY_TILE, hq), F32),
            pltpu.VMEM((SEL_CHAINS, HEAD_DIM, hq), F32),
            pltpu.SMEM((n_sel // WORD_BITS,), jnp.int32),
            pltpu.SMEM((seq // KEY_TILE + SEL_CHAINS,), jnp.int32),
            pltpu.SMEM((1,), jnp.int32),
            pltpu.VMEM((n_cmp, hq), F32),
            pltpu.VMEM((CMP_TILE, hq), jnp.int32),
            pltpu.VMEM((tq // LANES, n_cmp, LANES), F32),
        ],
        compiler_params=pltpu.CompilerParams(
            dimension_semantics=("arbitrary", "arbitrary", "arbitrary"), vmem_limit_bytes=VMEM_LIMIT_BYTES),
        name="nsa",
    )(qt, cp, kp, vt, glt, slopes, qg, kg, pe, w1, w2k, w2vt)


def _final_kernel(x_ref, ys_ref, ynt_ref, g1_ref, wmg_ref, wps_ref, wpn_ref, wout_ref, g2_ref, wup_ref, wdn_ref,
                  o_ref):
    x = x_ref[...]
    h = _rms_rows(x, g1_ref[...]).astype(BF16)
    mg = _dot(h, wmg_ref[...])
    ps = _dot(ys_ref[...], wps_ref[...])
    pn = _dot_tn(ynt_ref[...], wpn_ref[...])
    merged = jax.nn.sigmoid(mg[:, :D_MODEL]) * ps + jax.nn.sigmoid(mg[:, D_MODEL:]) * pn
    x1 = x + _dot(merged.astype(BF16), wout_ref[...])
    h2 = _rms_rows(x1, g2_ref[...]).astype(BF16)
    a = jnp.maximum(_dot(h2, wup_ref[...]), 0.0)
    o_ref[...] = x1 + _dot((a * a).astype(BF16), wdn_ref[...])


def _final(x2, ys, ynt, g1, wmg, wps, wpn, wout, g2, wup, wdn, *, tm):
    n = x2.shape[0]
    const = lambda i: (0, 0)
    resident = lambda w: pl.BlockSpec(w.shape, const, pipeline_mode=pl.Buffered(1))
    return pl.pallas_call(
        _final_kernel,
        grid=(n // tm,),
        in_specs=[
            pl.BlockSpec((tm, D_MODEL), lambda i: (i, 0)),
            pl.BlockSpec((tm, SSM_WIDTH), lambda i: (i, 0)),
            pl.BlockSpec((NSA_WIDTH, tm), lambda i: (0, i)),
            resident(g1), resident(wmg), resident(wps), resident(wpn), resident(wout),
            resident(g2), resident(wup), resident(wdn),
        ],
        out_specs=pl.BlockSpec((tm, D_MODEL), lambda i: (i, 0)),
        out_shape=jax.ShapeDtypeStruct((n, D_MODEL), F32),
        compiler_params=pltpu.CompilerParams(
            dimension_semantics=("arbitrary",), vmem_limit_bytes=VMEM_LIMIT_BYTES),
        name="final",
    )(x2, ys, ynt, g1, wmg, wps, wpn, wout, g2, wup, wdn)


def _pack_in_proj(w_in):
    o1 = SSM_WIDTH
    o2 = o1 + NSA_WIDTH
    o3 = o2 + KV_WIDTH
    o4 = o3 + 3 * NSA_HEADS
    w_u, w_q, w_kv, w_gl = w_in[:, :o1], w_in[:, o1:o2], w_in[:, o2:o3], w_in[:, o3:o4]
    w_mg = w_in[:, o4:]
    kv = w_kv.reshape(D_MODEL, 3, 2, NSA_KV_GROUPS, HEAD_DIM)
    groups = range(NSA_KV_GROUPS)
    cpack = [jnp.concatenate([kv[:, 0, 0, g], kv[:, 0, 1, g]], axis=1) for g in groups]
    kpack = [jnp.concatenate([kv[:, 1, 0, g], kv[:, 2, 0, g]], axis=1) for g in groups]
    wa = jnp.concatenate([w_u] + cpack + kpack, axis=1)
    vrows = [jnp.concatenate([kv[:, 1, 1, g], kv[:, 2, 1, g]], axis=1) for g in range(NSA_KV_GROUPS)]
    gl = w_gl.reshape(D_MODEL, 3, NSA_KV_GROUPS, HEADS_PER_GROUP)
    pad = jnp.zeros((D_MODEL, GATE_ROWS - 3 * HEADS_PER_GROUP), w_in.dtype)
    glrows = [jnp.concatenate([gl[:, 0, g], gl[:, 1, g], gl[:, 2, g], pad], axis=1) for g in range(NSA_KV_GROUPS)]
    wbt = jnp.concatenate([w_q] + vrows + glrows, axis=1).T
    return wa.astype(BF16), wbt.astype(BF16), w_mg.astype(BF16)


def _pack_compress(pe_k, pe_v, w1k, w1v):
    pe = jnp.concatenate([pe_k, pe_v], axis=1)
    w1k = w1k.reshape(CMP_BLOCK, HEAD_DIM, CMP_HIDDEN)
    w1v = w1v.reshape(CMP_BLOCK, HEAD_DIM, CMP_HIDDEN)
    zero = jnp.zeros_like(w1k)
    w1 = jnp.concatenate([jnp.concatenate([w1k, zero], axis=2), jnp.concatenate([zero, w1v], axis=2)], axis=1)
    return pe, w1.astype(BF16)


def _pack_s5(a_re, a_im, log_dt, b_re, b_im, c_re, c_im, seg):
    dt = jnp.exp(log_dt)[:, None]
    mag = jnp.exp(a_re * dt)
    abar_re = mag * jnp.cos(a_im * dt)
    abar_im = mag * jnp.sin(a_im * dt)
    den = a_re * a_re + a_im * a_im
    nr = abar_re - 1.0
    fr = (nr * a_re + abar_im * a_im) / den
    fi = (abar_im * a_re - nr * a_im) / den
    bbar_re = fr[..., None] * b_re - fi[..., None] * b_im
    bbar_im = fr[..., None] * b_im + fi[..., None] * b_re
    gh = SSM_GROUPS // 2
    eye = jnp.eye(gh, dtype=F32)

    def in_blocks(m):
        return (m.transpose(0, 2, 1)[:, :, None, :] * eye[:, None, :, None]).reshape(gh * SSM_GROUP, gh * SSM_STATE)

    def out_blocks(m):
        return (m.transpose(0, 2, 1)[:, :, None, :] * eye[:, None, :, None]).reshape(gh * SSM_STATE, gh * SSM_GROUP)

    halves = [slice(0, gh), slice(gh, SSM_GROUPS)]
    bblk = jnp.stack([jnp.concatenate([in_blocks(bbar_re[h]), in_blocks(bbar_im[h])], axis=1) for h in halves])
    cblk = jnp.stack([jnp.concatenate([out_blocks(c_re[h]), -out_blocks(c_im[h])], axis=0) for h in halves])

    k = jnp.arange(1, seg + 1, dtype=F32)[:, None, None]
    pmag = jnp.exp(k * (a_re * dt))
    apow_re = (pmag * jnp.cos(k * (a_im * dt))).reshape(seg, SSM_STATES)
    apow_im = (pmag * jnp.sin(k * (a_im * dt))).reshape(seg, SSM_STATES)
    acoef = jnp.zeros((8, SSM_STATES), F32)
    acoef = acoef.at[0].set(abar_re.reshape(-1)).at[1].set(abar_im.reshape(-1))
    acoef = acoef.at[2].set(apow_re[seg - 1]).at[3].set(apow_im[seg - 1])
    return bblk.astype(BF16), cblk.astype(BF16), acoef, apow_re, apow_im


def kernel(x, norm1_g, w_in, ssm_a_re, ssm_a_im, ssm_log_dt, ssm_b_re, ssm_b_im, ssm_c_re, ssm_c_im, ssm_d, ssm_w_glu, ssm_b_glu, cmp_pe_k, cmp_pe_v, cmp_wk1, cmp_wk2, cmp_wv1, cmp_wv2, q_norm_g, k_norm_g, w_proj_ssm, w_proj_nsa, w_out, norm2_g, w_up, w_down):
    batch, seq, d_model = x.shape
    assert d_model == D_MODEL and seq % 512 == 0
    n = batch * seq
    tq = KEY_TILE
    depth = w_in.shape[0]
    head = jnp.arange(1, NSA_HEADS + 1, dtype=F32)
    slopes = jnp.exp2(-8.0 * head / NSA_HEADS).reshape(NSA_KV_GROUPS, 1, HEADS_PER_GROUP, 1)
    slopes = jnp.broadcast_to(slopes, (NSA_KV_GROUPS, 8, HEADS_PER_GROUP, tq)).reshape(NSA_KV_GROUPS, 8, -1)

    x2 = x.reshape(n, D_MODEL)
    for l in range(depth):
        wa, wbt, wmg = _pack_in_proj(w_in[l])
        g1 = norm1_g[l].reshape(1, D_MODEL)
        u, cp, kp, qt, vt, glt = _in_proj(x2, g1, wa, wbt, tm=512)

        tc = 256
        bblk, cblk, acoef, apow_re, apow_im = _pack_s5(ssm_a_re[l], ssm_a_im[l], ssm_log_dt[l], ssm_b_re[l],
                                                       ssm_b_im[l], ssm_c_re[l], ssm_c_im[l], tc // SCAN_SEGMENTS)
        ys = _s5(u, bblk, cblk, acoef, apow_re, apow_im, ssm_d[l].reshape(1, SSM_WIDTH),
                 ssm_w_glu[l].astype(BF16), ssm_b_glu[l].reshape(1, SSM_WIDTH), batch=batch, seq=seq, tc=tc)

        qg = jnp.broadcast_to(q_norm_g[l].reshape(HEAD_DIM, 1), (HEAD_DIM, tq))
        kg = jnp.zeros((8, HEAD_DIM), F32).at[0:3].set(k_norm_g[l])
        pe, w1 = _pack_compress(cmp_pe_k[l], cmp_pe_v[l], cmp_wk1[l], cmp_wv1[l])
        ynt = _nsa(qt, cp, kp, vt, glt, slopes, qg, kg, pe, w1,
                   cmp_wk2[l].astype(BF16), cmp_wv2[l].T.astype(BF16), batch=batch, seq=seq, tq=tq)

        x2 = _final(x2, ys, ynt, g1, wmg, w_proj_ssm[l].astype(BF16), w_proj_nsa[l].astype(BF16),
                    w_out[l].astype(BF16), norm2_g[l].reshape(1, D_MODEL), w_up[l].astype(BF16),
                    w_down[l].astype(BF16), tm=256)
    return x2.reshape(batch, seq, D_MODEL)
```

```python
import functools

import jax
import jax.numpy as jnp
from jax import lax
from jax.experimental import pallas as pl
from jax.experimental.pallas import tpu as pltpu

D_MODEL = 1024
SSM_WIDTH = D_MODEL // 2
SSM_GROUP = 16
SSM_GROUPS = SSM_WIDTH // SSM_GROUP
SSM_STATE = 64
SSM_STATES = SSM_GROUPS * SSM_STATE
NSA_HEADS = 8
NSA_KV_GROUPS = 2
HEADS_PER_GROUP = NSA_HEADS // NSA_KV_GROUPS
HEAD_DIM = 64
NSA_WIDTH = NSA_HEADS * HEAD_DIM
CMP_STRIDE = 16
CMP_BLOCK = 2 * CMP_STRIDE
CMP_HIDDEN = 2 * HEAD_DIM
SEL_BLOCK = 64
SEL_TOP_K = 16
WINDOW = 512
D_FF = 4 * D_MODEL
N_MIXERS = 2
KV_WIDTH = 3 * 2 * NSA_KV_GROUPS * HEAD_DIM
EPS = 1e-6
NEG_INF = -1e30
REMOVED = -3e38

LANES = 128
VMEM_LIMIT_BYTES = 56 * 1024 * 1024

GATE_ROWS = 16

F32 = jnp.float32
BF16 = jnp.bfloat16


def _dot(a, b):
    return jnp.dot(a, b, preferred_element_type=F32)


def _dot_nt(a, b):
    return lax.dot_general(a, b, (((1,), (1,)), ((), ())), preferred_element_type=F32)


def _dot_tn(a, b):
    return lax.dot_general(a, b, (((0,), (0,)), ((), ())), preferred_element_type=F32)


def _rms_rows(x, g):
    ms = jnp.mean(x * x, axis=-1, keepdims=True)
    return x * lax.rsqrt(ms + EPS) * g


def _inproj_kernel(x_ref, g_ref, wa_ref, wbt_ref, u_ref, cp_ref, kp_ref, qt_ref, vt_ref, glt_ref):
    h = _rms_rows(x_ref[...], g_ref[...]).astype(BF16)
    a = _dot(h, wa_ref[...])
    for k in range(SSM_WIDTH // LANES):
        u_ref[k] = a[:, k * LANES:(k + 1) * LANES]
    cp_ref[...] = a[:, SSM_WIDTH:SSM_WIDTH + 4 * HEAD_DIM]
    kp_ref[...] = a[:, SSM_WIDTH + 4 * HEAD_DIM:]
    bt = _dot_nt(wbt_ref[...], h)
    qt_ref[...] = bt[:NSA_WIDTH]
    vt_ref[...] = bt[NSA_WIDTH:NSA_WIDTH + 4 * HEAD_DIM]
    glt_ref[...] = bt[NSA_WIDTH + 4 * HEAD_DIM:]


def _in_proj(x2, g1, wa, wbt, *, tm):
    n = x2.shape[0]
    nb = wbt.shape[0]
    const = lambda i: (0, 0)
    return pl.pallas_call(
        _inproj_kernel,
        grid=(n // tm,),
        in_specs=[
            pl.BlockSpec((tm, D_MODEL), lambda i: (i, 0)),
            pl.BlockSpec((1, D_MODEL), const),
            pl.BlockSpec(wa.shape, const),
            pl.BlockSpec(wbt.shape, const),
        ],
        out_specs=[
            pl.BlockSpec((SSM_WIDTH // LANES, tm, LANES), lambda i: (0, i, 0)),
            pl.BlockSpec((tm, 4 * HEAD_DIM), lambda i: (i, 0)),
            pl.BlockSpec((tm, 4 * HEAD_DIM), lambda i: (i, 0)),
            pl.BlockSpec((NSA_WIDTH, tm), lambda i: (0, i)),
            pl.BlockSpec((4 * HEAD_DIM, tm), lambda i: (0, i)),
            pl.BlockSpec((NSA_KV_GROUPS * GATE_ROWS, tm), lambda i: (0, i)),
        ],
        out_shape=[
            jax.ShapeDtypeStruct((SSM_WIDTH // LANES, n, LANES), F32),
            jax.ShapeDtypeStruct((n, 4 * HEAD_DIM), F32),
            jax.ShapeDtypeStruct((n, 4 * HEAD_DIM), F32),
            jax.ShapeDtypeStruct((NSA_WIDTH, n), F32),
            jax.ShapeDtypeStruct((4 * HEAD_DIM, n), F32),
            jax.ShapeDtypeStruct((NSA_KV_GROUPS * GATE_ROWS, n), F32),
        ],
        compiler_params=pltpu.CompilerParams(
            dimension_semantics=("arbitrary",), vmem_limit_bytes=VMEM_LIMIT_BYTES),
        name="in_proj",
    )(x2, g1, wa, wbt)


SCAN_SEGMENTS = 8
SCAN_SLAB = 512
STATE_HALF = SSM_STATES // 2
U_SLABS = SSM_WIDTH // LANES


def _s5_kernel(u_ref, bblk_ref, cblk_ref, acoef_ref, apow_re_ref, apow_im_ref, d_ref, wglu_ref, bglu_ref, y_ref,
               up_s, bx_s, carry_s, cseg_s, yp_s, *, tc):
    seg = tc // SCAN_SEGMENTS

    @pl.when(pl.program_id(1) == 0)
    def _reset():
        carry_s[...] = jnp.zeros_like(carry_s)

    for tl in range(seg):
        for k in range(U_SLABS):
            up_s[tl * SCAN_SEGMENTS:(tl + 1) * SCAN_SEGMENTS, k * LANES:(k + 1) * LANES] = (
                u_ref[k, pl.ds(tl, SCAN_SEGMENTS, stride=seg), :])
    up = up_s[...]
    half_w = SSM_WIDTH // 2
    for h in range(2):
        bx_s[:, h * 2 * STATE_HALF:(h + 1) * 2 * STATE_HALF] = _dot(
            up[:, h * half_w:(h + 1) * half_w].astype(BF16), bblk_ref[h])

    for slab in range(SSM_STATES // SCAN_SLAB):
        base = (slab * SCAN_SLAB // STATE_HALF) * 2 * STATE_HALF + (slab * SCAN_SLAB) % STATE_HALF
        re_cols = pl.ds(base, SCAN_SLAB)
        im_cols = pl.ds(base + STATE_HALF, SCAN_SLAB)
        nat = pl.ds(slab * SCAN_SLAB, SCAN_SLAB)
        a_re = acoef_ref[0:1, nat]
        a_im = acoef_ref[1:2, nat]
        al_re = acoef_ref[2:3, nat]
        al_im = acoef_ref[3:4, nat]

        def scan_step(tl, carry):
            xr, xi = carry
            rows = pl.ds(pl.multiple_of(tl * SCAN_SEGMENTS, SCAN_SEGMENTS), SCAN_SEGMENTS)
            nr = a_re * xr - a_im * xi + bx_s[rows, re_cols]
            ni = a_re * xi + a_im * xr + bx_s[rows, im_cols]
            bx_s[rows, re_cols] = nr
            bx_s[rows, im_cols] = ni
            return nr, ni

        zero = jnp.zeros((SCAN_SEGMENTS, SCAN_SLAB), F32)
        end_r, end_i = lax.fori_loop(0, seg, scan_step, (zero, zero), unroll=4)

        cr = carry_s[0:1, re_cols]
        ci = carry_s[0:1, im_cols]
        for r in range(SCAN_SEGMENTS):
            cseg_s[r:r + 1, re_cols] = cr
            cseg_s[r:r + 1, im_cols] = ci
            nr = al_re * cr - al_im * ci + end_r[r:r + 1]
            ni = al_re * ci + al_im * cr + end_i[r:r + 1]
            cr, ci = nr, ni
        carry_s[0:1, re_cols] = cr
        carry_s[0:1, im_cols] = ci

        cr8 = cseg_s[:, re_cols]
        ci8 = cseg_s[:, im_cols]

        def fix_step(tl, _):
            rows = pl.ds(pl.multiple_of(tl * SCAN_SEGMENTS, SCAN_SEGMENTS), SCAN_SEGMENTS)
            pr = apow_re_ref[pl.ds(tl, 1), nat]
            pi = apow_im_ref[pl.ds(tl, 1), nat]
            bx_s[rows, re_cols] = bx_s[rows, re_cols] + (pr * cr8 - pi * ci8)
            bx_s[rows, im_cols] = bx_s[rows, im_cols] + (pr * ci8 + pi * cr8)
            return 0

        lax.fori_loop(0, seg, fix_step, 0, unroll=4)

    y = jnp.concatenate(
        [_dot(bx_s[:, h * 2 * STATE_HALF:(h + 1) * 2 * STATE_HALF].astype(BF16), cblk_ref[h]) for h in range(2)],
        axis=1) + d_ref[...] * up
    yg = jax.nn.gelu(y)
    z = _dot(yg.astype(BF16), wglu_ref[...]) + bglu_ref[...]
    out = yg * jax.nn.sigmoid(z)
    for k in range(U_SLABS):
        yp_s[k] = out[:, k * LANES:(k + 1) * LANES]
    for r in range(SCAN_SEGMENTS):
        for k in range(U_SLABS):
            y_ref[r * seg:(r + 1) * seg, k * LANES:(k + 1) * LANES] = (
                yp_s[k, pl.ds(r, seg, stride=SCAN_SEGMENTS), :].astype(y_ref.dtype))


def _s5(u4, bblk, cblk, acoef, apow_re, apow_im, d_row, wglu, bglu, *, batch, seq, tc):
    nchunk = seq // tc
    const2 = lambda b, c: (0, 0)
    const3 = lambda b, c: (0, 0, 0)
    return pl.pallas_call(
        functools.partial(_s5_kernel, tc=tc),
        grid=(batch, nchunk),
        in_specs=[
            pl.BlockSpec((U_SLABS, tc, LANES), lambda b, c: (0, b * nchunk + c, 0)),
            pl.BlockSpec(bblk.shape, const3),
            pl.BlockSpec(cblk.shape, const3),
            pl.BlockSpec(acoef.shape, const2),
            pl.BlockSpec(apow_re.shape, const2),
            pl.BlockSpec(apow_im.shape, const2),
            pl.BlockSpec(d_row.shape, const2),
            pl.BlockSpec(wglu.shape, const2),
            pl.BlockSpec(bglu.shape, const2),
        ],
        out_specs=pl.BlockSpec((tc, SSM_WIDTH), lambda b, c: (b * nchunk + c, 0)),
        out_shape=jax.ShapeDtypeStruct((batch * seq, SSM_WIDTH), BF16),
        scratch_shapes=[
            pltpu.VMEM((tc, SSM_WIDTH), F32),
            pltpu.VMEM((tc, 2 * SSM_STATES), F32),
            pltpu.VMEM((8, 2 * SSM_STATES), F32),
            pltpu.VMEM((SCAN_SEGMENTS, 2 * SSM_STATES), F32),
            pltpu.VMEM((U_SLABS, tc, LANES), F32),
        ],
        compiler_params=pltpu.CompilerParams(
            dimension_semantics=("arbitrary", "arbitrary"), vmem_limit_bytes=VMEM_LIMIT_BYTES),
        name="s5",
    )(u4, bblk, cblk, acoef, apow_re, apow_im, d_row, wglu, bglu)


KEY_TILE = 256
CMP_TILE = 256
VALUE_ROWS = HEAD_DIM + 16
SEL_CHAINS = 4
WORD_BITS = 16


def _nsa_kernel(qt_ref, cp_ref, kp_ref, vt_ref, glt_ref, slope_ref, qg_ref, kg_ref, pe_ref,
                w1_ref, w2k_ref, w2vt_ref, out_ref,
                ks_s, kw_s, kc_s, vct_s, vst_s, vwt_s, bias_s, cmat_s, acc_s, words_s, tiles_s, count_s, amat_s, cvis_s, imp_s, *, seq, tq):
    assert tq == KEY_TILE
    i = pl.program_id(2)
    n_cmp = seq // CMP_STRIDE
    n_sel = seq // SEL_BLOCK
    k_top = min(SEL_TOP_K, n_sel)
    hq = HEADS_PER_GROUP * tq

    @pl.when(i == 0)
    def _prepare_keys_and_values():
        kg = kg_ref[...]
        offs = (lax.broadcasted_iota(jnp.int32, (seq, HEAD_DIM), 0) & (KEY_TILE - 1)).astype(F32)
        offs = jnp.where(lax.broadcasted_iota(jnp.int32, (seq, HEAD_DIM), 1) == 0, offs, 0.0)
        ks_s[...] = jnp.concatenate([_rms_rows(kp_ref[:, 0:HEAD_DIM], kg[1:2]), offs], axis=1).astype(BF16)
        kw_s[...] = jnp.concatenate([_rms_rows(kp_ref[:, HEAD_DIM:2 * HEAD_DIM], kg[2:3]), offs], axis=1).astype(BF16)
        ones = jnp.ones((VALUE_ROWS - HEAD_DIM, seq), F32)
        vst_s[...] = jnp.concatenate([vt_ref[0:HEAD_DIM, :], ones], axis=0).astype(BF16)
        vwt_s[...] = jnp.concatenate([vt_ref[HEAD_DIM:2 * HEAD_DIM, :], ones], axis=0).astype(BF16)

        lo = jnp.zeros((n_cmp, 2 * CMP_HIDDEN), F32)
        hi = jnp.zeros((n_cmp, 2 * CMP_HIDDEN), F32)
        for tt in range(CMP_STRIDE):
            tok = cp_ref[pl.ds(tt, n_cmp, stride=CMP_STRIDE), :]
            lo = lo + _dot((tok + pe_ref[tt:tt + 1, :]).astype(BF16), w1_ref[tt])
            hi = hi + _dot((tok + pe_ref[CMP_STRIDE + tt:CMP_STRIDE + tt + 1, :]).astype(BF16),
                           w1_ref[CMP_STRIDE + tt])
        hidden = jax.nn.gelu(lo + pltpu.roll(hi, n_cmp - 1, 0)).astype(BF16)
        kc_s[...] = _rms_rows(_dot(hidden[:, :CMP_HIDDEN], w2k_ref[...]), kg[0:1]).astype(BF16)
        vct_s[...] = _dot_nt(w2vt_ref[...], hidden[:, CMP_HIDDEN:]).astype(BF16)

        r = lax.broadcasted_iota(jnp.int32, (KEY_TILE, hq), 0)
        a = lax.broadcasted_iota(jnp.int32, (KEY_TILE, hq), 1) & (tq - 1)
        cmat_s[0] = jnp.where(r <= a, 0.0, NEG_INF)
        cmat_s[1] = jnp.where(r > a, 0.0, NEG_INF)

        jrow = lax.broadcasted_iota(jnp.int32, (n_cmp, hq), 0)
        amat_s[...] = slope_ref[0, 0:1, :] * (jrow * CMP_STRIDE + (CMP_BLOCK - 1)).astype(F32)
        cvis_s[...] = (r * CMP_STRIDE + (CMP_BLOCK - 1)) - a
        imp_s[...] = jnp.zeros_like(imp_s)

    qs = i * tq
    lane = lax.broadcasted_iota(jnp.int32, (1, hq), 1)
    t_row = qs + (lane & (tq - 1))
    slope_row = slope_ref[0, 0:1, :]

    qg = qg_ref[...]
    heads = []
    for h in range(HEADS_PER_GROUP):
        qh = qt_ref[h * HEAD_DIM:(h + 1) * HEAD_DIM, :]
        ms = jnp.mean(qh * qh, axis=0, keepdims=True)
        heads.append((qh * lax.rsqrt(ms + EPS) * qg * (HEAD_DIM ** -0.5)).astype(BF16))
    pad = lax.broadcasted_iota(jnp.int32, (HEAD_DIM, hq), 0)
    qst = jnp.concatenate([jnp.concatenate(heads, axis=1),
                           jnp.where(pad == 0, slope_row, 0.0).astype(BF16)], axis=0)

    def cmp_branch(n_tiles):
        def run():
            scores = []
            for jt in range(n_tiles):
                rows = pl.ds(jt * CMP_TILE, CMP_TILE)
                s = _dot(kc_s[rows, :], qst[0:HEAD_DIM]) + amat_s[rows, :]
                visible = cvis_s[...] <= qs - CMP_STRIDE * CMP_TILE * jt
                scores.append(jnp.where(visible, s, NEG_INF))
            m = functools.reduce(jnp.maximum, [jnp.max(s, axis=0, keepdims=True) for s in scores])
            shift = jnp.where(m > 0.5 * NEG_INF, m, -NEG_INF)
            exps = [jnp.exp(s - shift) for s in scores]
            l = functools.reduce(lambda x, y: x + y, [jnp.sum(e, axis=0, keepdims=True) for e in exps])
            r = jnp.where(l > 0.0, 1.0 / l, 0.0)
            acc = jnp.zeros((HEAD_DIM, hq), F32)
            for jt, e in enumerate(exps):
                p = e * r
                imp = p[:, 0:tq]
                for h in range(1, HEADS_PER_GROUP):
                    imp = imp + p[:, h * tq:(h + 1) * tq]
                for k in range(tq // LANES):
                    imp_s[k, pl.ds(jt * CMP_TILE, CMP_TILE), :] = imp[:, k * LANES:(k + 1) * LANES]
                acc = acc + _dot(vct_s[:, pl.ds(jt * CMP_TILE, CMP_TILE)], p.astype(BF16))
            return acc
        return run

    n_cmp_live = (qs + tq - CMP_STRIDE) // CMP_STRIDE
    cmp_tiles = n_cmp // CMP_TILE
    if cmp_tiles >= 2:
        o_cmp = lax.cond(n_cmp_live <= (cmp_tiles // 2) * CMP_TILE, cmp_branch(cmp_tiles // 2), cmp_branch(cmp_tiles))
    else:
        o_cmp = cmp_branch(cmp_tiles)()

    per_block = SEL_BLOCK // CMP_STRIDE
    chunk = [jnp.concatenate([imp_s[k, pl.ds(c, n_sel, stride=per_block), :] for k in range(tq // LANES)], axis=1)
             for c in range(per_block)]
    imp_sel = chunk[0]
    for c in range(1, per_block):
        imp_sel = imp_sel + chunk[c]
    blk = lax.broadcasted_iota(jnp.int32, (n_sel, tq), 0)
    imp_sel = imp_sel + jnp.where(blk == 0, 0.0, pltpu.roll(chunk[per_block - 1], 1, 0))
    cur = jnp.right_shift(t_row[:, 0:tq], SEL_BLOCK.bit_length() - 1)
    forced = (blk == 0) | (blk == cur) | (blk == cur - 1)
    valid = blk <= cur
    blk_f = blk.astype(F32)

    def pick_round(_, carry):
        work, taken = carry
        top = jnp.max(work, axis=0, keepdims=True)
        first = jnp.min(jnp.where(work == top, blk_f, float(n_sel)), axis=0, keepdims=True)
        pick = blk_f == first
        return jnp.where(pick, REMOVED, work), jnp.where(pick, 1.0, taken)

    _, taken = lax.fori_loop(
        0, max(k_top - 3, 0), pick_round,
        (jnp.where(valid & jnp.logical_not(forced), imp_sel, NEG_INF), jnp.where(forced & valid, 1.0, 0.0)),
        unroll=True)
    chosen = taken > 0.5
    bias = jnp.where(chosen, 0.0, NEG_INF)
    bias_s[...] = jnp.concatenate([bias] * HEADS_PER_GROUP, axis=1)

    blocks_per_tile = KEY_TILE // SEL_BLOCK
    hit = jnp.where(chosen & (blk < blocks_per_tile * i), 1.0, 0.0)
    blk_hit = jnp.max(hit, axis=1, keepdims=True)
    weight = jnp.left_shift(1, lax.broadcasted_iota(jnp.int32, (n_sel, 1), 0) & (WORD_BITS - 1)).astype(F32)
    packed = blk_hit * weight
    for w in range(n_sel // WORD_BITS):
        word = jnp.sum(packed[w * WORD_BITS:(w + 1) * WORD_BITS], axis=0, keepdims=True).astype(jnp.int32)
        words_s[w] = word[0, 0]

    tiles_per_word = WORD_BITS // blocks_per_tile
    count_s[0] = 0
    for w in range(n_sel // WORD_BITS):
        @pl.when(words_s[w] != 0)
        def _append_tiles_of_word():
            word = words_s[w]
            count = count_s[0]
            for k in range(tiles_per_word):
                bits = jnp.right_shift(word, blocks_per_tile * k) & ((1 << blocks_per_tile) - 1)
                tiles_s[count] = w * tiles_per_word + k
                count = count + jnp.where(bits != 0, 1, 0)
            count_s[0] = count

    n_active = count_s[0]
    for c in range(SEL_CHAINS):
        tiles_s[n_active + c] = -1

    def tiles_update(kts, k_s, vt_s, masks, biases_per_tile, ms, accs):
        k0s = [pl.multiple_of(kt * KEY_TILE, KEY_TILE) for kt in kts]
        scores = []
        for k0, mask in zip(k0s, masks):
            s = _dot(k_s[pl.ds(k0, KEY_TILE), :], qst)
            scores.append(s if mask is None else s + cmat_s[mask])
        new_ms, alphas, probs = [], [], []
        for s, k0, biases, m in zip(scores, k0s, biases_per_tile, ms):
            rv = slope_row * (t_row - k0).astype(F32)
            rows = KEY_TILE // len(biases)
            parts = [s[b * rows:(b + 1) * rows] for b in range(len(biases))]
            m_new = m
            for part, bias_row in zip(parts, biases):
                m_new = jnp.maximum(m_new, jnp.max(part, axis=0, keepdims=True) + (bias_row - rv))
            seen = m_new > 0.5 * NEG_INF
            ps = []
            for part, bias_row in zip(parts, biases):
                shift = jnp.where(seen, m_new + (rv - bias_row), -NEG_INF)
                ps.append(jnp.exp(part - shift))
            new_ms.append(m_new)
            alphas.append(jnp.exp(m - m_new))
            probs.append((jnp.concatenate(ps, axis=0) if len(ps) > 1 else ps[0]).astype(BF16))
        new_accs = [alpha * acc + _dot(vt_s[:, pl.ds(k0, KEY_TILE)], p)
                    for alpha, acc, k0, p in zip(alphas, accs, k0s, probs)]
        return new_ms, new_accs

    def merge(ms, accs):
        m = functools.reduce(jnp.maximum, ms)
        acc = functools.reduce(lambda x, y: x + y, [jnp.exp(mc - m) * ac for mc, ac in zip(ms, accs)])
        return acc[0:HEAD_DIM] * (1.0 / acc[HEAD_DIM:HEAD_DIM + 1])

    m_init = jnp.full((1, hq), NEG_INF, F32)
    acc_init = jnp.zeros((VALUE_ROWS, hq), F32)
    dead_row = jnp.full((1, hq), NEG_INF, F32)
    live_row = jnp.zeros((1, hq), F32)

    def block_biases(kt):
        return [bias_s[pl.ds(blocks_per_tile * kt + b, 1), :] for b in range(blocks_per_tile)]

    for c in range(SEL_CHAINS):
        acc_s[c] = acc_init

    def sel_step(j, ms):
        entries = [tiles_s[j * SEL_CHAINS + c] for c in range(SEL_CHAINS)]
        kts = [jnp.maximum(e, 0) for e in entries]
        biases = [[jnp.where(e < 0, dead_row, row) for row in block_biases(kt)] for e, kt in zip(entries, kts)]
        new_ms, new_accs = tiles_update(kts, ks_s, vst_s, [None] * SEL_CHAINS, biases, ms,
                                        [acc_s[c] for c in range(SEL_CHAINS)])
        for c in range(SEL_CHAINS):
            acc_s[c] = new_accs[c]
        return tuple(new_ms)

    ms = lax.fori_loop(0, (n_active + SEL_CHAINS - 1) // SEL_CHAINS, sel_step, (m_init,) * SEL_CHAINS)

    far = WINDOW // KEY_TILE
    kts = [i - d for d in range(far + 1)]
    win_masks = [0 if d == 0 else (1 if d == far else None) for d in range(far + 1)]
    win_biases = [[jnp.where(kt < 0, dead_row, live_row)] for kt in kts]
    n_win = far + 1
    ms_d, accs_d = tiles_update([i], ks_s, vst_s, [0], [block_biases(i)], [m_init], [acc_init])
    ms_w, accs_w = tiles_update([jnp.maximum(kt, 0) for kt in kts], kw_s, vwt_s, win_masks, win_biases,
                                [m_init] * n_win, [acc_init] * n_win)
    o_sel = merge(list(ms) + ms_d, [acc_s[c] for c in range(SEL_CHAINS)] + accs_d)
    o_win = merge(ms_w, accs_w)

    gl = glt_ref[...]

    def gate_row(branch):
        rows = [jax.nn.sigmoid(gl[branch * HEADS_PER_GROUP + h:branch * HEADS_PER_GROUP + h + 1, :])
                for h in range(HEADS_PER_GROUP)]
        return jnp.concatenate(rows, axis=1)

    ot = gate_row(0) * o_cmp + gate_row(1) * o_sel + gate_row(2) * o_win
    out_ref[...] = jnp.concatenate(
        [ot[:, h * tq:(h + 1) * tq] for h in range(HEADS_PER_GROUP)], axis=0).astype(out_ref.dtype)


def _nsa(qt, cp, kp, vt, glt, slopes, qg, kg, pe, w1, w2k, w2vt, *, batch, seq, tq):
    nq = seq // tq
    n_cmp = seq // CMP_STRIDE
    n_sel = seq // SEL_BLOCK
    hq = HEADS_PER_GROUP * tq
    const2 = lambda b, g, i: (0, 0)
    const3 = lambda b, g, i: (0, 0, 0)
    return pl.pallas_call(
        functools.partial(_nsa_kernel, seq=seq, tq=tq),
        grid=(batch, NSA_KV_GROUPS, nq),
        in_specs=[
            pl.BlockSpec((HEADS_PER_GROUP * HEAD_DIM, tq), lambda b, g, i: (g, b * nq + i)),
            pl.BlockSpec((seq, 2 * HEAD_DIM), lambda b, g, i: (b, g)),
            pl.BlockSpec((seq, 2 * HEAD_DIM), lambda b, g, i: (b, g)),
            pl.BlockSpec((2 * HEAD_DIM, seq), lambda b, g, i: (g, b)),
            pl.BlockSpec((GATE_ROWS, tq), lambda b, g, i: (g, b * nq + i)),
            pl.BlockSpec((1, 8, hq), lambda b, g, i: (g, 0, 0)),
            pl.BlockSpec(qg.shape, const2),
            pl.BlockSpec(kg.shape, const2),
            pl.BlockSpec(pe.shape, const2),
            pl.BlockSpec(w1.shape, const3),
            pl.BlockSpec(w2k.shape, const2),
            pl.BlockSpec(w2vt.shape, const2),
        ],
        out_specs=pl.BlockSpec((HEADS_PER_GROUP * HEAD_DIM, tq), lambda b, g, i: (g, b * nq + i)),
        out_shape=jax.ShapeDtypeStruct((NSA_WIDTH, batch * seq), BF16),
        scratch_shapes=[
            pltpu.VMEM((seq, 2 * HEAD_DIM), BF16),
            pltpu.VMEM((seq, 2 * HEAD_DIM), BF16),
            pltpu.VMEM((n_cmp, HEAD_DIM), BF16),
            pltpu.VMEM((HEAD_DIM, n_cmp), BF16),
            pltpu.VMEM((VALUE_ROWS, seq), BF16),
            pltpu.VMEM((VALUE_ROWS, seq), BF16),
            pltpu.VMEM((n_sel, hq), F32),
            pltpu.VMEM((2, KEY_TILE, hq), F32),
            pltpu.VMEM((SEL_CHAINS, VALUE_ROWS, hq), F32),
            pltpu.SMEM((n_sel // WORD_BITS,), jnp.int32),
            pltpu.SMEM((seq // KEY_TILE + SEL_CHAINS,), jnp.int32),
            pltpu.SMEM((1,), jnp.int32),
            pltpu.VMEM((n_cmp, hq), F32),
            pltpu.VMEM((CMP_TILE, hq), jnp.int32),
            pltpu.VMEM((tq // LANES, n_cmp, LANES), F32),
        ],
        compiler_params=pltpu.CompilerParams(
            dimension_semantics=("arbitrary", "arbitrary", "arbitrary"), vmem_limit_bytes=VMEM_LIMIT_BYTES),
        name="nsa",
    )(qt, cp, kp, vt, glt, slopes, qg, kg, pe, w1, w2k, w2vt)


def _final_kernel(x_ref, ys_ref, ynt_ref, g1_ref, wmg_ref, wps_ref, wpn_ref, wout_ref, g2_ref, wup_ref, wdn_ref,
                  o_ref):
    x = x_ref[...]
    h = _rms_rows(x, g1_ref[...]).astype(BF16)
    mg = _dot(h, wmg_ref[...])
    ps = _dot(ys_ref[...], wps_ref[...])
    pn = _dot_tn(ynt_ref[...], wpn_ref[...])
    merged = jax.nn.sigmoid(mg[:, :D_MODEL]) * ps + jax.nn.sigmoid(mg[:, D_MODEL:]) * pn
    x1 = x + _dot(merged.astype(BF16), wout_ref[...])
    h2 = _rms_rows(x1, g2_ref[...]).astype(BF16)
    a = jnp.maximum(_dot(h2, wup_ref[...]), 0.0)
    o_ref[...] = x1 + _dot((a * a).astype(BF16), wdn_ref[...])


def _final(x2, ys, ynt, g1, wmg, wps, wpn, wout, g2, wup, wdn, *, tm):
    n = x2.shape[0]
    const = lambda i: (0, 0)
    resident = lambda w: pl.BlockSpec(w.shape, const, pipeline_mode=pl.Buffered(1))
    return pl.pallas_call(
        _final_kernel,
        grid=(n // tm,),
        in_specs=[
            pl.BlockSpec((tm, D_MODEL), lambda i: (i, 0)),
            pl.BlockSpec((tm, SSM_WIDTH), lambda i: (i, 0)),
            pl.BlockSpec((NSA_WIDTH, tm), lambda i: (0, i)),
            resident(g1), resident(wmg), resident(wps), resident(wpn), resident(wout),
            resident(g2), resident(wup), resident(wdn),
        ],
        out_specs=pl.BlockSpec((tm, D_MODEL), lambda i: (i, 0)),
        out_shape=jax.ShapeDtypeStruct((n, D_MODEL), F32),
        compiler_params=pltpu.CompilerParams(
            dimension_semantics=("arbitrary",), vmem_limit_bytes=VMEM_LIMIT_BYTES),
        name="final",
    )(x2, ys, ynt, g1, wmg, wps, wpn, wout, g2, wup, wdn)


def _pack_in_proj(w_in):
    o1 = SSM_WIDTH
    o2 = o1 + NSA_WIDTH
    o3 = o2 + KV_WIDTH
    o4 = o3 + 3 * NSA_HEADS
    w_u, w_q, w_kv, w_gl = w_in[:, :o1], w_in[:, o1:o2], w_in[:, o2:o3], w_in[:, o3:o4]
    w_mg = w_in[:, o4:]
    kv = w_kv.reshape(D_MODEL, 3, 2, NSA_KV_GROUPS, HEAD_DIM)
    groups = range(NSA_KV_GROUPS)
    cpack = [jnp.concatenate([kv[:, 0, 0, g], kv[:, 0, 1, g]], axis=1) for g in groups]
    kpack = [jnp.concatenate([kv[:, 1, 0, g], kv[:, 2, 0, g]], axis=1) for g in groups]
    wa = jnp.concatenate([w_u] + cpack + kpack, axis=1)
    vrows = [jnp.concatenate([kv[:, 1, 1, g], kv[:, 2, 1, g]], axis=1) for g in range(NSA_KV_GROUPS)]
    gl = w_gl.reshape(D_MODEL, 3, NSA_KV_GROUPS, HEADS_PER_GROUP)
    pad = jnp.zeros((D_MODEL, GATE_ROWS - 3 * HEADS_PER_GROUP), w_in.dtype)
    glrows = [jnp.concatenate([gl[:, 0, g], gl[:, 1, g], gl[:, 2, g], pad], axis=1) for g in range(NSA_KV_GROUPS)]
    wbt = jnp.concatenate([w_q] + vrows + glrows, axis=1).T
    return wa.astype(BF16), wbt.astype(BF16), w_mg.astype(BF16)


def _pack_compress(pe_k, pe_v, w1k, w1v):
    pe = jnp.concatenate([pe_k, pe_v], axis=1)
    w1k = w1k.reshape(CMP_BLOCK, HEAD_DIM, CMP_HIDDEN)
    w1v = w1v.reshape(CMP_BLOCK, HEAD_DIM, CMP_HIDDEN)
    zero = jnp.zeros_like(w1k)
    w1 = jnp.concatenate([jnp.concatenate([w1k, zero], axis=2), jnp.concatenate([zero, w1v], axis=2)], axis=1)
    return pe, w1.astype(BF16)


def _pack_s5(a_re, a_im, log_dt, b_re, b_im, c_re, c_im, seg):
    dt = jnp.exp(log_dt)[:, None]
    mag = jnp.exp(a_re * dt)
    abar_re = mag * jnp.cos(a_im * dt)
    abar_im = mag * jnp.sin(a_im * dt)
    den = a_re * a_re + a_im * a_im
    nr = abar_re - 1.0
    fr = (nr * a_re + abar_im * a_im) / den
    fi = (abar_im * a_re - nr * a_im) / den
    bbar_re = fr[..., None] * b_re - fi[..., None] * b_im
    bbar_im = fr[..., None] * b_im + fi[..., None] * b_re
    gh = SSM_GROUPS // 2
    eye = jnp.eye(gh, dtype=F32)

    def in_blocks(m):
        return (m.transpose(0, 2, 1)[:, :, None, :] * eye[:, None, :, None]).reshape(gh * SSM_GROUP, gh * SSM_STATE)

    def out_blocks(m):
        return (m.transpose(0, 2, 1)[:, :, None, :] * eye[:, None, :, None]).reshape(gh * SSM_STATE, gh * SSM_GROUP)

    halves = [slice(0, gh), slice(gh, SSM_GROUPS)]
    bblk = jnp.stack([jnp.concatenate([in_blocks(bbar_re[h]), in_blocks(bbar_im[h])], axis=1) for h in halves])
    cblk = jnp.stack([jnp.concatenate([out_blocks(c_re[h]), -out_blocks(c_im[h])], axis=0) for h in halves])

    k = jnp.arange(1, seg + 1, dtype=F32)[:, None, None]
    pmag = jnp.exp(k * (a_re * dt))
    apow_re = (pmag * jnp.cos(k * (a_im * dt))).reshape(seg, SSM_STATES)
    apow_im = (pmag * jnp.sin(k * (a_im * dt))).reshape(seg, SSM_STATES)
    acoef = jnp.zeros((8, SSM_STATES), F32)
    acoef = acoef.at[0].set(abar_re.reshape(-1)).at[1].set(abar_im.reshape(-1))
    acoef = acoef.at[2].set(apow_re[seg - 1]).at[3].set(apow_im[seg - 1])
    return bblk.astype(BF16), cblk.astype(BF16), acoef, apow_re, apow_im


def kernel(x, norm1_g, w_in, ssm_a_re, ssm_a_im, ssm_log_dt, ssm_b_re, ssm_b_im, ssm_c_re, ssm_c_im, ssm_d, ssm_w_glu, ssm_b_glu, cmp_pe_k, cmp_pe_v, cmp_wk1, cmp_wk2, cmp_wv1, cmp_wv2, q_norm_g, k_norm_g, w_proj_ssm, w_proj_nsa, w_out, norm2_g, w_up, w_down):
    batch, seq, d_model = x.shape
    assert d_model == D_MODEL and seq % 512 == 0
    n = batch * seq
    tq = KEY_TILE
    depth = w_in.shape[0]
    head = jnp.arange(1, NSA_HEADS + 1, dtype=F32)
    slopes = jnp.exp2(-8.0 * head / NSA_HEADS).reshape(NSA_KV_GROUPS, 1, HEADS_PER_GROUP, 1)
    slopes = jnp.broadcast_to(slopes, (NSA_KV_GROUPS, 8, HEADS_PER_GROUP, tq)).reshape(NSA_KV_GROUPS, 8, -1)

    x2 = x.reshape(n, D_MODEL)
    for l in range(depth):
        wa, wbt, wmg = _pack_in_proj(w_in[l])
        g1 = norm1_g[l].reshape(1, D_MODEL)
        u, cp, kp, qt, vt, glt = _in_proj(x2, g1, wa, wbt, tm=512)

        tc = 256
        bblk, cblk, acoef, apow_re, apow_im = _pack_s5(ssm_a_re[l], ssm_a_im[l], ssm_log_dt[l], ssm_b_re[l],
                                                       ssm_b_im[l], ssm_c_re[l], ssm_c_im[l], tc // SCAN_SEGMENTS)
        ys = _s5(u, bblk, cblk, acoef, apow_re, apow_im, ssm_d[l].reshape(1, SSM_WIDTH),
                 ssm_w_glu[l].astype(BF16), ssm_b_glu[l].reshape(1, SSM_WIDTH), batch=batch, seq=seq, tc=tc)

        qg = jnp.broadcast_to(q_norm_g[l].reshape(HEAD_DIM, 1), (HEAD_DIM, tq))
        kg = jnp.zeros((8, HEAD_DIM), F32).at[0:3].set(k_norm_g[l])
        pe, w1 = _pack_compress(cmp_pe_k[l], cmp_pe_v[l], cmp_wk1[l], cmp_wv1[l])
        ynt = _nsa(qt, cp, kp, vt, glt, slopes, qg, kg, pe, w1,
                   cmp_wk2[l].astype(BF16), cmp_wv2[l].T.astype(BF16), batch=batch, seq=seq, tq=tq)

        x2 = _final(x2, ys, ynt, g1, wmg, w_proj_ssm[l].astype(BF16), w_proj_nsa[l].astype(BF16),
                    w_out[l].astype(BF16), norm2_g[l].reshape(1, D_MODEL), w_up[l].astype(BF16),
                    w_down[l].astype(BF16), tm=256)
    return x2.reshape(batch, seq, D_MODEL)
```

```python
import functools

import jax
import jax.numpy as jnp
from jax import lax
from jax.experimental import pallas as pl
from jax.experimental.pallas import tpu as pltpu

D_MODEL = 1024
SSM_WIDTH = D_MODEL // 2
SSM_GROUP = 16
SSM_GROUPS = SSM_WIDTH // SSM_GROUP
SSM_STATE = 64
SSM_STATES = SSM_GROUPS * SSM_STATE
NSA_HEADS = 8
NSA_KV_GROUPS = 2
HEADS_PER_GROUP = NSA_HEADS // NSA_KV_GROUPS
HEAD_DIM = 64
NSA_WIDTH = NSA_HEADS * HEAD_DIM
CMP_STRIDE = 16
CMP_BLOCK = 2 * CMP_STRIDE
CMP_HIDDEN = 2 * HEAD_DIM
SEL_BLOCK = 64
SEL_TOP_K = 16
WINDOW = 512
D_FF = 4 * D_MODEL
N_MIXERS = 2
KV_WIDTH = 3 * 2 * NSA_KV_GROUPS * HEAD_DIM
EPS = 1e-6
NEG_INF = -1e30
REMOVED = -3e38

LANES = 128
VMEM_LIMIT_BYTES = 56 * 1024 * 1024

GATE_ROWS = 16

F32 = jnp.float32
BF16 = jnp.bfloat16


def _dot(a, b):
    return jnp.dot(a, b, preferred_element_type=F32)


def _dot_nt(a, b):
    return lax.dot_general(a, b, (((1,), (1,)), ((), ())), preferred_element_type=F32)


def _dot_tn(a, b):
    return lax.dot_general(a, b, (((0,), (0,)), ((), ())), preferred_element_type=F32)


def _rms_rows(x, g):
    ms = jnp.mean(x * x, axis=-1, keepdims=True)
    return x * lax.rsqrt(ms + EPS) * g


def _inproj_kernel(x_ref, g_ref, wa_ref, wbt_ref, u_ref, cp_ref, kp_ref, qt_ref, vt_ref, glt_ref):
    h = _rms_rows(x_ref[...], g_ref[...]).astype(BF16)
    a = _dot(h, wa_ref[...])
    for k in range(SSM_WIDTH // LANES):
        u_ref[k] = a[:, k * LANES:(k + 1) * LANES]
    cp_ref[...] = a[:, SSM_WIDTH:SSM_WIDTH + 4 * HEAD_DIM]
    kp_ref[...] = a[:, SSM_WIDTH + 4 * HEAD_DIM:]
    bt = _dot_nt(wbt_ref[...], h)
    qt_ref[...] = bt[:NSA_WIDTH]
    vt_ref[...] = bt[NSA_WIDTH:NSA_WIDTH + 4 * HEAD_DIM]
    glt_ref[...] = bt[NSA_WIDTH + 4 * HEAD_DIM:]


def _in_proj(x2, g1, wa, wbt, *, tm):
    n = x2.shape[0]
    nb = wbt.shape[0]
    const = lambda i: (0, 0)
    return pl.pallas_call(
        _inproj_kernel,
        grid=(n // tm,),
        in_specs=[
            pl.BlockSpec((tm, D_MODEL), lambda i: (i, 0)),
            pl.BlockSpec((1, D_MODEL), const),
            pl.BlockSpec(wa.shape, const),
            pl.BlockSpec(wbt.shape, const),
        ],
        out_specs=[
            pl.BlockSpec((SSM_WIDTH // LANES, tm, LANES), lambda i: (0, i, 0)),
            pl.BlockSpec((tm, 4 * HEAD_DIM), lambda i: (i, 0)),
            pl.BlockSpec((tm, 4 * HEAD_DIM), lambda i: (i, 0)),
            pl.BlockSpec((NSA_WIDTH, tm), lambda i: (0, i)),
            pl.BlockSpec((4 * HEAD_DIM, tm), lambda i: (0, i)),
            pl.BlockSpec((NSA_KV_GROUPS * GATE_ROWS, tm), lambda i: (0, i)),
        ],
        out_shape=[
            jax.ShapeDtypeStruct((SSM_WIDTH // LANES, n, LANES), F32),
            jax.ShapeDtypeStruct((n, 4 * HEAD_DIM), F32),
            jax.ShapeDtypeStruct((n, 4 * HEAD_DIM), F32),
            jax.ShapeDtypeStruct((NSA_WIDTH, n), F32),
            jax.ShapeDtypeStruct((4 * HEAD_DIM, n), F32),
            jax.ShapeDtypeStruct((NSA_KV_GROUPS * GATE_ROWS, n), F32),
        ],
        compiler_params=pltpu.CompilerParams(
            dimension_semantics=("arbitrary",), vmem_limit_bytes=VMEM_LIMIT_BYTES),
        name="in_proj",
    )(x2, g1, wa, wbt)


SCAN_SEGMENTS = 8
SCAN_SLAB = 512
STATE_HALF = SSM_STATES // 2
U_SLABS = SSM_WIDTH // LANES


def _s5_kernel(u_ref, bblk_ref, cblk_ref, acoef_ref, apow_re_ref, apow_im_ref, d_ref, wglu_ref, bglu_ref, y_ref,
               up_s, bx_s, carry_s, cseg_s, yp_s, *, tc):
    seg = tc // SCAN_SEGMENTS

    @pl.when(pl.program_id(1) == 0)
    def _reset():
        carry_s[...] = jnp.zeros_like(carry_s)

    for tl in range(seg):
        for k in range(U_SLABS):
            up_s[tl * SCAN_SEGMENTS:(tl + 1) * SCAN_SEGMENTS, k * LANES:(k + 1) * LANES] = (
                u_ref[k, pl.ds(tl, SCAN_SEGMENTS, stride=seg), :])
    up = up_s[...]
    half_w = SSM_WIDTH // 2
    for h in range(2):
        bx_s[:, h * 2 * STATE_HALF:(h + 1) * 2 * STATE_HALF] = _dot(
            up[:, h * half_w:(h + 1) * half_w].astype(BF16), bblk_ref[h])

    for slab in range(SSM_STATES // SCAN_SLAB):
        base = (slab * SCAN_SLAB // STATE_HALF) * 2 * STATE_HALF + (slab * SCAN_SLAB) % STATE_HALF
        re_cols = pl.ds(base, SCAN_SLAB)
        im_cols = pl.ds(base + STATE_HALF, SCAN_SLAB)
        nat = pl.ds(slab * SCAN_SLAB, SCAN_SLAB)
        a_re = acoef_ref[0:1, nat]
        a_im = acoef_ref[1:2, nat]
        al_re = acoef_ref[2:3, nat]
        al_im = acoef_ref[3:4, nat]

        def scan_step(tl, carry):
            xr, xi = carry
            rows = pl.ds(pl.multiple_of(tl * SCAN_SEGMENTS, SCAN_SEGMENTS), SCAN_SEGMENTS)
            nr = a_re * xr - a_im * xi + bx_s[rows, re_cols]
            ni = a_re * xi + a_im * xr + bx_s[rows, im_cols]
            bx_s[rows, re_cols] = nr
            bx_s[rows, im_cols] = ni
            return nr, ni

        zero = jnp.zeros((SCAN_SEGMENTS, SCAN_SLAB), F32)
        end_r, end_i = lax.fori_loop(0, seg, scan_step, (zero, zero), unroll=True)

        cr = carry_s[0:1, re_cols]
        ci = carry_s[0:1, im_cols]
        for r in range(SCAN_SEGMENTS):
            cseg_s[r:r + 1, re_cols] = cr
            cseg_s[r:r + 1, im_cols] = ci
            nr = al_re * cr - al_im * ci + end_r[r:r + 1]
            ni = al_re * ci + al_im * cr + end_i[r:r + 1]
            cr, ci = nr, ni
        carry_s[0:1, re_cols] = cr
        carry_s[0:1, im_cols] = ci

        cr8 = cseg_s[:, re_cols]
        ci8 = cseg_s[:, im_cols]

        def fix_step(tl, _):
            rows = pl.ds(pl.multiple_of(tl * SCAN_SEGMENTS, SCAN_SEGMENTS), SCAN_SEGMENTS)
            pr = apow_re_ref[pl.ds(tl, 1), nat]
            pi = apow_im_ref[pl.ds(tl, 1), nat]
            bx_s[rows, re_cols] = bx_s[rows, re_cols] + (pr * cr8 - pi * ci8)
            bx_s[rows, im_cols] = bx_s[rows, im_cols] + (pr * ci8 + pi * cr8)
            return 0

        lax.fori_loop(0, seg, fix_step, 0, unroll=True)

    y = jnp.concatenate(
        [_dot(bx_s[:, h * 2 * STATE_HALF:(h + 1) * 2 * STATE_HALF].astype(BF16), cblk_ref[h]) for h in range(2)],
        axis=1) + d_ref[...] * up
    yg = jax.nn.gelu(y)
    z = _dot(yg.astype(BF16), wglu_ref[...]) + bglu_ref[...]
    out = yg * jax.nn.sigmoid(z)
    for k in range(U_SLABS):
        yp_s[k] = out[:, k * LANES:(k + 1) * LANES]
    for r in range(SCAN_SEGMENTS):
        for k in range(U_SLABS):
            y_ref[r * seg:(r + 1) * seg, k * LANES:(k + 1) * LANES] = (
                yp_s[k, pl.ds(r, seg, stride=SCAN_SEGMENTS), :].astype(y_ref.dtype))


def _s5(u4, bblk, cblk, acoef, apow_re, apow_im, d_row, wglu, bglu, *, batch, seq, tc):
    nchunk = seq // tc
    const2 = lambda b, c: (0, 0)
    const3 = lambda b, c: (0, 0, 0)
    return pl.pallas_call(
        functools.partial(_s5_kernel, tc=tc),
        grid=(batch, nchunk),
        in_specs=[
            pl.BlockSpec((U_SLABS, tc, LANES), lambda b, c: (0, b * nchunk + c, 0)),
            pl.BlockSpec(bblk.shape, const3),
            pl.BlockSpec(cblk.shape, const3),
            pl.BlockSpec(acoef.shape, const2),
            pl.BlockSpec(apow_re.shape, const2),
            pl.BlockSpec(apow_im.shape, const2),
            pl.BlockSpec(d_row.shape, const2),
            pl.BlockSpec(wglu.shape, const2),
            pl.BlockSpec(bglu.shape, const2),
        ],
        out_specs=pl.BlockSpec((tc, SSM_WIDTH), lambda b, c: (b * nchunk + c, 0)),
        out_shape=jax.ShapeDtypeStruct((batch * seq, SSM_WIDTH), BF16),
        scratch_shapes=[
            pltpu.VMEM((tc, SSM_WIDTH), F32),
            pltpu.VMEM((tc, 2 * SSM_STATES), F32),
            pltpu.VMEM((8, 2 * SSM_STATES), F32),
            pltpu.VMEM((SCAN_SEGMENTS, 2 * SSM_STATES), F32),
            pltpu.VMEM((U_SLABS, tc, LANES), F32),
        ],
        compiler_params=pltpu.CompilerParams(
            dimension_semantics=("arbitrary", "arbitrary"), vmem_limit_bytes=VMEM_LIMIT_BYTES),
        name="s5",
    )(u4, bblk, cblk, acoef, apow_re, apow_im, d_row, wglu, bglu)


KEY_TILE = 256
CMP_TILE = 256
VALUE_ROWS = HEAD_DIM + 16
SEL_CHAINS = 4
WORD_BITS = 16


def _nsa_kernel(qt_ref, cp_ref, kp_ref, vt_ref, glt_ref, slope_ref, qg_ref, kg_ref, pe_ref,
                w1_ref, w2k_ref, w2vt_ref, out_ref,
                ks_s, kw_s, kc_s, vct_s, vst_s, vwt_s, bias_s, cmat_s, acc_s, words_s, tiles_s, count_s, amat_s, cvis_s, imp_s, *, seq, tq):
    assert tq == KEY_TILE
    i = pl.program_id(2)
    n_cmp = seq // CMP_STRIDE
    n_sel = seq // SEL_BLOCK
    k_top = min(SEL_TOP_K, n_sel)
    hq = HEADS_PER_GROUP * tq

    @pl.when(i == 0)
    def _prepare_keys_and_values():
        kg = kg_ref[...]
        offs = (lax.broadcasted_iota(jnp.int32, (seq, HEAD_DIM), 0) & (KEY_TILE - 1)).astype(F32)
        offs = jnp.where(lax.broadcasted_iota(jnp.int32, (seq, HEAD_DIM), 1) == 0, offs, 0.0)
        ks_s[...] = jnp.concatenate([_rms_rows(kp_ref[:, 0:HEAD_DIM], kg[1:2]), offs], axis=1).astype(BF16)
        kw_s[...] = jnp.concatenate([_rms_rows(kp_ref[:, HEAD_DIM:2 * HEAD_DIM], kg[2:3]), offs], axis=1).astype(BF16)
        ones = jnp.ones((VALUE_ROWS - HEAD_DIM, seq), F32)
        vst_s[...] = jnp.concatenate([vt_ref[0:HEAD_DIM, :], ones], axis=0).astype(BF16)
        vwt_s[...] = jnp.concatenate([vt_ref[HEAD_DIM:2 * HEAD_DIM, :], ones], axis=0).astype(BF16)

        lo = jnp.zeros((n_cmp, 2 * CMP_HIDDEN), F32)
        hi = jnp.zeros((n_cmp, 2 * CMP_HIDDEN), F32)
        for tt in range(CMP_STRIDE):
            tok = cp_ref[pl.ds(tt, n_cmp, stride=CMP_STRIDE), :]
            lo = lo + _dot((tok + pe_ref[tt:tt + 1, :]).astype(BF16), w1_ref[tt])
            hi = hi + _dot((tok + pe_ref[CMP_STRIDE + tt:CMP_STRIDE + tt + 1, :]).astype(BF16),
                           w1_ref[CMP_STRIDE + tt])
        hidden = jax.nn.gelu(lo + pltpu.roll(hi, n_cmp - 1, 0)).astype(BF16)
        kc_s[...] = _rms_rows(_dot(hidden[:, :CMP_HIDDEN], w2k_ref[...]), kg[0:1]).astype(BF16)
        vct_s[...] = _dot_nt(w2vt_ref[...], hidden[:, CMP_HIDDEN:]).astype(BF16)

        r = lax.broadcasted_iota(jnp.int32, (KEY_TILE, hq), 0)
        a = lax.broadcasted_iota(jnp.int32, (KEY_TILE, hq), 1) & (tq - 1)
        cmat_s[0] = jnp.where(r <= a, 0.0, NEG_INF)
        cmat_s[1] = jnp.where(r > a, 0.0, NEG_INF)

        jrow = lax.broadcasted_iota(jnp.int32, (n_cmp, hq), 0)
        amat_s[...] = slope_ref[0, 0:1, :] * (jrow * CMP_STRIDE + (CMP_BLOCK - 1)).astype(F32)
        cvis_s[...] = (r * CMP_STRIDE + (CMP_BLOCK - 1)) - a
        imp_s[...] = jnp.zeros_like(imp_s)

    qs = i * tq
    lane = lax.broadcasted_iota(jnp.int32, (1, hq), 1)
    t_row = qs + (lane & (tq - 1))
    slope_row = slope_ref[0, 0:1, :]

    qg = qg_ref[...]
    heads = []
    for h in range(HEADS_PER_GROUP):
        qh = qt_ref[h * HEAD_DIM:(h + 1) * HEAD_DIM, :]
        ms = jnp.mean(qh * qh, axis=0, keepdims=True)
        heads.append((qh * lax.rsqrt(ms + EPS) * qg * (HEAD_DIM ** -0.5)).astype(BF16))
    pad = lax.broadcasted_iota(jnp.int32, (HEAD_DIM, hq), 0)
    qst = jnp.concatenate([jnp.concatenate(heads, axis=1),
                           jnp.where(pad == 0, slope_row, 0.0).astype(BF16)], axis=0)

    def cmp_branch(n_tiles):
        def run():
            scores = []
            for jt in range(n_tiles):
                rows = pl.ds(jt * CMP_TILE, CMP_TILE)
                s = _dot(kc_s[rows, :], qst[0:HEAD_DIM]) + amat_s[rows, :]
                visible = cvis_s[...] <= qs - CMP_STRIDE * CMP_TILE * jt
                scores.append(jnp.where(visible, s, NEG_INF))
            m = functools.reduce(jnp.maximum, [jnp.max(s, axis=0, keepdims=True) for s in scores])
            shift = jnp.where(m > 0.5 * NEG_INF, m, -NEG_INF)
            exps = [jnp.exp(s - shift) for s in scores]
            l = functools.reduce(lambda x, y: x + y, [jnp.sum(e, axis=0, keepdims=True) for e in exps])
            r = jnp.where(l > 0.0, 1.0 / l, 0.0)
            acc = jnp.zeros((HEAD_DIM, hq), F32)
            for jt, e in enumerate(exps):
                p = e * r
                imp = p[:, 0:tq]
                for h in range(1, HEADS_PER_GROUP):
                    imp = imp + p[:, h * tq:(h + 1) * tq]
                for k in range(tq // LANES):
                    imp_s[k, pl.ds(jt * CMP_TILE, CMP_TILE), :] = imp[:, k * LANES:(k + 1) * LANES]
                acc = acc + _dot(vct_s[:, pl.ds(jt * CMP_TILE, CMP_TILE)], p.astype(BF16))
            return acc
        return run

    n_cmp_live = (qs + tq - CMP_STRIDE) // CMP_STRIDE
    cmp_tiles = n_cmp // CMP_TILE
    if cmp_tiles >= 2:
        o_cmp = lax.cond(n_cmp_live <= (cmp_tiles // 2) * CMP_TILE, cmp_branch(cmp_tiles // 2), cmp_branch(cmp_tiles))
    else:
        o_cmp = cmp_branch(cmp_tiles)()

    per_block = SEL_BLOCK // CMP_STRIDE
    chunk = [jnp.concatenate([imp_s[k, pl.ds(c, n_sel, stride=per_block), :] for k in range(tq // LANES)], axis=1)
             for c in range(per_block)]
    imp_sel = chunk[0]
    for c in range(1, per_block):
        imp_sel = imp_sel + chunk[c]
    blk = lax.broadcasted_iota(jnp.int32, (n_sel, tq), 0)
    imp_sel = imp_sel + jnp.where(blk == 0, 0.0, pltpu.roll(chunk[per_block - 1], 1, 0))
    cur = jnp.right_shift(t_row[:, 0:tq], SEL_BLOCK.bit_length() - 1)
    forced = (blk == 0) | (blk == cur) | (blk == cur - 1)
    valid = blk <= cur
    blk_f = blk.astype(F32)

    def pick_round(_, carry):
        work, taken = carry
        top = jnp.max(work, axis=0, keepdims=True)
        first = jnp.min(jnp.where(work == top, blk_f, float(n_sel)), axis=0, keepdims=True)
        pick = blk_f == first
        return jnp.where(pick, REMOVED, work), jnp.where(pick, 1.0, taken)

    _, taken = lax.fori_loop(
        0, max(k_top - 3, 0), pick_round,
        (jnp.where(valid & jnp.logical_not(forced), imp_sel, NEG_INF), jnp.where(forced & valid, 1.0, 0.0)),
        unroll=True)
    chosen = taken > 0.5
    bias = jnp.where(chosen, 0.0, NEG_INF)
    bias_s[...] = jnp.concatenate([bias] * HEADS_PER_GROUP, axis=1)

    blocks_per_tile = KEY_TILE // SEL_BLOCK
    hit = jnp.where(chosen & (blk < blocks_per_tile * i), 1.0, 0.0)
    blk_hit = jnp.max(hit, axis=1, keepdims=True)
    weight = jnp.left_shift(1, lax.broadcasted_iota(jnp.int32, (n_sel, 1), 0) & (WORD_BITS - 1)).astype(F32)
    packed = blk_hit * weight
    for w in range(n_sel // WORD_BITS):
        word = jnp.sum(packed[w * WORD_BITS:(w + 1) * WORD_BITS], axis=0, keepdims=True).astype(jnp.int32)
        words_s[w] = word[0, 0]

    tiles_per_word = WORD_BITS // blocks_per_tile
    count_s[0] = 0
    for w in range(n_sel // WORD_BITS):
        @pl.when(words_s[w] != 0)
        def _append_tiles_of_word():
            word = words_s[w]
            count = count_s[0]
            for k in range(tiles_per_word):
                bits = jnp.right_shift(word, blocks_per_tile * k) & ((1 << blocks_per_tile) - 1)
                tiles_s[count] = w * tiles_per_word + k
                count = count + jnp.where(bits != 0, 1, 0)
            count_s[0] = count

    n_active = count_s[0]
    for c in range(SEL_CHAINS):
        tiles_s[n_active + c] = -1

    def tiles_update(kts, k_s, vt_s, masks, biases_per_tile, ms, accs):
        k0s = [pl.multiple_of(kt * KEY_TILE, KEY_TILE) for kt in kts]
        scores = []
        for k0, mask in zip(k0s, masks):
            s = _dot(k_s[pl.ds(k0, KEY_TILE), :], qst)
            scores.append(s if mask is None else s + cmat_s[mask])
        new_ms, alphas, probs = [], [], []
        for s, k0, biases, m in zip(scores, k0s, biases_per_tile, ms):
            rv = slope_row * (t_row - k0).astype(F32)
            rows = KEY_TILE // len(biases)
            parts = [s[b * rows:(b + 1) * rows] for b in range(len(biases))]
            m_new = m
            for part, bias_row in zip(parts, biases):
                m_new = jnp.maximum(m_new, jnp.max(part, axis=0, keepdims=True) + (bias_row - rv))
            seen = m_new > 0.5 * NEG_INF
            ps = []
            for part, bias_row in zip(parts, biases):
                shift = jnp.where(seen, m_new + (rv - bias_row), -NEG_INF)
                ps.append(jnp.exp(part - shift))
            new_ms.append(m_new)
            alphas.append(jnp.exp(m - m_new))
            probs.append((jnp.concatenate(ps, axis=0) if len(ps) > 1 else ps[0]).astype(BF16))
        new_accs = [alpha * acc + _dot(vt_s[:, pl.ds(k0, KEY_TILE)], p)
                    for alpha, acc, k0, p in zip(alphas, accs, k0s, probs)]
        return new_ms, new_accs

    def merge(ms, accs):
        m = functools.reduce(jnp.maximum, ms)
        acc = functools.reduce(lambda x, y: x + y, [jnp.exp(mc - m) * ac for mc, ac in zip(ms, accs)])
        return acc[0:HEAD_DIM] * (1.0 / acc[HEAD_DIM:HEAD_DIM + 1])

    m_init = jnp.full((1, hq), NEG_INF, F32)
    acc_init = jnp.zeros((VALUE_ROWS, hq), F32)
    dead_row = jnp.full((1, hq), NEG_INF, F32)
    live_row = jnp.zeros((1, hq), F32)

    def block_biases(kt):
        return [bias_s[pl.ds(blocks_per_tile * kt + b, 1), :] for b in range(blocks_per_tile)]

    for c in range(SEL_CHAINS):
        acc_s[c] = acc_init

    def sel_step(j, ms):
        entries = [tiles_s[j * SEL_CHAINS + c] for c in range(SEL_CHAINS)]
        kts = [jnp.maximum(e, 0) for e in entries]
        biases = [[jnp.where(e < 0, dead_row, row) for row in block_biases(kt)] for e, kt in zip(entries, kts)]
        new_ms, new_accs = tiles_update(kts, ks_s, vst_s, [None] * SEL_CHAINS, biases, ms,
                                        [acc_s[c] for c in range(SEL_CHAINS)])
        for c in range(SEL_CHAINS):
            acc_s[c] = new_accs[c]
        return tuple(new_ms)

    ms = lax.fori_loop(0, (n_active + SEL_CHAINS - 1) // SEL_CHAINS, sel_step, (m_init,) * SEL_CHAINS)

    far = WINDOW // KEY_TILE
    kts = [i - d for d in range(far + 1)]
    win_masks = [0 if d == 0 else (1 if d == far else None) for d in range(far + 1)]
    win_biases = [[jnp.where(kt < 0, dead_row, live_row)] for kt in kts]
    n_win = far + 1
    ms_d, accs_d = tiles_update([i], ks_s, vst_s, [0], [block_biases(i)], [m_init], [acc_init])
    ms_w, accs_w = tiles_update([jnp.maximum(kt, 0) for kt in kts], kw_s, vwt_s, win_masks, win_biases,
                                [m_init] * n_win, [acc_init] * n_win)
    o_sel = merge(list(ms) + ms_d, [acc_s[c] for c in range(SEL_CHAINS)] + accs_d)
    o_win = merge(ms_w, accs_w)

    gl = glt_ref[...]

    def gate_row(branch):
        rows = [jax.nn.sigmoid(gl[branch * HEADS_PER_GROUP + h:branch * HEADS_PER_GROUP + h + 1, :])
                for h in range(HEADS_PER_GROUP)]
        return jnp.concatenate(rows, axis=1)

    ot = gate_row(0) * o_cmp + gate_row(1) * o_sel + gate_row(2) * o_win
    out_ref[...] = jnp.concatenate(
        [ot[:, h * tq:(h + 1) * tq] for h in range(HEADS_PER_GROUP)], axis=0).astype(out_ref.dtype)


def _nsa(qt, cp, kp, vt, glt, slopes, qg, kg, pe, w1, w2k, w2vt, *, batch, seq, tq):
    nq = seq // tq
    n_cmp = seq // CMP_STRIDE
    n_sel = seq // SEL_BLOCK
    hq = HEADS_PER_GROUP * tq
    const2 = lambda b, g, i: (0, 0)
    const3 = lambda b, g, i: (0, 0, 0)
    return pl.pallas_call(
        functools.partial(_nsa_kernel, seq=seq, tq=tq),
        grid=(batch, NSA_KV_GROUPS, nq),
        in_specs=[
            pl.BlockSpec((HEADS_PER_GROUP * HEAD_DIM, tq), lambda b, g, i: (g, b * nq + i)),
            pl.BlockSpec((seq, 2 * HEAD_DIM), lambda b, g, i: (b, g)),
            pl.BlockSpec((seq, 2 * HEAD_DIM), lambda b, g, i: (b, g)),
            pl.BlockSpec((2 * HEAD_DIM, seq), lambda b, g, i: (g, b)),
            pl.BlockSpec((GATE_ROWS, tq), lambda b, g, i: (g, b * nq + i)),
            pl.BlockSpec((1, 8, hq), lambda b, g, i: (g, 0, 0)),
            pl.BlockSpec(qg.shape, const2),
            pl.BlockSpec(kg.shape, const2),
            pl.BlockSpec(pe.shape, const2),
            pl.BlockSpec(w1.shape, const3),
            pl.BlockSpec(w2k.shape, const2),
            pl.BlockSpec(w2vt.shape, const2),
        ],
        out_specs=pl.BlockSpec((HEADS_PER_GROUP * HEAD_DIM, tq), lambda b, g, i: (g, b * nq + i)),
        out_shape=jax.ShapeDtypeStruct((NSA_WIDTH, batch * seq), BF16),
        scratch_shapes=[
            pltpu.VMEM((seq, 2 * HEAD_DIM), BF16),
            pltpu.VMEM((seq, 2 * HEAD_DIM), BF16),
            pltpu.VMEM((n_cmp, HEAD_DIM), BF16),
            pltpu.VMEM((HEAD_DIM, n_cmp), BF16),
            pltpu.VMEM((VALUE_ROWS, seq), BF16),
            pltpu.VMEM((VALUE_ROWS, seq), BF16),
            pltpu.VMEM((n_sel, hq), F32),
            pltpu.VMEM((2, KEY_TILE, hq), F32),
            pltpu.VMEM((SEL_CHAINS, VALUE_ROWS, hq), F32),
            pltpu.SMEM((n_sel // WORD_BITS,), jnp.int32),
            pltpu.SMEM((seq // KEY_TILE + SEL_CHAINS,), jnp.int32),
            pltpu.SMEM((1,), jnp.int32),
            pltpu.VMEM((n_cmp, hq), F32),
            pltpu.VMEM((CMP_TILE, hq), jnp.int32),
            pltpu.VMEM((tq // LANES, n_cmp, LANES), F32),
        ],
        compiler_params=pltpu.CompilerParams(
            dimension_semantics=("arbitrary", "arbitrary", "arbitrary"), vmem_limit_bytes=VMEM_LIMIT_BYTES),
        name="nsa",
    )(qt, cp, kp, vt, glt, slopes, qg, kg, pe, w1, w2k, w2vt)


def _final_kernel(x_ref, ys_ref, ynt_ref, g1_ref, wmg_ref, wps_ref, wpn_ref, wout_ref, g2_ref, wup_ref, wdn_ref,
                  o_ref):
    x = x_ref[...]
    h = _rms_rows(x, g1_ref[...]).astype(BF16)
    mg = _dot(h, wmg_ref[...])
    ps = _dot(ys_ref[...], wps_ref[...])
    pn = _dot_tn(ynt_ref[...], wpn_ref[...])
    merged = jax.nn.sigmoid(mg[:, :D_MODEL]) * ps + jax.nn.sigmoid(mg[:, D_MODEL:]) * pn
    x1 = x + _dot(merged.astype(BF16), wout_ref[...])
    h2 = _rms_rows(x1, g2_ref[...]).astype(BF16)
    a = jnp.maximum(_dot(h2, wup_ref[...]), 0.0)
    o_ref[...] = x1 + _dot((a * a).astype(BF16), wdn_ref[...])


def _final(x2, ys, ynt, g1, wmg, wps, wpn, wout, g2, wup, wdn, *, tm):
    n = x2.shape[0]
    const = lambda i: (0, 0)
    resident = lambda w: pl.BlockSpec(w.shape, const, pipeline_mode=pl.Buffered(1))
    return pl.pallas_call(
        _final_kernel,
        grid=(n // tm,),
        in_specs=[
            pl.BlockSpec((tm, D_MODEL), lambda i: (i, 0)),
            pl.BlockSpec((tm, SSM_WIDTH), lambda i: (i, 0)),
            pl.BlockSpec((NSA_WIDTH, tm), lambda i: (0, i)),
            resident(g1), resident(wmg), resident(wps), resident(wpn), resident(wout),
            resident(g2), resident(wup), resident(wdn),
        ],
        out_specs=pl.BlockSpec((tm, D_MODEL), lambda i: (i, 0)),
        out_shape=jax.ShapeDtypeStruct((n, D_MODEL), F32),
        compiler_params=pltpu.CompilerParams(
            dimension_semantics=("arbitrary",), vmem_limit_bytes=VMEM_LIMIT_BYTES),
        name="final",
    )(x2, ys, ynt, g1, wmg, wps, wpn, wout, g2, wup, wdn)


def _pack_in_proj(w_in):
    o1 = SSM_WIDTH
    o2 = o1 + NSA_WIDTH
    o3 = o2 + KV_WIDTH
    o4 = o3 + 3 * NSA_HEADS
    w_u, w_q, w_kv, w_gl = w_in[:, :o1], w_in[:, o1:o2], w_in[:, o2:o3], w_in[:, o3:o4]
    w_mg = w_in[:, o4:]
    kv = w_kv.reshape(D_MODEL, 3, 2, NSA_KV_GROUPS, HEAD_DIM)
    groups = range(NSA_KV_GROUPS)
    cpack = [jnp.concatenate([kv[:, 0, 0, g], kv[:, 0, 1, g]], axis=1) for g in groups]
    kpack = [jnp.concatenate([kv[:, 1, 0, g], kv[:, 2, 0, g]], axis=1) for g in groups]
    wa = jnp.concatenate([w_u] + cpack + kpack, axis=1)
    vrows = [jnp.concatenate([kv[:, 1, 1, g], kv[:, 2, 1, g]], axis=1) for g in range(NSA_KV_GROUPS)]
    gl = w_gl.reshape(D_MODEL, 3, NSA_KV_GROUPS, HEADS_PER_GROUP)
    pad = jnp.zeros((D_MODEL, GATE_ROWS - 3 * HEADS_PER_GROUP), w_in.dtype)
    glrows = [jnp.concatenate([gl[:, 0, g], gl[:, 1, g], gl[:, 2, g], pad], axis=1) for g in range(NSA_KV_GROUPS)]
    wbt = jnp.concatenate([w_q] + vrows + glrows, axis=1).T
    return wa.astype(BF16), wbt.astype(BF16), w_mg.astype(BF16)


def _pack_compress(pe_k, pe_v, w1k, w1v):
    pe = jnp.concatenate([pe_k, pe_v], axis=1)
    w1k = w1k.reshape(CMP_BLOCK, HEAD_DIM, CMP_HIDDEN)
    w1v = w1v.reshape(CMP_BLOCK, HEAD_DIM, CMP_HIDDEN)
    zero = jnp.zeros_like(w1k)
    w1 = jnp.concatenate([jnp.concatenate([w1k, zero], axis=2), jnp.concatenate([zero, w1v], axis=2)], axis=1)
    return pe, w1.astype(BF16)


def _pack_s5(a_re, a_im, log_dt, b_re, b_im, c_re, c_im, seg):
    dt = jnp.exp(log_dt)[:, None]
    mag = jnp.exp(a_re * dt)
    abar_re = mag * jnp.cos(a_im * dt)
    abar_im = mag * jnp.sin(a_im * dt)
    den = a_re * a_re + a_im * a_im
    nr = abar_re - 1.0
    fr = (nr * a_re + abar_im * a_im) / den
    fi = (abar_im * a_re - nr * a_im) / den
    bbar_re = fr[..., None] * b_re - fi[..., None] * b_im
    bbar_im = fr[..., None] * b_im + fi[..., None] * b_re
    gh = SSM_GROUPS // 2
    eye = jnp.eye(gh, dtype=F32)

    def in_blocks(m):
        return (m.transpose(0, 2, 1)[:, :, None, :] * eye[:, None, :, None]).reshape(gh * SSM_GROUP, gh * SSM_STATE)

    def out_blocks(m):
        return (m.transpose(0, 2, 1)[:, :, None, :] * eye[:, None, :, None]).reshape(gh * SSM_STATE, gh * SSM_GROUP)

    halves = [slice(0, gh), slice(gh, SSM_GROUPS)]
    bblk = jnp.stack([jnp.concatenate([in_blocks(bbar_re[h]), in_blocks(bbar_im[h])], axis=1) for h in halves])
    cblk = jnp.stack([jnp.concatenate([out_blocks(c_re[h]), -out_blocks(c_im[h])], axis=0) for h in halves])

    k = jnp.arange(1, seg + 1, dtype=F32)[:, None, None]
    pmag = jnp.exp(k * (a_re * dt))
    apow_re = (pmag * jnp.cos(k * (a_im * dt))).reshape(seg, SSM_STATES)
    apow_im = (pmag * jnp.sin(k * (a_im * dt))).reshape(seg, SSM_STATES)
    acoef = jnp.zeros((8, SSM_STATES), F32)
    acoef = acoef.at[0].set(abar_re.reshape(-1)).at[1].set(abar_im.reshape(-1))
    acoef = acoef.at[2].set(apow_re[seg - 1]).at[3].set(apow_im[seg - 1])
    return bblk.astype(BF16), cblk.astype(BF16), acoef, apow_re, apow_im


def kernel(x, norm1_g, w_in, ssm_a_re, ssm_a_im, ssm_log_dt, ssm_b_re, ssm_b_im, ssm_c_re, ssm_c_im, ssm_d, ssm_w_glu, ssm_b_glu, cmp_pe_k, cmp_pe_v, cmp_wk1, cmp_wk2, cmp_wv1, cmp_wv2, q_norm_g, k_norm_g, w_proj_ssm, w_proj_nsa, w_out, norm2_g, w_up, w_down):
    batch, seq, d_model = x.shape
    assert d_model == D_MODEL and seq % 512 == 0
    n = batch * seq
    tq = KEY_TILE
    depth = w_in.shape[0]
    head = jnp.arange(1, NSA_HEADS + 1, dtype=F32)
    slopes = jnp.exp2(-8.0 * head / NSA_HEADS).reshape(NSA_KV_GROUPS, 1, HEADS_PER_GROUP, 1)
    slopes = jnp.broadcast_to(slopes, (NSA_KV_GROUPS, 8, HEADS_PER_GROUP, tq)).reshape(NSA_KV_GROUPS, 8, -1)

    x2 = x.reshape(n, D_MODEL)
    for l in range(depth):
        wa, wbt, wmg = _pack_in_proj(w_in[l])
        g1 = norm1_g[l].reshape(1, D_MODEL)
        u, cp, kp, qt, vt, glt = _in_proj(x2, g1, wa, wbt, tm=512)

        tc = 256
        bblk, cblk, acoef, apow_re, apow_im = _pack_s5(ssm_a_re[l], ssm_a_im[l], ssm_log_dt[l], ssm_b_re[l],
                                                       ssm_b_im[l], ssm_c_re[l], ssm_c_im[l], tc // SCAN_SEGMENTS)
        ys = _s5(u, bblk, cblk, acoef, apow_re, apow_im, ssm_d[l].reshape(1, SSM_WIDTH),
                 ssm_w_glu[l].astype(BF16), ssm_b_glu[l].reshape(1, SSM_WIDTH), batch=batch, seq=seq, tc=tc)

        qg = jnp.broadcast_to(q_norm_g[l].reshape(HEAD_DIM, 1), (HEAD_DIM, tq))
        kg = jnp.zeros((8, HEAD_DIM), F32).at[0:3].set(k_norm_g[l])
        pe, w1 = _pack_compress(cmp_pe_k[l], cmp_pe_v[l], cmp_wk1[l], cmp_wv1[l])
        ynt = _nsa(qt, cp, kp, vt, glt, slopes, qg, kg, pe, w1,
                   cmp_wk2[l].astype(BF16), cmp_wv2[l].T.astype(BF16), batch=batch, seq=seq, tq=tq)

        x2 = _final(x2, ys, ynt, g1, wmg, w_proj_ssm[l].astype(BF16), w_proj_nsa[l].astype(BF16),
                    w_out[l].astype(BF16), norm2_g[l].reshape(1, D_MODEL), w_up[l].astype(BF16),
                    w_down[l].astype(BF16), tm=256)
    return x2.reshape(batch, seq, D_MODEL)
```

```python
import functools

import jax
import jax.numpy as jnp
from jax import lax
from jax.experimental import pallas as pl
from jax.experimental.pallas import tpu as pltpu

D_MODEL = 1024
SSM_WIDTH = D_MODEL // 2
SSM_GROUP = 16
SSM_GROUPS = SSM_WIDTH // SSM_GROUP
SSM_STATE = 64
SSM_STATES = SSM_GROUPS * SSM_STATE
NSA_HEADS = 8
NSA_KV_GROUPS = 2
HEADS_PER_GROUP = NSA_HEADS // NSA_KV_GROUPS
HEAD_DIM = 64
NSA_WIDTH = NSA_HEADS * HEAD_DIM
CMP_STRIDE = 16
CMP_BLOCK = 2 * CMP_STRIDE
CMP_HIDDEN = 2 * HEAD_DIM
SEL_BLOCK = 64
SEL_TOP_K = 16
WINDOW = 512
D_FF = 4 * D_MODEL
N_MIXERS = 2
KV_WIDTH = 3 * 2 * NSA_KV_GROUPS * HEAD_DIM
EPS = 1e-6
NEG_INF = -1e30
REMOVED = -3e38

LANES = 128
VMEM_LIMIT_BYTES = 56 * 1024 * 1024

GATE_ROWS = 16

F32 = jnp.float32
BF16 = jnp.bfloat16


def _dot(a, b):
    return jnp.dot(a, b, preferred_element_type=F32)


def _dot_nt(a, b):
    return lax.dot_general(a, b, (((1,), (1,)), ((), ())), preferred_element_type=F32)


def _dot_tn(a, b):
    return lax.dot_general(a, b, (((0,), (0,)), ((), ())), preferred_element_type=F32)


def _rms_rows(x, g):
    ms = jnp.mean(x * x, axis=-1, keepdims=True)
    return x * lax.rsqrt(ms + EPS) * g


def _inproj_kernel(x_ref, g_ref, wa_ref, wbt_ref, u_ref, cp_ref, kp_ref, qt_ref, vt_ref, glt_ref):
    h = _rms_rows(x_ref[...], g_ref[...]).astype(BF16)
    a = _dot(h, wa_ref[...])
    for k in range(SSM_WIDTH // LANES):
        u_ref[k] = a[:, k * LANES:(k + 1) * LANES]
    cp_ref[...] = a[:, SSM_WIDTH:SSM_WIDTH + 4 * HEAD_DIM]
    kp_ref[...] = a[:, SSM_WIDTH + 4 * HEAD_DIM:]
    bt = _dot_nt(wbt_ref[...], h)
    qt_ref[...] = bt[:NSA_WIDTH]
    vt_ref[...] = bt[NSA_WIDTH:NSA_WIDTH + 4 * HEAD_DIM]
    glt_ref[...] = bt[NSA_WIDTH + 4 * HEAD_DIM:]


def _in_proj(x2, g1, wa, wbt, *, tm):
    n = x2.shape[0]
    nb = wbt.shape[0]
    const = lambda i: (0, 0)
    return pl.pallas_call(
        _inproj_kernel,
        grid=(n // tm,),
        in_specs=[
            pl.BlockSpec((tm, D_MODEL), lambda i: (i, 0)),
            pl.BlockSpec((1, D_MODEL), const),
            pl.BlockSpec(wa.shape, const),
            pl.BlockSpec(wbt.shape, const),
        ],
        out_specs=[
            pl.BlockSpec((SSM_WIDTH // LANES, tm, LANES), lambda i: (0, i, 0)),
            pl.BlockSpec((tm, 4 * HEAD_DIM), lambda i: (i, 0)),
            pl.BlockSpec((tm, 4 * HEAD_DIM), lambda i: (i, 0)),
            pl.BlockSpec((NSA_WIDTH, tm), lambda i: (0, i)),
            pl.BlockSpec((4 * HEAD_DIM, tm), lambda i: (0, i)),
            pl.BlockSpec((NSA_KV_GROUPS * GATE_ROWS, tm), lambda i: (0, i)),
        ],
        out_shape=[
            jax.ShapeDtypeStruct((SSM_WIDTH // LANES, n, LANES), F32),
            jax.ShapeDtypeStruct((n, 4 * HEAD_DIM), F32),
            jax.ShapeDtypeStruct((n, 4 * HEAD_DIM), F32),
            jax.ShapeDtypeStruct((NSA_WIDTH, n), F32),
            jax.ShapeDtypeStruct((4 * HEAD_DIM, n), F32),
            jax.ShapeDtypeStruct((NSA_KV_GROUPS * GATE_ROWS, n), F32),
        ],
        compiler_params=pltpu.CompilerParams(
            dimension_semantics=("arbitrary",), vmem_limit_bytes=VMEM_LIMIT_BYTES),
        name="in_proj",
    )(x2, g1, wa, wbt)


SCAN_SEGMENTS = 8
SCAN_SLAB = 512
STATE_HALF = SSM_STATES // 2
U_SLABS = SSM_WIDTH // LANES


def _s5_kernel(u_ref, bblk_ref, cblk_ref, acoef_ref, apow_re_ref, apow_im_ref, d_ref, wglu_ref, bglu_ref, y_ref,
               up_s, bx_s, carry_s, cseg_s, yp_s, *, tc):
    seg = tc // SCAN_SEGMENTS

    @pl.when(pl.program_id(1) == 0)
    def _reset():
        carry_s[...] = jnp.zeros_like(carry_s)

    for tl in range(seg):
        for k in range(U_SLABS):
            up_s[tl * SCAN_SEGMENTS:(tl + 1) * SCAN_SEGMENTS, k * LANES:(k + 1) * LANES] = (
                u_ref[k, pl.ds(tl, SCAN_SEGMENTS, stride=seg), :])
    up = up_s[...]
    half_w = SSM_WIDTH // 2
    for h in range(2):
        bx_s[:, h * 2 * STATE_HALF:(h + 1) * 2 * STATE_HALF] = _dot(
            up[:, h * half_w:(h + 1) * half_w].astype(BF16), bblk_ref[h])

    for slab in range(SSM_STATES // SCAN_SLAB):
        base = (slab * SCAN_SLAB // STATE_HALF) * 2 * STATE_HALF + (slab * SCAN_SLAB) % STATE_HALF
        re_cols = pl.ds(base, SCAN_SLAB)
        im_cols = pl.ds(base + STATE_HALF, SCAN_SLAB)
        nat = pl.ds(slab * SCAN_SLAB, SCAN_SLAB)
        a_re = acoef_ref[0:1, nat]
        a_im = acoef_ref[1:2, nat]
        al_re = acoef_ref[2:3, nat]
        al_im = acoef_ref[3:4, nat]

        def scan_step(tl, carry):
            xr, xi = carry
            rows = pl.ds(pl.multiple_of(tl * SCAN_SEGMENTS, SCAN_SEGMENTS), SCAN_SEGMENTS)
            nr = a_re * xr - a_im * xi + bx_s[rows, re_cols]
            ni = a_re * xi + a_im * xr + bx_s[rows, im_cols]
            bx_s[rows, re_cols] = nr
            bx_s[rows, im_cols] = ni
            return nr, ni

        zero = jnp.zeros((SCAN_SEGMENTS, SCAN_SLAB), F32)
        end_r, end_i = lax.fori_loop(0, seg, scan_step, (zero, zero), unroll=True)

        cr = carry_s[0:1, re_cols]
        ci = carry_s[0:1, im_cols]
        for r in range(SCAN_SEGMENTS):
            cseg_s[r:r + 1, re_cols] = cr
            cseg_s[r:r + 1, im_cols] = ci
            nr = al_re * cr - al_im * ci + end_r[r:r + 1]
            ni = al_re * ci + al_im * cr + end_i[r:r + 1]
            cr, ci = nr, ni
        carry_s[0:1, re_cols] = cr
        carry_s[0:1, im_cols] = ci

        cr8 = cseg_s[:, re_cols]
        ci8 = cseg_s[:, im_cols]

        def fix_step(tl, _):
            rows = pl.ds(pl.multiple_of(tl * SCAN_SEGMENTS, SCAN_SEGMENTS), SCAN_SEGMENTS)
            pr = apow_re_ref[pl.ds(tl, 1), nat]
            pi = apow_im_ref[pl.ds(tl, 1), nat]
            bx_s[rows, re_cols] = bx_s[rows, re_cols] + (pr * cr8 - pi * ci8)
            bx_s[rows, im_cols] = bx_s[rows, im_cols] + (pr * ci8 + pi * cr8)
            return 0

        lax.fori_loop(0, seg, fix_step, 0, unroll=True)

    y = jnp.concatenate(
        [_dot(bx_s[:, h * 2 * STATE_HALF:(h + 1) * 2 * STATE_HALF].astype(BF16), cblk_ref[h]) for h in range(2)],
        axis=1) + d_ref[...] * up
    yg = jax.nn.gelu(y)
    z = _dot(yg.astype(BF16), wglu_ref[...]) + bglu_ref[...]
    out = yg * jax.nn.sigmoid(z)
    for k in range(U_SLABS):
        yp_s[k] = out[:, k * LANES:(k + 1) * LANES]
    for r in range(SCAN_SEGMENTS):
        for k in range(U_SLABS):
            y_ref[r * seg:(r + 1) * seg, k * LANES:(k + 1) * LANES] = (
                yp_s[k, pl.ds(r, seg, stride=SCAN_SEGMENTS), :].astype(y_ref.dtype))


def _s5(u4, bblk, cblk, acoef, apow_re, apow_im, d_row, wglu, bglu, *, batch, seq, tc):
    nchunk = seq // tc
    const2 = lambda b, c: (0, 0)
    const3 = lambda b, c: (0, 0, 0)
    return pl.pallas_call(
        functools.partial(_s5_kernel, tc=tc),
        grid=(batch, nchunk),
        in_specs=[
            pl.BlockSpec((U_SLABS, tc, LANES), lambda b, c: (0, b * nchunk + c, 0)),
            pl.BlockSpec(bblk.shape, const3),
            pl.BlockSpec(cblk.shape, const3),
            pl.BlockSpec(acoef.shape, const2),
            pl.BlockSpec(apow_re.shape, const2),
            pl.BlockSpec(apow_im.shape, const2),
            pl.BlockSpec(d_row.shape, const2),
            pl.BlockSpec(wglu.shape, const2),
            pl.BlockSpec(bglu.shape, const2),
        ],
        out_specs=pl.BlockSpec((tc, SSM_WIDTH), lambda b, c: (b * nchunk + c, 0)),
        out_shape=jax.ShapeDtypeStruct((batch * seq, SSM_WIDTH), BF16),
        scratch_shapes=[
            pltpu.VMEM((tc, SSM_WIDTH), F32),
            pltpu.VMEM((tc, 2 * SSM_STATES), F32),
            pltpu.VMEM((8, 2 * SSM_STATES), F32),
            pltpu.VMEM((SCAN_SEGMENTS, 2 * SSM_STATES), F32),
            pltpu.VMEM((U_SLABS, tc, LANES), F32),
        ],
        compiler_params=pltpu.CompilerParams(
            dimension_semantics=("arbitrary", "arbitrary"), vmem_limit_bytes=VMEM_LIMIT_BYTES),
        name="s5",
    )(u4, bblk, cblk, acoef, apow_re, apow_im, d_row, wglu, bglu)


KEY_TILE = 256
CMP_TILE = 256
LOG2E = 1.4426950408889634
SLOPE_PARTS = 3
VALUE_ROWS = HEAD_DIM + 16
SEL_CHAINS = 4
WORD_BITS = 16


def _nsa_kernel(qt_ref, cp_ref, kp_ref, vt_ref, glt_ref, slope_ref, qg_ref, kg_ref, pe_ref,
                w1_ref, w2k_ref, w2vt_ref, out_ref,
                ks_s, kw_s, kc_s, vct_s, vst_s, vwt_s, bias_s, cmat_s, acc_s, words_s, tiles_s, count_s, amat_s, cvis_s, imp_s, *, seq, tq):
    assert tq == KEY_TILE
    i = pl.program_id(2)
    n_cmp = seq // CMP_STRIDE
    n_sel = seq // SEL_BLOCK
    k_top = min(SEL_TOP_K, n_sel)
    hq = HEADS_PER_GROUP * tq

    @pl.when(i == 0)
    def _prepare_keys_and_values():
        kg = kg_ref[...]
        offs = (lax.broadcasted_iota(jnp.int32, (seq, HEAD_DIM), 0) & (KEY_TILE - 1)).astype(F32)
        offs = jnp.where(lax.broadcasted_iota(jnp.int32, (seq, HEAD_DIM), 1) < SLOPE_PARTS, offs, 0.0)
        ks_s[...] = jnp.concatenate([_rms_rows(kp_ref[:, 0:HEAD_DIM], kg[1:2]), offs], axis=1).astype(BF16)
        kw_s[...] = jnp.concatenate([_rms_rows(kp_ref[:, HEAD_DIM:2 * HEAD_DIM], kg[2:3]), offs], axis=1).astype(BF16)
        ones = jnp.ones((VALUE_ROWS - HEAD_DIM, seq), F32)
        vst_s[...] = jnp.concatenate([vt_ref[0:HEAD_DIM, :], ones], axis=0).astype(BF16)
        vwt_s[...] = jnp.concatenate([vt_ref[HEAD_DIM:2 * HEAD_DIM, :], ones], axis=0).astype(BF16)

        lo = jnp.zeros((n_cmp, 2 * CMP_HIDDEN), F32)
        hi = jnp.zeros((n_cmp, 2 * CMP_HIDDEN), F32)
        for tt in range(CMP_STRIDE):
            tok = cp_ref[pl.ds(tt, n_cmp, stride=CMP_STRIDE), :]
            lo = lo + _dot((tok + pe_ref[tt:tt + 1, :]).astype(BF16), w1_ref[tt])
            hi = hi + _dot((tok + pe_ref[CMP_STRIDE + tt:CMP_STRIDE + tt + 1, :]).astype(BF16),
                           w1_ref[CMP_STRIDE + tt])
        hidden = jax.nn.gelu(lo + pltpu.roll(hi, n_cmp - 1, 0)).astype(BF16)
        kc_s[...] = _rms_rows(_dot(hidden[:, :CMP_HIDDEN], w2k_ref[...]), kg[0:1]).astype(BF16)
        vct_s[...] = _dot_nt(w2vt_ref[...], hidden[:, CMP_HIDDEN:]).astype(BF16)

        r = lax.broadcasted_iota(jnp.int32, (KEY_TILE, hq), 0)
        a = lax.broadcasted_iota(jnp.int32, (KEY_TILE, hq), 1) & (tq - 1)
        cmat_s[0] = jnp.where(r <= a, 0.0, NEG_INF)
        cmat_s[1] = jnp.where(r > a, 0.0, NEG_INF)

        jrow = lax.broadcasted_iota(jnp.int32, (n_cmp, hq), 0)
        amat_s[...] = (slope_ref[0, 0:1, :] * LOG2E) * (jrow * CMP_STRIDE + (CMP_BLOCK - 1)).astype(F32)
        cvis_s[...] = (r * CMP_STRIDE + (CMP_BLOCK - 1)) - a
        imp_s[...] = jnp.zeros_like(imp_s)

    qs = i * tq
    lane = lax.broadcasted_iota(jnp.int32, (1, hq), 1)
    t_row = qs + (lane & (tq - 1))
    slope_row = slope_ref[0, 0:1, :] * LOG2E

    qg = qg_ref[...]
    heads = []
    for h in range(HEADS_PER_GROUP):
        qh = qt_ref[h * HEAD_DIM:(h + 1) * HEAD_DIM, :]
        ms = jnp.mean(qh * qh, axis=0, keepdims=True)
        heads.append((qh * lax.rsqrt(ms + EPS) * qg * (HEAD_DIM ** -0.5 * LOG2E)).astype(BF16))
    pad = lax.broadcasted_iota(jnp.int32, (HEAD_DIM, hq), 0)
    slope_rows = jnp.zeros((HEAD_DIM, hq), F32)
    rest = slope_row
    for part in range(SLOPE_PARTS):
        piece = rest.astype(BF16).astype(F32)
        slope_rows = jnp.where(pad == part, piece, slope_rows)
        rest = rest - piece
    qst = jnp.concatenate([jnp.concatenate(heads, axis=1), slope_rows.astype(BF16)], axis=0)

    def cmp_branch(n_tiles):
        def run():
            scores = []
            for jt in range(n_tiles):
                rows = pl.ds(jt * CMP_TILE, CMP_TILE)
                s = _dot(kc_s[rows, :], qst[0:HEAD_DIM]) + amat_s[rows, :]
                visible = cvis_s[...] <= qs - CMP_STRIDE * CMP_TILE * jt
                scores.append(jnp.where(visible, s, NEG_INF))
            m = functools.reduce(jnp.maximum, [jnp.max(s, axis=0, keepdims=True) for s in scores])
            shift = jnp.where(m > 0.5 * NEG_INF, m, -NEG_INF)
            exps = [jnp.exp2(s - shift) for s in scores]
            l = functools.reduce(lambda x, y: x + y, [jnp.sum(e, axis=0, keepdims=True) for e in exps])
            r = jnp.where(l > 0.0, 1.0 / l, 0.0)
            acc = jnp.zeros((HEAD_DIM, hq), F32)
            for jt, e in enumerate(exps):
                p = e * r
                imp = p[:, 0:tq]
                for h in range(1, HEADS_PER_GROUP):
                    imp = imp + p[:, h * tq:(h + 1) * tq]
                for k in range(tq // LANES):
                    imp_s[k, pl.ds(jt * CMP_TILE, CMP_TILE), :] = imp[:, k * LANES:(k + 1) * LANES]
                acc = acc + _dot(vct_s[:, pl.ds(jt * CMP_TILE, CMP_TILE)], p.astype(BF16))
            return acc
        return run

    n_cmp_live = (qs + tq - CMP_STRIDE) // CMP_STRIDE
    cmp_tiles = n_cmp // CMP_TILE
    if cmp_tiles >= 2:
        o_cmp = lax.cond(n_cmp_live <= (cmp_tiles // 2) * CMP_TILE, cmp_branch(cmp_tiles // 2), cmp_branch(cmp_tiles))
    else:
        o_cmp = cmp_branch(cmp_tiles)()

    per_block = SEL_BLOCK // CMP_STRIDE
    chunk = [jnp.concatenate([imp_s[k, pl.ds(c, n_sel, stride=per_block), :] for k in range(tq // LANES)], axis=1)
             for c in range(per_block)]
    imp_sel = chunk[0]
    for c in range(1, per_block):
        imp_sel = imp_sel + chunk[c]
    blk = lax.broadcasted_iota(jnp.int32, (n_sel, tq), 0)
    imp_sel = imp_sel + jnp.where(blk == 0, 0.0, pltpu.roll(chunk[per_block - 1], 1, 0))
    cur = jnp.right_shift(t_row[:, 0:tq], SEL_BLOCK.bit_length() - 1)
    forced = (blk == 0) | (blk == cur) | (blk == cur - 1)
    valid = blk <= cur
    blk_f = blk.astype(F32)

    def pick_round(_, carry):
        work, taken = carry
        top = jnp.max(work, axis=0, keepdims=True)
        first = jnp.min(jnp.where(work == top, blk_f, float(n_sel)), axis=0, keepdims=True)
        pick = blk_f == first
        return jnp.where(pick, REMOVED, work), jnp.where(pick, 1.0, taken)

    _, taken = lax.fori_loop(
        0, max(k_top - 3, 0), pick_round,
        (jnp.where(valid & jnp.logical_not(forced), imp_sel, NEG_INF), jnp.where(forced & valid, 1.0, 0.0)),
        unroll=True)
    chosen = taken > 0.5
    bias = jnp.where(chosen, 0.0, NEG_INF)
    bias_s[...] = jnp.concatenate([bias] * HEADS_PER_GROUP, axis=1)

    blocks_per_tile = KEY_TILE // SEL_BLOCK
    hit = jnp.where(chosen & (blk < blocks_per_tile * i), 1.0, 0.0)
    blk_hit = jnp.max(hit, axis=1, keepdims=True)
    weight = jnp.left_shift(1, lax.broadcasted_iota(jnp.int32, (n_sel, 1), 0) & (WORD_BITS - 1)).astype(F32)
    packed = blk_hit * weight
    for w in range(n_sel // WORD_BITS):
        word = jnp.sum(packed[w * WORD_BITS:(w + 1) * WORD_BITS], axis=0, keepdims=True).astype(jnp.int32)
        words_s[w] = word[0, 0]

    tiles_per_word = WORD_BITS // blocks_per_tile
    count_s[0] = 0
    for w in range(n_sel // WORD_BITS):
        @pl.when(words_s[w] != 0)
        def _append_tiles_of_word():
            word = words_s[w]
            count = count_s[0]
            for k in range(tiles_per_word):
                bits = jnp.right_shift(word, blocks_per_tile * k) & ((1 << blocks_per_tile) - 1)
                tiles_s[count] = w * tiles_per_word + k
                count = count + jnp.where(bits != 0, 1, 0)
            count_s[0] = count

    n_active = count_s[0]
    for c in range(SEL_CHAINS):
        tiles_s[n_active + c] = -1

    def tiles_update(kts, k_s, vt_s, masks, biases_per_tile, ms, accs):
        k0s = [pl.multiple_of(kt * KEY_TILE, KEY_TILE) for kt in kts]
        scores = []
        for k0, mask in zip(k0s, masks):
            s = _dot(k_s[pl.ds(k0, KEY_TILE), :], qst)
            scores.append(s if mask is None else s + cmat_s[mask])
        new_ms, alphas, probs = [], [], []
        for s, k0, biases, m in zip(scores, k0s, biases_per_tile, ms):
            rv = slope_row * (t_row - k0).astype(F32)
            rows = KEY_TILE // len(biases)
            parts = [s[b * rows:(b + 1) * rows] for b in range(len(biases))]
            m_new = m
            for part, bias_row in zip(parts, biases):
                m_new = jnp.maximum(m_new, jnp.max(part, axis=0, keepdims=True) + (bias_row - rv))
            seen = m_new > 0.5 * NEG_INF
            ps = []
            for part, bias_row in zip(parts, biases):
                shift = jnp.where(seen, m_new + (rv - bias_row), -NEG_INF)
                ps.append(jnp.exp2(part - shift))
            new_ms.append(m_new)
            alphas.append(jnp.exp2(m - m_new))
            probs.append((jnp.concatenate(ps, axis=0) if len(ps) > 1 else ps[0]).astype(BF16))
        new_accs = [alpha * acc + _dot(vt_s[:, pl.ds(k0, KEY_TILE)], p)
                    for alpha, acc, k0, p in zip(alphas, accs, k0s, probs)]
        return new_ms, new_accs

    def merge(ms, accs):
        m = functools.reduce(jnp.maximum, ms)
        acc = functools.reduce(lambda x, y: x + y, [jnp.exp2(mc - m) * ac for mc, ac in zip(ms, accs)])
        return acc[0:HEAD_DIM] * (1.0 / acc[HEAD_DIM:HEAD_DIM + 1])

    m_init = jnp.full((1, hq), NEG_INF, F32)
    acc_init = jnp.zeros((VALUE_ROWS, hq), F32)
    dead_row = jnp.full((1, hq), NEG_INF, F32)
    live_row = jnp.zeros((1, hq), F32)

    def block_biases(kt):
        return [bias_s[pl.ds(blocks_per_tile * kt + b, 1), :] for b in range(blocks_per_tile)]

    for c in range(SEL_CHAINS):
        acc_s[c] = acc_init

    def sel_step(j, ms):
        entries = [tiles_s[j * SEL_CHAINS + c] for c in range(SEL_CHAINS)]
        kts = [jnp.maximum(e, 0) for e in entries]
        biases = [[jnp.where(e < 0, dead_row, row) for row in block_biases(kt)] for e, kt in zip(entries, kts)]
        new_ms, new_accs = tiles_update(kts, ks_s, vst_s, [None] * SEL_CHAINS, biases, ms,
                                        [acc_s[c] for c in range(SEL_CHAINS)])
        for c in range(SEL_CHAINS):
            acc_s[c] = new_accs[c]
        return tuple(new_ms)

    ms = lax.fori_loop(0, (n_active + SEL_CHAINS - 1) // SEL_CHAINS, sel_step, (m_init,) * SEL_CHAINS)

    far = WINDOW // KEY_TILE
    kts = [i - d for d in range(far + 1)]
    win_masks = [0 if d == 0 else (1 if d == far else None) for d in range(far + 1)]
    win_biases = [[jnp.where(kt < 0, dead_row, live_row)] for kt in kts]
    n_win = far + 1
    ms_d, accs_d = tiles_update([i], ks_s, vst_s, [0], [block_biases(i)], [m_init], [acc_init])
    ms_w, accs_w = tiles_update([jnp.maximum(kt, 0) for kt in kts], kw_s, vwt_s, win_masks, win_biases,
                                [m_init] * n_win, [acc_init] * n_win)
    o_sel = merge(list(ms) + ms_d, [acc_s[c] for c in range(SEL_CHAINS)] + accs_d)
    o_win = merge(ms_w, accs_w)

    gl = glt_ref[...]

    def gate_row(branch):
        rows = [jax.nn.sigmoid(gl[branch * HEADS_PER_GROUP + h:branch * HEADS_PER_GROUP + h + 1, :])
                for h in range(HEADS_PER_GROUP)]
        return jnp.concatenate(rows, axis=1)

    ot = gate_row(0) * o_cmp + gate_row(1) * o_sel + gate_row(2) * o_win
    out_ref[...] = jnp.concatenate(
        [ot[:, h * tq:(h + 1) * tq] for h in range(HEADS_PER_GROUP)], axis=0).astype(out_ref.dtype)


def _nsa(qt, cp, kp, vt, glt, slopes, qg, kg, pe, w1, w2k, w2vt, *, batch, seq, tq):
    nq = seq // tq
    n_cmp = seq // CMP_STRIDE
    n_sel = seq // SEL_BLOCK
    hq = HEADS_PER_GROUP * tq
    const2 = lambda b, g, i: (0, 0)
    const3 = lambda b, g, i: (0, 0, 0)
    return pl.pallas_call(
        functools.partial(_nsa_kernel, seq=seq, tq=tq),
        grid=(batch, NSA_KV_GROUPS, nq),
        in_specs=[
            pl.BlockSpec((HEADS_PER_GROUP * HEAD_DIM, tq), lambda b, g, i: (g, b * nq + i)),
            pl.BlockSpec((seq, 2 * HEAD_DIM), lambda b, g, i: (b, g)),
            pl.BlockSpec((seq, 2 * HEAD_DIM), lambda b, g, i: (b, g)),
            pl.BlockSpec((2 * HEAD_DIM, seq), lambda b, g, i: (g, b)),
            pl.BlockSpec((GATE_ROWS, tq), lambda b, g, i: (g, b * nq + i)),
            pl.BlockSpec((1, 8, hq), lambda b, g, i: (g, 0, 0)),
            pl.BlockSpec(qg.shape, const2),
            pl.BlockSpec(kg.shape, const2),
            pl.BlockSpec(pe.shape, const2),
            pl.BlockSpec(w1.shape, const3),
            pl.BlockSpec(w2k.shape, const2),
            pl.BlockSpec(w2vt.shape, const2),
        ],
        out_specs=pl.BlockSpec((HEADS_PER_GROUP * HEAD_DIM, tq), lambda b, g, i: (g, b * nq + i)),
        out_shape=jax.ShapeDtypeStruct((NSA_WIDTH, batch * seq), BF16),
        scratch_shapes=[
            pltpu.VMEM((seq, 2 * HEAD_DIM), BF16),
            pltpu.VMEM((seq, 2 * HEAD_DIM), BF16),
            pltpu.VMEM((n_cmp, HEAD_DIM), BF16),
            pltpu.VMEM((HEAD_DIM, n_cmp), BF16),
            pltpu.VMEM((VALUE_ROWS, seq), BF16),
            pltpu.VMEM((VALUE_ROWS, seq), BF16),
            pltpu.VMEM((n_sel, hq), F32),
            pltpu.VMEM((2, KEY_TILE, hq), F32),
            pltpu.VMEM((SEL_CHAINS, VALUE_ROWS, hq), F32),
            pltpu.SMEM((n_sel // WORD_BITS,), jnp.int32),
            pltpu.SMEM((seq // KEY_TILE + SEL_CHAINS,), jnp.int32),
            pltpu.SMEM((1,), jnp.int32),
            pltpu.VMEM((n_cmp, hq), F32),
            pltpu.VMEM((CMP_TILE, hq), jnp.int32),
            pltpu.VMEM((tq // LANES, n_cmp, LANES), F32),
        ],
        compiler_params=pltpu.CompilerParams(
            dimension_semantics=("arbitrary", "arbitrary", "arbitrary"), vmem_limit_bytes=VMEM_LIMIT_BYTES),
        name="nsa",
    )(qt, cp, kp, vt, glt, slopes, qg, kg, pe, w1, w2k, w2vt)


def _final_kernel(x_ref, ys_ref, ynt_ref, g1_ref, wmg_ref, wps_ref, wpn_ref, wout_ref, g2_ref, wup_ref, wdn_ref,
                  o_ref):
    x = x_ref[...]
    h = _rms_rows(x, g1_ref[...]).astype(BF16)
    mg = _dot(h, wmg_ref[...])
    ps = _dot(ys_ref[...], wps_ref[...])
    pn = _dot_tn(ynt_ref[...], wpn_ref[...])
    merged = jax.nn.sigmoid(mg[:, :D_MODEL]) * ps + jax.nn.sigmoid(mg[:, D_MODEL:]) * pn
    x1 = x + _dot(merged.astype(BF16), wout_ref[...])
    h2 = _rms_rows(x1, g2_ref[...]).astype(BF16)
    a = jnp.maximum(_dot(h2, wup_ref[...]), 0.0)
    o_ref[...] = x1 + _dot((a * a).astype(BF16), wdn_ref[...])


def _final(x2, ys, ynt, g1, wmg, wps, wpn, wout, g2, wup, wdn, *, tm):
    n = x2.shape[0]
    const = lambda i: (0, 0)
    resident = lambda w: pl.BlockSpec(w.shape, const, pipeline_mode=pl.Buffered(1))
    return pl.pallas_call(
        _final_kernel,
        grid=(n // tm,),
        in_specs=[
            pl.BlockSpec((tm, D_MODEL), lambda i: (i, 0)),
            pl.BlockSpec((tm, SSM_WIDTH), lambda i: (i, 0)),
            pl.BlockSpec((NSA_WIDTH, tm), lambda i: (0, i)),
            resident(g1), resident(wmg), resident(wps), resident(wpn), resident(wout),
            resident(g2), resident(wup), resident(wdn),
        ],
        out_specs=pl.BlockSpec((tm, D_MODEL), lambda i: (i, 0)),
        out_shape=jax.ShapeDtypeStruct((n, D_MODEL), F32),
        compiler_params=pltpu.CompilerParams(
            dimension_semantics=("arbitrary",), vmem_limit_bytes=VMEM_LIMIT_BYTES),
        name="final",
    )(x2, ys, ynt, g1, wmg, wps, wpn, wout, g2, wup, wdn)


def _pack_in_proj(w_in):
    o1 = SSM_WIDTH
    o2 = o1 + NSA_WIDTH
    o3 = o2 + KV_WIDTH
    o4 = o3 + 3 * NSA_HEADS
    w_u, w_q, w_kv, w_gl = w_in[:, :o1], w_in[:, o1:o2], w_in[:, o2:o3], w_in[:, o3:o4]
    w_mg = w_in[:, o4:]
    kv = w_kv.reshape(D_MODEL, 3, 2, NSA_KV_GROUPS, HEAD_DIM)
    groups = range(NSA_KV_GROUPS)
    cpack = [jnp.concatenate([kv[:, 0, 0, g], kv[:, 0, 1, g]], axis=1) for g in groups]
    kpack = [jnp.concatenate([kv[:, 1, 0, g], kv[:, 2, 0, g]], axis=1) for g in groups]
    wa = jnp.concatenate([w_u] + cpack + kpack, axis=1)
    vrows = [jnp.concatenate([kv[:, 1, 1, g], kv[:, 2, 1, g]], axis=1) for g in range(NSA_KV_GROUPS)]
    gl = w_gl.reshape(D_MODEL, 3, NSA_KV_GROUPS, HEADS_PER_GROUP)
    pad = jnp.zeros((D_MODEL, GATE_ROWS - 3 * HEADS_PER_GROUP), w_in.dtype)
    glrows = [jnp.concatenate([gl[:, 0, g], gl[:, 1, g], gl[:, 2, g], pad], axis=1) for g in range(NSA_KV_GROUPS)]
    wbt = jnp.concatenate([w_q] + vrows + glrows, axis=1).T
    return wa.astype(BF16), wbt.astype(BF16), w_mg.astype(BF16)


def _pack_compress(pe_k, pe_v, w1k, w1v):
    pe = jnp.concatenate([pe_k, pe_v], axis=1)
    w1k = w1k.reshape(CMP_BLOCK, HEAD_DIM, CMP_HIDDEN)
    w1v = w1v.reshape(CMP_BLOCK, HEAD_DIM, CMP_HIDDEN)
    zero = jnp.zeros_like(w1k)
    w1 = jnp.concatenate([jnp.concatenate([w1k, zero], axis=2), jnp.concatenate([zero, w1v], axis=2)], axis=1)
    return pe, w1.astype(BF16)


def _pack_s5(a_re, a_im, log_dt, b_re, b_im, c_re, c_im, seg):
    dt = jnp.exp(log_dt)[:, None]
    mag = jnp.exp(a_re * dt)
    abar_re = mag * jnp.cos(a_im * dt)
    abar_im = mag * jnp.sin(a_im * dt)
    den = a_re * a_re + a_im * a_im
    nr = abar_re - 1.0
    fr = (nr * a_re + abar_im * a_im) / den
    fi = (abar_im * a_re - nr * a_im) / den
    bbar_re = fr[..., None] * b_re - fi[..., None] * b_im
    bbar_im = fr[..., None] * b_im + fi[..., None] * b_re
    gh = SSM_GROUPS // 2
    eye = jnp.eye(gh, dtype=F32)

    def in_blocks(m):
        return (m.transpose(0, 2, 1)[:, :, None, :] * eye[:, None, :, None]).reshape(gh * SSM_GROUP, gh * SSM_STATE)

    def out_blocks(m):
        return (m.transpose(0, 2, 1)[:, :, None, :] * eye[:, None, :, None]).reshape(gh * SSM_STATE, gh * SSM_GROUP)

    halves = [slice(0, gh), slice(gh, SSM_GROUPS)]
    bblk = jnp.stack([jnp.concatenate([in_blocks(bbar_re[h]), in_blocks(bbar_im[h])], axis=1) for h in halves])
    cblk = jnp.stack([jnp.concatenate([out_blocks(c_re[h]), -out_blocks(c_im[h])], axis=0) for h in halves])

    k = jnp.arange(1, seg + 1, dtype=F32)[:, None, None]
    pmag = jnp.exp(k * (a_re * dt))
    apow_re = (pmag * jnp.cos(k * (a_im * dt))).reshape(seg, SSM_STATES)
    apow_im = (pmag * jnp.sin(k * (a_im * dt))).reshape(seg, SSM_STATES)
    acoef = jnp.zeros((8, SSM_STATES), F32)
    acoef = acoef.at[0].set(abar_re.reshape(-1)).at[1].set(abar_im.reshape(-1))
    acoef = acoef.at[2].set(apow_re[seg - 1]).at[3].set(apow_im[seg - 1])
    return bblk.astype(BF16), cblk.astype(BF16), acoef, apow_re, apow_im


def kernel(x, norm1_g, w_in, ssm_a_re, ssm_a_im, ssm_log_dt, ssm_b_re, ssm_b_im, ssm_c_re, ssm_c_im, ssm_d, ssm_w_glu, ssm_b_glu, cmp_pe_k, cmp_pe_v, cmp_wk1, cmp_wk2, cmp_wv1, cmp_wv2, q_norm_g, k_norm_g, w_proj_ssm, w_proj_nsa, w_out, norm2_g, w_up, w_down):
    batch, seq, d_model = x.shape
    assert d_model == D_MODEL and seq % 512 == 0
    n = batch * seq
    tq = KEY_TILE
    depth = w_in.shape[0]
    head = jnp.arange(1, NSA_HEADS + 1, dtype=F32)
    slopes = jnp.exp2(-8.0 * head / NSA_HEADS).reshape(NSA_KV_GROUPS, 1, HEADS_PER_GROUP, 1)
    slopes = jnp.broadcast_to(slopes, (NSA_KV_GROUPS, 8, HEADS_PER_GROUP, tq)).reshape(NSA_KV_GROUPS, 8, -1)

    x2 = x.reshape(n, D_MODEL)
    for l in range(depth):
        wa, wbt, wmg = _pack_in_proj(w_in[l])
        g1 = norm1_g[l].reshape(1, D_MODEL)
        u, cp, kp, qt, vt, glt = _in_proj(x2, g1, wa, wbt, tm=512)

        tc = 256
        bblk, cblk, acoef, apow_re, apow_im = _pack_s5(ssm_a_re[l], ssm_a_im[l], ssm_log_dt[l], ssm_b_re[l],
                                                       ssm_b_im[l], ssm_c_re[l], ssm_c_im[l], tc // SCAN_SEGMENTS)
        ys = _s5(u, bblk, cblk, acoef, apow_re, apow_im, ssm_d[l].reshape(1, SSM_WIDTH),
                 ssm_w_glu[l].astype(BF16), ssm_b_glu[l].reshape(1, SSM_WIDTH), batch=batch, seq=seq, tc=tc)

        qg = jnp.broadcast_to(q_norm_g[l].reshape(HEAD_DIM, 1), (HEAD_DIM, tq))
        kg = jnp.zeros((8, HEAD_DIM), F32).at[0:3].set(k_norm_g[l])
        pe, w1 = _pack_compress(cmp_pe_k[l], cmp_pe_v[l], cmp_wk1[l], cmp_wv1[l])
        ynt = _nsa(qt, cp, kp, vt, glt, slopes, qg, kg, pe, w1,
                   cmp_wk2[l].astype(BF16), cmp_wv2[l].T.astype(BF16), batch=batch, seq=seq, tq=tq)

        x2 = _final(x2, ys, ynt, g1, wmg, w_proj_ssm[l].astype(BF16), w_proj_nsa[l].astype(BF16),
                    w_out[l].astype(BF16), norm2_g[l].reshape(1, D_MODEL), w_up[l].astype(BF16),
                    w_down[l].astype(BF16), tm=256)
    return x2.reshape(batch, seq, D_MODEL)
```

```python
import functools

import jax
import jax.numpy as jnp
from jax import lax
from jax.experimental import pallas as pl
from jax.experimental.pallas import tpu as pltpu

D_MODEL = 1024
SSM_WIDTH = D_MODEL // 2
SSM_GROUP = 16
SSM_GROUPS = SSM_WIDTH // SSM_GROUP
SSM_STATE = 64
SSM_STATES = SSM_GROUPS * SSM_STATE
NSA_HEADS = 8
NSA_KV_GROUPS = 2
HEADS_PER_GROUP = NSA_HEADS // NSA_KV_GROUPS
HEAD_DIM = 64
NSA_WIDTH = NSA_HEADS * HEAD_DIM
CMP_STRIDE = 16
CMP_BLOCK = 2 * CMP_STRIDE
CMP_HIDDEN = 2 * HEAD_DIM
SEL_BLOCK = 64
SEL_TOP_K = 16
WINDOW = 512
D_FF = 4 * D_MODEL
N_MIXERS = 2
KV_WIDTH = 3 * 2 * NSA_KV_GROUPS * HEAD_DIM
EPS = 1e-6
NEG_INF = -1e30
REMOVED = -3e38

LANES = 128
VMEM_LIMIT_BYTES = 56 * 1024 * 1024

GATE_ROWS = 16

F32 = jnp.float32
BF16 = jnp.bfloat16


def _dot(a, b):
    return jnp.dot(a, b, preferred_element_type=F32)


def _dot_nt(a, b):
    return lax.dot_general(a, b, (((1,), (1,)), ((), ())), preferred_element_type=F32)


def _dot_tn(a, b):
    return lax.dot_general(a, b, (((0,), (0,)), ((), ())), preferred_element_type=F32)


def _rms_rows(x, g):
    ms = jnp.mean(x * x, axis=-1, keepdims=True)
    return x * lax.rsqrt(ms + EPS) * g


def _inproj_kernel(x_ref, g_ref, wa_ref, wbt_ref, u_ref, cp_ref, kp_ref, qt_ref, vt_ref, glt_ref):
    h = _rms_rows(x_ref[...], g_ref[...]).astype(BF16)
    a = _dot(h, wa_ref[...])
    for k in range(SSM_WIDTH // LANES):
        u_ref[k] = a[:, k * LANES:(k + 1) * LANES]
    cp_ref[...] = a[:, SSM_WIDTH:SSM_WIDTH + 4 * HEAD_DIM]
    kp_ref[...] = a[:, SSM_WIDTH + 4 * HEAD_DIM:]
    bt = _dot_nt(wbt_ref[...], h)
    qt_ref[...] = bt[:NSA_WIDTH]
    vt_ref[...] = bt[NSA_WIDTH:NSA_WIDTH + 4 * HEAD_DIM]
    glt_ref[...] = bt[NSA_WIDTH + 4 * HEAD_DIM:]


def _in_proj(x2, g1, wa, wbt, *, tm):
    n = x2.shape[0]
    nb = wbt.shape[0]
    const = lambda i: (0, 0)
    return pl.pallas_call(
        _inproj_kernel,
        grid=(n // tm,),
        in_specs=[
            pl.BlockSpec((tm, D_MODEL), lambda i: (i, 0)),
            pl.BlockSpec((1, D_MODEL), const),
            pl.BlockSpec(wa.shape, const),
            pl.BlockSpec(wbt.shape, const),
        ],
        out_specs=[
            pl.BlockSpec((SSM_WIDTH // LANES, tm, LANES), lambda i: (0, i, 0)),
            pl.BlockSpec((tm, 4 * HEAD_DIM), lambda i: (i, 0)),
            pl.BlockSpec((tm, 4 * HEAD_DIM), lambda i: (i, 0)),
            pl.BlockSpec((NSA_WIDTH, tm), lambda i: (0, i)),
            pl.BlockSpec((4 * HEAD_DIM, tm), lambda i: (0, i)),
            pl.BlockSpec((NSA_KV_GROUPS * GATE_ROWS, tm), lambda i: (0, i)),
        ],
        out_shape=[
            jax.ShapeDtypeStruct((SSM_WIDTH // LANES, n, LANES), F32),
            jax.ShapeDtypeStruct((n, 4 * HEAD_DIM), F32),
            jax.ShapeDtypeStruct((n, 4 * HEAD_DIM), F32),
            jax.ShapeDtypeStruct((NSA_WIDTH, n), F32),
            jax.ShapeDtypeStruct((4 * HEAD_DIM, n), F32),
            jax.ShapeDtypeStruct((NSA_KV_GROUPS * GATE_ROWS, n), F32),
        ],
        compiler_params=pltpu.CompilerParams(
            dimension_semantics=("arbitrary",), vmem_limit_bytes=VMEM_LIMIT_BYTES),
        name="in_proj",
    )(x2, g1, wa, wbt)


SCAN_SEGMENTS = 8
SCAN_SLAB = 512
STATE_HALF = SSM_STATES // 2
U_SLABS = SSM_WIDTH // LANES


def _s5_kernel(u_ref, bblk_ref, cblk_ref, acoef_ref, apow_re_ref, apow_im_ref, d_ref, wglu_ref, bglu_ref, y_ref,
               up_s, bx_s, carry_s, cseg_s, yp_s, *, tc):
    seg = tc // SCAN_SEGMENTS

    @pl.when(pl.program_id(1) == 0)
    def _reset():
        carry_s[...] = jnp.zeros_like(carry_s)

    for tl in range(seg):
        for k in range(U_SLABS):
            up_s[tl * SCAN_SEGMENTS:(tl + 1) * SCAN_SEGMENTS, k * LANES:(k + 1) * LANES] = (
                u_ref[k, pl.ds(tl, SCAN_SEGMENTS, stride=seg), :])
    up = up_s[...]
    half_w = SSM_WIDTH // 2
    for h in range(2):
        bx_s[:, h * 2 * STATE_HALF:(h + 1) * 2 * STATE_HALF] = _dot(
            up[:, h * half_w:(h + 1) * half_w].astype(BF16), bblk_ref[h])

    for slab in range(SSM_STATES // SCAN_SLAB):
        base = (slab * SCAN_SLAB // STATE_HALF) * 2 * STATE_HALF + (slab * SCAN_SLAB) % STATE_HALF
        re_cols = pl.ds(base, SCAN_SLAB)
        im_cols = pl.ds(base + STATE_HALF, SCAN_SLAB)
        nat = pl.ds(slab * SCAN_SLAB, SCAN_SLAB)
        a_re = acoef_ref[0:1, nat]
        a_im = acoef_ref[1:2, nat]
        al_re = acoef_ref[2:3, nat]
        al_im = acoef_ref[3:4, nat]

        def scan_step(tl, carry):
            xr, xi = carry
            rows = pl.ds(pl.multiple_of(tl * SCAN_SEGMENTS, SCAN_SEGMENTS), SCAN_SEGMENTS)
            nr = a_re * xr - a_im * xi + bx_s[rows, re_cols]
            ni = a_re * xi + a_im * xr + bx_s[rows, im_cols]
            bx_s[rows, re_cols] = nr
            bx_s[rows, im_cols] = ni
            return nr, ni

        zero = jnp.zeros((SCAN_SEGMENTS, SCAN_SLAB), F32)
        end_r, end_i = lax.fori_loop(0, seg, scan_step, (zero, zero), unroll=True)

        cr = carry_s[0:1, re_cols]
        ci = carry_s[0:1, im_cols]
        for r in range(SCAN_SEGMENTS):
            cseg_s[r:r + 1, re_cols] = cr
            cseg_s[r:r + 1, im_cols] = ci
            nr = al_re * cr - al_im * ci + end_r[r:r + 1]
            ni = al_re * ci + al_im * cr + end_i[r:r + 1]
            cr, ci = nr, ni
        carry_s[0:1, re_cols] = cr
        carry_s[0:1, im_cols] = ci

        cr8 = cseg_s[:, re_cols]
        ci8 = cseg_s[:, im_cols]

        def fix_step(tl, _):
            rows = pl.ds(pl.multiple_of(tl * SCAN_SEGMENTS, SCAN_SEGMENTS), SCAN_SEGMENTS)
            pr = apow_re_ref[pl.ds(tl, 1), nat]
            pi = apow_im_ref[pl.ds(tl, 1), nat]
            bx_s[rows, re_cols] = bx_s[rows, re_cols] + (pr * cr8 - pi * ci8)
            bx_s[rows, im_cols] = bx_s[rows, im_cols] + (pr * ci8 + pi * cr8)
            return 0

        lax.fori_loop(0, seg, fix_step, 0, unroll=True)

    y = jnp.concatenate(
        [_dot(bx_s[:, h * 2 * STATE_HALF:(h + 1) * 2 * STATE_HALF].astype(BF16), cblk_ref[h]) for h in range(2)],
        axis=1) + d_ref[...] * up
    yg = jax.nn.gelu(y)
    z = _dot(yg.astype(BF16), wglu_ref[...]) + bglu_ref[...]
    out = yg * jax.nn.sigmoid(z)
    for k in range(U_SLABS):
        yp_s[k] = out[:, k * LANES:(k + 1) * LANES]
    for r in range(SCAN_SEGMENTS):
        for k in range(U_SLABS):
            y_ref[r * seg:(r + 1) * seg, k * LANES:(k + 1) * LANES] = (
                yp_s[k, pl.ds(r, seg, stride=SCAN_SEGMENTS), :].astype(y_ref.dtype))


def _s5(u4, bblk, cblk, acoef, apow_re, apow_im, d_row, wglu, bglu, *, batch, seq, tc):
    nchunk = seq // tc
    const2 = lambda b, c: (0, 0)
    const3 = lambda b, c: (0, 0, 0)
    return pl.pallas_call(
        functools.partial(_s5_kernel, tc=tc),
        grid=(batch, nchunk),
        in_specs=[
            pl.BlockSpec((U_SLABS, tc, LANES), lambda b, c: (0, b * nchunk + c, 0)),
            pl.BlockSpec(bblk.shape, const3),
            pl.BlockSpec(cblk.shape, const3),
            pl.BlockSpec(acoef.shape, const2),
            pl.BlockSpec(apow_re.shape, const2),
            pl.BlockSpec(apow_im.shape, const2),
            pl.BlockSpec(d_row.shape, const2),
            pl.BlockSpec(wglu.shape, const2),
            pl.BlockSpec(bglu.shape, const2),
        ],
        out_specs=pl.BlockSpec((tc, SSM_WIDTH), lambda b, c: (b * nchunk + c, 0)),
        out_shape=jax.ShapeDtypeStruct((batch * seq, SSM_WIDTH), BF16),
        scratch_shapes=[
            pltpu.VMEM((tc, SSM_WIDTH), F32),
            pltpu.VMEM((tc, 2 * SSM_STATES), F32),
            pltpu.VMEM((8, 2 * SSM_STATES), F32),
            pltpu.VMEM((SCAN_SEGMENTS, 2 * SSM_STATES), F32),
            pltpu.VMEM((U_SLABS, tc, LANES), F32),
        ],
        compiler_params=pltpu.CompilerParams(
            dimension_semantics=("arbitrary", "arbitrary"), vmem_limit_bytes=VMEM_LIMIT_BYTES),
        name="s5",
    )(u4, bblk, cblk, acoef, apow_re, apow_im, d_row, wglu, bglu)


KEY_TILE = 256
CMP_TILE = 256
LOG2E = 1.4426950408889634
SLOPE_PARTS = 3
VALUE_ROWS = HEAD_DIM + 16
SEL_CHAINS = 3
WORD_BITS = 16


def _nsa_kernel(qt_ref, cp_ref, kp_ref, vt_ref, glt_ref, slope_ref, qg_ref, kg_ref, pe_ref,
                w1_ref, w2k_ref, w2vt_ref, out_ref,
                ks_s, kw_s, kc_s, vct_s, vst_s, vwt_s, bias_s, cmat_s, acc_s, words_s, tiles_s, count_s, amat_s, cvis_s, imp_s, *, seq, tq):
    assert tq == KEY_TILE
    i = pl.program_id(2)
    n_cmp = seq // CMP_STRIDE
    n_sel = seq // SEL_BLOCK
    k_top = min(SEL_TOP_K, n_sel)
    hq = HEADS_PER_GROUP * tq

    @pl.when(i == 0)
    def _prepare_keys_and_values():
        kg = kg_ref[...]
        offs = (lax.broadcasted_iota(jnp.int32, (seq, HEAD_DIM), 0) & (KEY_TILE - 1)).astype(F32)
        offs = jnp.where(lax.broadcasted_iota(jnp.int32, (seq, HEAD_DIM), 1) < SLOPE_PARTS, offs, 0.0)
        ks_s[...] = jnp.concatenate([_rms_rows(kp_ref[:, 0:HEAD_DIM], kg[1:2]), offs], axis=1).astype(BF16)
        kw_s[...] = jnp.concatenate([_rms_rows(kp_ref[:, HEAD_DIM:2 * HEAD_DIM], kg[2:3]), offs], axis=1).astype(BF16)
        ones = jnp.ones((VALUE_ROWS - HEAD_DIM, seq), F32)
        vst_s[...] = jnp.concatenate([vt_ref[0:HEAD_DIM, :], ones], axis=0).astype(BF16)
        vwt_s[...] = jnp.concatenate([vt_ref[HEAD_DIM:2 * HEAD_DIM, :], ones], axis=0).astype(BF16)

        lo = jnp.zeros((n_cmp, 2 * CMP_HIDDEN), F32)
        hi = jnp.zeros((n_cmp, 2 * CMP_HIDDEN), F32)
        for tt in range(CMP_STRIDE):
            tok = cp_ref[pl.ds(tt, n_cmp, stride=CMP_STRIDE), :]
            lo = lo + _dot((tok + pe_ref[tt:tt + 1, :]).astype(BF16), w1_ref[tt])
            hi = hi + _dot((tok + pe_ref[CMP_STRIDE + tt:CMP_STRIDE + tt + 1, :]).astype(BF16),
                           w1_ref[CMP_STRIDE + tt])
        hidden = jax.nn.gelu(lo + pltpu.roll(hi, n_cmp - 1, 0)).astype(BF16)
        kc_s[...] = _rms_rows(_dot(hidden[:, :CMP_HIDDEN], w2k_ref[...]), kg[0:1]).astype(BF16)
        vct_s[...] = _dot_nt(w2vt_ref[...], hidden[:, CMP_HIDDEN:]).astype(BF16)

        r = lax.broadcasted_iota(jnp.int32, (KEY_TILE, hq), 0)
        a = lax.broadcasted_iota(jnp.int32, (KEY_TILE, hq), 1) & (tq - 1)
        cmat_s[0] = jnp.where(r <= a, 0.0, NEG_INF)
        cmat_s[1] = jnp.where(r > a, 0.0, NEG_INF)

        jrow = lax.broadcasted_iota(jnp.int32, (n_cmp, hq), 0)
        amat_s[...] = (slope_ref[0, 0:1, :] * LOG2E) * (jrow * CMP_STRIDE + (CMP_BLOCK - 1)).astype(F32)
        cvis_s[...] = (r * CMP_STRIDE + (CMP_BLOCK - 1)) - a
        imp_s[...] = jnp.zeros_like(imp_s)

    qs = i * tq
    lane = lax.broadcasted_iota(jnp.int32, (1, hq), 1)
    t_row = qs + (lane & (tq - 1))
    slope_row = slope_ref[0, 0:1, :] * LOG2E

    qg = qg_ref[...]
    heads = []
    for h in range(HEADS_PER_GROUP):
        qh = qt_ref[h * HEAD_DIM:(h + 1) * HEAD_DIM, :]
        ms = jnp.mean(qh * qh, axis=0, keepdims=True)
        heads.append((qh * lax.rsqrt(ms + EPS) * qg * (HEAD_DIM ** -0.5 * LOG2E)).astype(BF16))
    pad = lax.broadcasted_iota(jnp.int32, (HEAD_DIM, hq), 0)
    slope_rows = jnp.zeros((HEAD_DIM, hq), F32)
    rest = slope_row
    for part in range(SLOPE_PARTS):
        piece = rest.astype(BF16).astype(F32)
        slope_rows = jnp.where(pad == part, piece, slope_rows)
        rest = rest - piece
    qst = jnp.concatenate([jnp.concatenate(heads, axis=1), slope_rows.astype(BF16)], axis=0)

    def cmp_branch(n_tiles):
        def run():
            scores = []
            for jt in range(n_tiles):
                rows = pl.ds(jt * CMP_TILE, CMP_TILE)
                s = _dot(kc_s[rows, :], qst[0:HEAD_DIM]) + amat_s[rows, :]
                visible = cvis_s[...] <= qs - CMP_STRIDE * CMP_TILE * jt
                scores.append(jnp.where(visible, s, NEG_INF))
            m = functools.reduce(jnp.maximum, [jnp.max(s, axis=0, keepdims=True) for s in scores])
            shift = jnp.where(m > 0.5 * NEG_INF, m, -NEG_INF)
            exps = [jnp.exp2(s - shift) for s in scores]
            l = functools.reduce(lambda x, y: x + y, [jnp.sum(e, axis=0, keepdims=True) for e in exps])
            r = jnp.where(l > 0.0, 1.0 / l, 0.0)
            acc = jnp.zeros((HEAD_DIM, hq), F32)
            for jt, e in enumerate(exps):
                p = e * r
                imp = p[:, 0:tq]
                for h in range(1, HEADS_PER_GROUP):
                    imp = imp + p[:, h * tq:(h + 1) * tq]
                for k in range(tq // LANES):
                    imp_s[k, pl.ds(jt * CMP_TILE, CMP_TILE), :] = imp[:, k * LANES:(k + 1) * LANES]
                acc = acc + _dot(vct_s[:, pl.ds(jt * CMP_TILE, CMP_TILE)], p.astype(BF16))
            return acc
        return run

    n_cmp_live = (qs + tq - CMP_STRIDE) // CMP_STRIDE
    cmp_tiles = n_cmp // CMP_TILE
    if cmp_tiles >= 2:
        o_cmp = lax.cond(n_cmp_live <= (cmp_tiles // 2) * CMP_TILE, cmp_branch(cmp_tiles // 2), cmp_branch(cmp_tiles))
    else:
        o_cmp = cmp_branch(cmp_tiles)()

    per_block = SEL_BLOCK // CMP_STRIDE
    chunk = [jnp.concatenate([imp_s[k, pl.ds(c, n_sel, stride=per_block), :] for k in range(tq // LANES)], axis=1)
             for c in range(per_block)]
    imp_sel = chunk[0]
    for c in range(1, per_block):
        imp_sel = imp_sel + chunk[c]
    blk = lax.broadcasted_iota(jnp.int32, (n_sel, tq), 0)
    imp_sel = imp_sel + jnp.where(blk == 0, 0.0, pltpu.roll(chunk[per_block - 1], 1, 0))
    cur = jnp.right_shift(t_row[:, 0:tq], SEL_BLOCK.bit_length() - 1)
    forced = (blk == 0) | (blk == cur) | (blk == cur - 1)
    valid = blk <= cur
    blk_f = blk.astype(F32)

    def pick_round(_, carry):
        work, taken = carry
        top = jnp.max(work, axis=0, keepdims=True)
        first = jnp.min(jnp.where(work == top, blk_f, float(n_sel)), axis=0, keepdims=True)
        pick = blk_f == first
        return jnp.where(pick, REMOVED, work), jnp.where(pick, 1.0, taken)

    _, taken = lax.fori_loop(
        0, max(k_top - 3, 0), pick_round,
        (jnp.where(valid & jnp.logical_not(forced), imp_sel, NEG_INF), jnp.where(forced & valid, 1.0, 0.0)),
        unroll=True)
    chosen = taken > 0.5
    bias = jnp.where(chosen, 0.0, NEG_INF)
    bias_s[...] = jnp.concatenate([bias] * HEADS_PER_GROUP, axis=1)

    blocks_per_tile = KEY_TILE // SEL_BLOCK
    hit = jnp.where(chosen & (blk < blocks_per_tile * i), 1.0, 0.0)
    blk_hit = jnp.max(hit, axis=1, keepdims=True)
    weight = jnp.left_shift(1, lax.broadcasted_iota(jnp.int32, (n_sel, 1), 0) & (WORD_BITS - 1)).astype(F32)
    packed = blk_hit * weight
    for w in range(n_sel // WORD_BITS):
        word = jnp.sum(packed[w * WORD_BITS:(w + 1) * WORD_BITS], axis=0, keepdims=True).astype(jnp.int32)
        words_s[w] = word[0, 0]

    tiles_per_word = WORD_BITS // blocks_per_tile
    count_s[0] = 0
    for w in range(n_sel // WORD_BITS):
        @pl.when(words_s[w] != 0)
        def _append_tiles_of_word():
            word = words_s[w]
            count = count_s[0]
            for k in range(tiles_per_word):
                bits = jnp.right_shift(word, blocks_per_tile * k) & ((1 << blocks_per_tile) - 1)
                tiles_s[count] = w * tiles_per_word + k
                count = count + jnp.where(bits != 0, 1, 0)
            count_s[0] = count

    n_active = count_s[0]
    for c in range(SEL_CHAINS):
        tiles_s[n_active + c] = -1

    def tile_scores(kts, k_s, masks):
        k0s = [pl.multiple_of(kt * KEY_TILE, KEY_TILE) for kt in kts]
        scores = []
        for k0, mask in zip(k0s, masks):
            s = _dot(k_s[pl.ds(k0, KEY_TILE), :], qst)
            scores.append(s if mask is None else s + cmat_s[mask])
        return scores

    def tiles_finish(kts, scores, vt_s, biases_per_tile, ms, accs):
        k0s = [pl.multiple_of(kt * KEY_TILE, KEY_TILE) for kt in kts]
        new_ms, alphas, probs = [], [], []
        for s, k0, biases, m in zip(scores, k0s, biases_per_tile, ms):
            rv = slope_row * (t_row - k0).astype(F32)
            rows = KEY_TILE // len(biases)
            parts = [s[b * rows:(b + 1) * rows] for b in range(len(biases))]
            m_new = m
            for part, bias_row in zip(parts, biases):
                m_new = jnp.maximum(m_new, jnp.max(part, axis=0, keepdims=True) + (bias_row - rv))
            seen = m_new > 0.5 * NEG_INF
            ps = []
            for part, bias_row in zip(parts, biases):
                shift = jnp.where(seen, m_new + (rv - bias_row), -NEG_INF)
                ps.append(jnp.exp2(part - shift))
            new_ms.append(m_new)
            alphas.append(jnp.exp2(m - m_new))
            probs.append((jnp.concatenate(ps, axis=0) if len(ps) > 1 else ps[0]).astype(BF16))
        new_accs = [alpha * acc + _dot(vt_s[:, pl.ds(k0, KEY_TILE)], p)
                    for alpha, acc, k0, p in zip(alphas, accs, k0s, probs)]
        return new_ms, new_accs

    def tiles_update(kts, k_s, vt_s, masks, biases_per_tile, ms, accs):
        return tiles_finish(kts, tile_scores(kts, k_s, masks), vt_s, biases_per_tile, ms, accs)

    def merge(ms, accs):
        m = functools.reduce(jnp.maximum, ms)
        acc = functools.reduce(lambda x, y: x + y, [jnp.exp2(mc - m) * ac for mc, ac in zip(ms, accs)])
        return acc[0:HEAD_DIM] * (1.0 / acc[HEAD_DIM:HEAD_DIM + 1])

    m_init = jnp.full((1, hq), NEG_INF, F32)
    acc_init = jnp.zeros((VALUE_ROWS, hq), F32)
    dead_row = jnp.full((1, hq), NEG_INF, F32)
    live_row = jnp.zeros((1, hq), F32)

    def block_biases(kt):
        return [bias_s[pl.ds(blocks_per_tile * kt + b, 1), :] for b in range(blocks_per_tile)]

    for c in range(SEL_CHAINS):
        acc_s[c] = acc_init

    def sel_step(j, ms):
        entries = [tiles_s[j * SEL_CHAINS + c] for c in range(SEL_CHAINS)]
        kts = [jnp.maximum(e, 0) for e in entries]
        biases = [[jnp.where(e < 0, dead_row, row) for row in block_biases(kt)] for e, kt in zip(entries, kts)]
        new_ms, new_accs = tiles_update(kts, ks_s, vst_s, [None] * SEL_CHAINS, biases, ms,
                                        [acc_s[c] for c in range(SEL_CHAINS)])
        for c in range(SEL_CHAINS):
            acc_s[c] = new_accs[c]
        return tuple(new_ms)

    ms = lax.fori_loop(0, (n_active + SEL_CHAINS - 1) // SEL_CHAINS, sel_step, (m_init,) * SEL_CHAINS)

    far = WINDOW // KEY_TILE
    kts = [i - d for d in range(far + 1)]
    win_masks = [0 if d == 0 else (1 if d == far else None) for d in range(far + 1)]
    win_biases = [[jnp.where(kt < 0, dead_row, live_row)] for kt in kts]
    n_win = far + 1
    ms_d, accs_d = tiles_update([i], ks_s, vst_s, [0], [block_biases(i)], [m_init], [acc_init])
    ms_w, accs_w = tiles_update([jnp.maximum(kt, 0) for kt in kts], kw_s, vwt_s, win_masks, win_biases,
                                [m_init] * n_win, [acc_init] * n_win)
    o_sel = merge(list(ms) + ms_d, [acc_s[c] for c in range(SEL_CHAINS)] + accs_d)
    o_win = merge(ms_w, accs_w)

    gl = glt_ref[...]

    def gate_row(branch):
        rows = [jax.nn.sigmoid(gl[branch * HEADS_PER_GROUP + h:branch * HEADS_PER_GROUP + h + 1, :])
                for h in range(HEADS_PER_GROUP)]
        return jnp.concatenate(rows, axis=1)

    ot = gate_row(0) * o_cmp + gate_row(1) * o_sel + gate_row(2) * o_win
    out_ref[...] = jnp.concatenate(
        [ot[:, h * tq:(h + 1) * tq] for h in range(HEADS_PER_GROUP)], axis=0).astype(out_ref.dtype)


def _nsa(qt, cp, kp, vt, glt, slopes, qg, kg, pe, w1, w2k, w2vt, *, batch, seq, tq):
    nq = seq // tq
    n_cmp = seq // CMP_STRIDE
    n_sel = seq // SEL_BLOCK
    hq = HEADS_PER_GROUP * tq
    const2 = lambda b, g, i: (0, 0)
    const3 = lambda b, g, i: (0, 0, 0)
    return pl.pallas_call(
        functools.partial(_nsa_kernel, seq=seq, tq=tq),
        grid=(batch, NSA_KV_GROUPS, nq),
        in_specs=[
            pl.BlockSpec((HEADS_PER_GROUP * HEAD_DIM, tq), lambda b, g, i: (g, b * nq + i)),
            pl.BlockSpec((seq, 2 * HEAD_DIM), lambda b, g, i: (b, g)),
            pl.BlockSpec((seq, 2 * HEAD_DIM), lambda b, g, i: (b, g)),
            pl.BlockSpec((2 * HEAD_DIM, seq), lambda b, g, i: (g, b)),
            pl.BlockSpec((GATE_ROWS, tq), lambda b, g, i: (g, b * nq + i)),
            pl.BlockSpec((1, 8, hq), lambda b, g, i: (g, 0, 0)),
            pl.BlockSpec(qg.shape, const2),
            pl.BlockSpec(kg.shape, const2),
            pl.BlockSpec(pe.shape, const2),
            pl.BlockSpec(w1.shape, const3),
            pl.BlockSpec(w2k.shape, const2),
            pl.BlockSpec(w2vt.shape, const2),
        ],
        out_specs=pl.BlockSpec((HEADS_PER_GROUP * HEAD_DIM, tq), lambda b, g, i: (g, b * nq + i)),
        out_shape=jax.ShapeDtypeStruct((NSA_WIDTH, batch * seq), BF16),
        scratch_shapes=[
            pltpu.VMEM((seq, 2 * HEAD_DIM), BF16),
            pltpu.VMEM((seq, 2 * HEAD_DIM), BF16),
            pltpu.VMEM((n_cmp, HEAD_DIM), BF16),
            pltpu.VMEM((HEAD_DIM, n_cmp), BF16),
            pltpu.VMEM((VALUE_ROWS, seq), BF16),
            pltpu.VMEM((VALUE_ROWS, seq), BF16),
            pltpu.VMEM((n_sel, hq), F32),
            pltpu.VMEM((2, KEY_TILE, hq), F32),
            pltpu.VMEM((SEL_CHAINS, VALUE_ROWS, hq), F32),
            pltpu.SMEM((n_sel // WORD_BITS,), jnp.int32),
            pltpu.SMEM((seq // KEY_TILE + SEL_CHAINS,), jnp.int32),
            pltpu.SMEM((1,), jnp.int32),
            pltpu.VMEM((n_cmp, hq), F32),
            pltpu.VMEM((CMP_TILE, hq), jnp.int32),
            pltpu.VMEM((tq // LANES, n_cmp, LANES), F32),
        ],
        compiler_params=pltpu.CompilerParams(
            dimension_semantics=("arbitrary", "arbitrary", "arbitrary"), vmem_limit_bytes=VMEM_LIMIT_BYTES),
        name="nsa",
    )(qt, cp, kp, vt, glt, slopes, qg, kg, pe, w1, w2k, w2vt)


def _final_kernel(x_ref, ys_ref, ynt_ref, g1_ref, wmg_ref, wps_ref, wpn_ref, wout_ref, g2_ref, wup_ref, wdn_ref,
                  o_ref):
    x = x_ref[...]
    h = _rms_rows(x, g1_ref[...]).astype(BF16)
    mg = _dot(h, wmg_ref[...])
    ps = _dot(ys_ref[...], wps_ref[...])
    pn = _dot_tn(ynt_ref[...], wpn_ref[...])
    merged = jax.nn.sigmoid(mg[:, :D_MODEL]) * ps + jax.nn.sigmoid(mg[:, D_MODEL:]) * pn
    x1 = x + _dot(merged.astype(BF16), wout_ref[...])
    h2 = _rms_rows(x1, g2_ref[...]).astype(BF16)
    a = jnp.maximum(_dot(h2, wup_ref[...]), 0.0)
    o_ref[...] = x1 + _dot((a * a).astype(BF16), wdn_ref[...])


def _final(x2, ys, ynt, g1, wmg, wps, wpn, wout, g2, wup, wdn, *, tm):
    n = x2.shape[0]
    const = lambda i: (0, 0)
    resident = lambda w: pl.BlockSpec(w.shape, const, pipeline_mode=pl.Buffered(1))
    return pl.pallas_call(
        _final_kernel,
        grid=(n // tm,),
        in_specs=[
            pl.BlockSpec((tm, D_MODEL), lambda i: (i, 0)),
            pl.BlockSpec((tm, SSM_WIDTH), lambda i: (i, 0)),
            pl.BlockSpec((NSA_WIDTH, tm), lambda i: (0, i)),
            resident(g1), resident(wmg), resident(wps), resident(wpn), resident(wout),
            resident(g2), resident(wup), resident(wdn),
        ],
        out_specs=pl.BlockSpec((tm, D_MODEL), lambda i: (i, 0)),
        out_shape=jax.ShapeDtypeStruct((n, D_MODEL), F32),
        compiler_params=pltpu.CompilerParams(
            dimension_semantics=("arbitrary",), vmem_limit_bytes=VMEM_LIMIT_BYTES),
        name="final",
    )(x2, ys, ynt, g1, wmg, wps, wpn, wout, g2, wup, wdn)


def _pack_in_proj(w_in):
    o1 = SSM_WIDTH
    o2 = o1 + NSA_WIDTH
    o3 = o2 + KV_WIDTH
    o4 = o3 + 3 * NSA_HEADS
    w_u, w_q, w_kv, w_gl = w_in[:, :o1], w_in[:, o1:o2], w_in[:, o2:o3], w_in[:, o3:o4]
    w_mg = w_in[:, o4:]
    kv = w_kv.reshape(D_MODEL, 3, 2, NSA_KV_GROUPS, HEAD_DIM)
    groups = range(NSA_KV_GROUPS)
    cpack = [jnp.concatenate([kv[:, 0, 0, g], kv[:, 0, 1, g]], axis=1) for g in groups]
    kpack = [jnp.concatenate([kv[:, 1, 0, g], kv[:, 2, 0, g]], axis=1) for g in groups]
    wa = jnp.concatenate([w_u] + cpack + kpack, axis=1)
    vrows = [jnp.concatenate([kv[:, 1, 1, g], kv[:, 2, 1, g]], axis=1) for g in range(NSA_KV_GROUPS)]
    gl = w_gl.reshape(D_MODEL, 3, NSA_KV_GROUPS, HEADS_PER_GROUP)
    pad = jnp.zeros((D_MODEL, GATE_ROWS - 3 * HEADS_PER_GROUP), w_in.dtype)
    glrows = [jnp.concatenate([gl[:, 0, g], gl[:, 1, g], gl[:, 2, g], pad], axis=1) for g in range(NSA_KV_GROUPS)]
    wbt = jnp.concatenate([w_q] + vrows + glrows, axis=1).T
    return wa.astype(BF16), wbt.astype(BF16), w_mg.astype(BF16)


def _pack_compress(pe_k, pe_v, w1k, w1v):
    pe = jnp.concatenate([pe_k, pe_v], axis=1)
    w1k = w1k.reshape(CMP_BLOCK, HEAD_DIM, CMP_HIDDEN)
    w1v = w1v.reshape(CMP_BLOCK, HEAD_DIM, CMP_HIDDEN)
    zero = jnp.zeros_like(w1k)
    w1 = jnp.concatenate([jnp.concatenate([w1k, zero], axis=2), jnp.concatenate([zero, w1v], axis=2)], axis=1)
    return pe, w1.astype(BF16)


def _pack_s5(a_re, a_im, log_dt, b_re, b_im, c_re, c_im, seg):
    dt = jnp.exp(log_dt)[:, None]
    mag = jnp.exp(a_re * dt)
    abar_re = mag * jnp.cos(a_im * dt)
    abar_im = mag * jnp.sin(a_im * dt)
    den = a_re * a_re + a_im * a_im
    nr = abar_re - 1.0
    fr = (nr * a_re + abar_im * a_im) / den
    fi = (abar_im * a_re - nr * a_im) / den
    bbar_re = fr[..., None] * b_re - fi[..., None] * b_im
    bbar_im = fr[..., None] * b_im + fi[..., None] * b_re
    gh = SSM_GROUPS // 2
    eye = jnp.eye(gh, dtype=F32)

    def in_blocks(m):
        return (m.transpose(0, 2, 1)[:, :, None, :] * eye[:, None, :, None]).reshape(gh * SSM_GROUP, gh * SSM_STATE)

    def out_blocks(m):
        return (m.transpose(0, 2, 1)[:, :, None, :] * eye[:, None, :, None]).reshape(gh * SSM_STATE, gh * SSM_GROUP)

    halves = [slice(0, gh), slice(gh, SSM_GROUPS)]
    bblk = jnp.stack([jnp.concatenate([in_blocks(bbar_re[h]), in_blocks(bbar_im[h])], axis=1) for h in halves])
    cblk = jnp.stack([jnp.concatenate([out_blocks(c_re[h]), -out_blocks(c_im[h])], axis=0) for h in halves])

    k = jnp.arange(1, seg + 1, dtype=F32)[:, None, None]
    pmag = jnp.exp(k * (a_re * dt))
    apow_re = (pmag * jnp.cos(k * (a_im * dt))).reshape(seg, SSM_STATES)
    apow_im = (pmag * jnp.sin(k * (a_im * dt))).reshape(seg, SSM_STATES)
    acoef = jnp.zeros((8, SSM_STATES), F32)
    acoef = acoef.at[0].set(abar_re.reshape(-1)).at[1].set(abar_im.reshape(-1))
    acoef = acoef.at[2].set(apow_re[seg - 1]).at[3].set(apow_im[seg - 1])
    return bblk.astype(BF16), cblk.astype(BF16), acoef, apow_re, apow_im


def kernel(x, norm1_g, w_in, ssm_a_re, ssm_a_im, ssm_log_dt, ssm_b_re, ssm_b_im, ssm_c_re, ssm_c_im, ssm_d, ssm_w_glu, ssm_b_glu, cmp_pe_k, cmp_pe_v, cmp_wk1, cmp_wk2, cmp_wv1, cmp_wv2, q_norm_g, k_norm_g, w_proj_ssm, w_proj_nsa, w_out, norm2_g, w_up, w_down):
    batch, seq, d_model = x.shape
    assert d_model == D_MODEL and seq % 512 == 0
    n = batch * seq
    tq = KEY_TILE
    depth = w_in.shape[0]
    head = jnp.arange(1, NSA_HEADS + 1, dtype=F32)
    slopes = jnp.exp2(-8.0 * head / NSA_HEADS).reshape(NSA_KV_GROUPS, 1, HEADS_PER_GROUP, 1)
    slopes = jnp.broadcast_to(slopes, (NSA_KV_GROUPS, 8, HEADS_PER_GROUP, tq)).reshape(NSA_KV_GROUPS, 8, -1)

    x2 = x.reshape(n, D_MODEL)
    for l in range(depth):
        wa, wbt, wmg = _pack_in_proj(w_in[l])
        g1 = norm1_g[l].reshape(1, D_MODEL)
        u, cp, kp, qt, vt, glt = _in_proj(x2, g1, wa, wbt, tm=512)

        tc = 256
        bblk, cblk, acoef, apow_re, apow_im = _pack_s5(ssm_a_re[l], ssm_a_im[l], ssm_log_dt[l], ssm_b_re[l],
                                                       ssm_b_im[l], ssm_c_re[l], ssm_c_im[l], tc // SCAN_SEGMENTS)
        ys = _s5(u, bblk, cblk, acoef, apow_re, apow_im, ssm_d[l].reshape(1, SSM_WIDTH),
                 ssm_w_glu[l].astype(BF16), ssm_b_glu[l].reshape(1, SSM_WIDTH), batch=batch, seq=seq, tc=tc)

        qg = jnp.broadcast_to(q_norm_g[l].reshape(HEAD_DIM, 1), (HEAD_DIM, tq))
        kg = jnp.zeros((8, HEAD_DIM), F32).at[0:3].set(k_norm_g[l])
        pe, w1 = _pack_compress(cmp_pe_k[l], cmp_pe_v[l], cmp_wk1[l], cmp_wv1[l])
        ynt = _nsa(qt, cp, kp, vt, glt, slopes, qg, kg, pe, w1,
                   cmp_wk2[l].astype(BF16), cmp_wv2[l].T.astype(BF16), batch=batch, seq=seq, tq=tq)

        x2 = _final(x2, ys, ynt, g1, wmg, w_proj_ssm[l].astype(BF16), w_proj_nsa[l].astype(BF16),
                    w_out[l].astype(BF16), norm2_g[l].reshape(1, D_MODEL), w_up[l].astype(BF16),
                    w_down[l].astype(BF16), tm=256)
    return x2.reshape(batch, seq, D_MODEL)
```

```python
import functools

import jax
import jax.numpy as jnp
from jax import lax
from jax.experimental import pallas as pl
from jax.experimental.pallas import tpu as pltpu

D_MODEL = 1024
SSM_WIDTH = D_MODEL // 2
SSM_GROUP = 16
SSM_GROUPS = SSM_WIDTH // SSM_GROUP
SSM_STATE = 64
SSM_STATES = SSM_GROUPS * SSM_STATE
NSA_HEADS = 8
NSA_KV_GROUPS = 2
HEADS_PER_GROUP = NSA_HEADS // NSA_KV_GROUPS
HEAD_DIM = 64
NSA_WIDTH = NSA_HEADS * HEAD_DIM
CMP_STRIDE = 16
CMP_BLOCK = 2 * CMP_STRIDE
CMP_HIDDEN = 2 * HEAD_DIM
SEL_BLOCK = 64
SEL_TOP_K = 16
WINDOW = 512
D_FF = 4 * D_MODEL
N_MIXERS = 2
KV_WIDTH = 3 * 2 * NSA_KV_GROUPS * HEAD_DIM
EPS = 1e-6
NEG_INF = -1e30
REMOVED = -3e38

LANES = 128
VMEM_LIMIT_BYTES = 56 * 1024 * 1024

GATE_ROWS = 16

F32 = jnp.float32
BF16 = jnp.bfloat16


def _dot(a, b):
    return jnp.dot(a, b, preferred_element_type=F32)


def _dot_nt(a, b):
    return lax.dot_general(a, b, (((1,), (1,)), ((), ())), preferred_element_type=F32)


def _dot_tn(a, b):
    return lax.dot_general(a, b, (((0,), (0,)), ((), ())), preferred_element_type=F32)


def _rms_rows(x, g):
    ms = jnp.mean(x * x, axis=-1, keepdims=True)
    return x * lax.rsqrt(ms + EPS) * g


def _inproj_kernel(x_ref, g_ref, wa_ref, wbt_ref, u_ref, cp_ref, kp_ref, qt_ref, vt_ref, glt_ref):
    h = _rms_rows(x_ref[...], g_ref[...]).astype(BF16)
    a = _dot(h, wa_ref[...])
    for k in range(SSM_WIDTH // LANES):
        u_ref[k] = a[:, k * LANES:(k + 1) * LANES]
    cp_ref[...] = a[:, SSM_WIDTH:SSM_WIDTH + 4 * HEAD_DIM]
    kp_ref[...] = a[:, SSM_WIDTH + 4 * HEAD_DIM:]
    bt = _dot_nt(wbt_ref[...], h)
    qt_ref[...] = bt[:NSA_WIDTH]
    vt_ref[...] = bt[NSA_WIDTH:NSA_WIDTH + 4 * HEAD_DIM]
    glt_ref[...] = bt[NSA_WIDTH + 4 * HEAD_DIM:]


def _in_proj(x2, g1, wa, wbt, *, tm):
    n = x2.shape[0]
    nb = wbt.shape[0]
    const = lambda i: (0, 0)
    return pl.pallas_call(
        _inproj_kernel,
        grid=(n // tm,),
        in_specs=[
            pl.BlockSpec((tm, D_MODEL), lambda i: (i, 0)),
            pl.BlockSpec((1, D_MODEL), const),
            pl.BlockSpec(wa.shape, const),
            pl.BlockSpec(wbt.shape, const),
        ],
        out_specs=[
            pl.BlockSpec((SSM_WIDTH // LANES, tm, LANES), lambda i: (0, i, 0)),
            pl.BlockSpec((tm, 4 * HEAD_DIM), lambda i: (i, 0)),
            pl.BlockSpec((tm, 4 * HEAD_DIM), lambda i: (i, 0)),
            pl.BlockSpec((NSA_WIDTH, tm), lambda i: (0, i)),
            pl.BlockSpec((4 * HEAD_DIM, tm), lambda i: (0, i)),
            pl.BlockSpec((NSA_KV_GROUPS * GATE_ROWS, tm), lambda i: (0, i)),
        ],
        out_shape=[
            jax.ShapeDtypeStruct((SSM_WIDTH // LANES, n, LANES), F32),
            jax.ShapeDtypeStruct((n, 4 * HEAD_DIM), F32),
            jax.ShapeDtypeStruct((n, 4 * HEAD_DIM), F32),
            jax.ShapeDtypeStruct((NSA_WIDTH, n), F32),
            jax.ShapeDtypeStruct((4 * HEAD_DIM, n), F32),
            jax.ShapeDtypeStruct((NSA_KV_GROUPS * GATE_ROWS, n), F32),
        ],
        compiler_params=pltpu.CompilerParams(
            dimension_semantics=("arbitrary",), vmem_limit_bytes=VMEM_LIMIT_BYTES),
        name="in_proj",
    )(x2, g1, wa, wbt)


SCAN_SEGMENTS = 8
SCAN_SLAB = 512
STATE_HALF = SSM_STATES // 2
U_SLABS = SSM_WIDTH // LANES


def _s5_kernel(u_ref, bblk_ref, cblk_ref, acoef_ref, apow_re_ref, apow_im_ref, d_ref, wglu_ref, bglu_ref, y_ref,
               up_s, bx_s, carry_s, cseg_s, yp_s, *, tc):
    seg = tc // SCAN_SEGMENTS

    @pl.when(pl.program_id(1) == 0)
    def _reset():
        carry_s[...] = jnp.zeros_like(carry_s)

    for tl in range(seg):
        for k in range(U_SLABS):
            up_s[tl * SCAN_SEGMENTS:(tl + 1) * SCAN_SEGMENTS, k * LANES:(k + 1) * LANES] = (
                u_ref[k, pl.ds(tl, SCAN_SEGMENTS, stride=seg), :])
    up = up_s[...]
    half_w = SSM_WIDTH // 2
    for h in range(2):
        bx_s[:, h * 2 * STATE_HALF:(h + 1) * 2 * STATE_HALF] = _dot(
            up[:, h * half_w:(h + 1) * half_w].astype(BF16), bblk_ref[h])

    for slab in range(SSM_STATES // SCAN_SLAB):
        base = (slab * SCAN_SLAB // STATE_HALF) * 2 * STATE_HALF + (slab * SCAN_SLAB) % STATE_HALF
        re_cols = pl.ds(base, SCAN_SLAB)
        im_cols = pl.ds(base + STATE_HALF, SCAN_SLAB)
        nat = pl.ds(slab * SCAN_SLAB, SCAN_SLAB)
        a_re = acoef_ref[0:1, nat]
        a_im = acoef_ref[1:2, nat]
        al_re = acoef_ref[2:3, nat]
        al_im = acoef_ref[3:4, nat]

        def scan_step(tl, carry):
            xr, xi = carry
            rows = pl.ds(pl.multiple_of(tl * SCAN_SEGMENTS, SCAN_SEGMENTS), SCAN_SEGMENTS)
            nr = a_re * xr - a_im * xi + bx_s[rows, re_cols]
            ni = a_re * xi + a_im * xr + bx_s[rows, im_cols]
            bx_s[rows, re_cols] = nr
            bx_s[rows, im_cols] = ni
            return nr, ni

        zero = jnp.zeros((SCAN_SEGMENTS, SCAN_SLAB), F32)
        end_r, end_i = lax.fori_loop(0, seg, scan_step, (zero, zero), unroll=True)

        cr = carry_s[0:1, re_cols]
        ci = carry_s[0:1, im_cols]
        for r in range(SCAN_SEGMENTS):
            cseg_s[r:r + 1, re_cols] = cr
            cseg_s[r:r + 1, im_cols] = ci
            nr = al_re * cr - al_im * ci + end_r[r:r + 1]
            ni = al_re * ci + al_im * cr + end_i[r:r + 1]
            cr, ci = nr, ni
        carry_s[0:1, re_cols] = cr
        carry_s[0:1, im_cols] = ci

        cr8 = cseg_s[:, re_cols]
        ci8 = cseg_s[:, im_cols]

        def fix_step(tl, _):
            rows = pl.ds(pl.multiple_of(tl * SCAN_SEGMENTS, SCAN_SEGMENTS), SCAN_SEGMENTS)
            pr = apow_re_ref[pl.ds(tl, 1), nat]
            pi = apow_im_ref[pl.ds(tl, 1), nat]
            bx_s[rows, re_cols] = bx_s[rows, re_cols] + (pr * cr8 - pi * ci8)
            bx_s[rows, im_cols] = bx_s[rows, im_cols] + (pr * ci8 + pi * cr8)
            return 0

        lax.fori_loop(0, seg, fix_step, 0, unroll=True)

    y = jnp.concatenate(
        [_dot(bx_s[:, h * 2 * STATE_HALF:(h + 1) * 2 * STATE_HALF].astype(BF16), cblk_ref[h]) for h in range(2)],
        axis=1) + d_ref[...] * up
    yg = jax.nn.gelu(y)
    z = _dot(yg.astype(BF16), wglu_ref[...]) + bglu_ref[...]
    out = yg * jax.nn.sigmoid(z)
    for k in range(U_SLABS):
        yp_s[k] = out[:, k * LANES:(k + 1) * LANES]
    for r in range(SCAN_SEGMENTS):
        for k in range(U_SLABS):
            y_ref[r * seg:(r + 1) * seg, k * LANES:(k + 1) * LANES] = (
                yp_s[k, pl.ds(r, seg, stride=SCAN_SEGMENTS), :].astype(y_ref.dtype))


def _s5(u4, bblk, cblk, acoef, apow_re, apow_im, d_row, wglu, bglu, *, batch, seq, tc):
    nchunk = seq // tc
    const2 = lambda b, c: (0, 0)
    const3 = lambda b, c: (0, 0, 0)
    return pl.pallas_call(
        functools.partial(_s5_kernel, tc=tc),
        grid=(batch, nchunk),
        in_specs=[
            pl.BlockSpec((U_SLABS, tc, LANES), lambda b, c: (0, b * nchunk + c, 0)),
            pl.BlockSpec(bblk.shape, const3),
            pl.BlockSpec(cblk.shape, const3),
            pl.BlockSpec(acoef.shape, const2),
            pl.BlockSpec(apow_re.shape, const2),
            pl.BlockSpec(apow_im.shape, const2),
            pl.BlockSpec(d_row.shape, const2),
            pl.BlockSpec(wglu.shape, const2),
            pl.BlockSpec(bglu.shape, const2),
        ],
        out_specs=pl.BlockSpec((tc, SSM_WIDTH), lambda b, c: (b * nchunk + c, 0)),
        out_shape=jax.ShapeDtypeStruct((batch * seq, SSM_WIDTH), BF16),
        scratch_shapes=[
            pltpu.VMEM((tc, SSM_WIDTH), F32),
            pltpu.VMEM((tc, 2 * SSM_STATES), F32),
            pltpu.VMEM((8, 2 * SSM_STATES), F32),
            pltpu.VMEM((SCAN_SEGMENTS, 2 * SSM_STATES), F32),
            pltpu.VMEM((U_SLABS, tc, LANES), F32),
        ],
        compiler_params=pltpu.CompilerParams(
            dimension_semantics=("arbitrary", "arbitrary"), vmem_limit_bytes=VMEM_LIMIT_BYTES),
        name="s5",
    )(u4, bblk, cblk, acoef, apow_re, apow_im, d_row, wglu, bglu)


KEY_TILE = 256
CMP_TILE = 256
LOG2E = 1.4426950408889634
SLOPE_PARTS = 3
VALUE_ROWS = HEAD_DIM + 16
SEL_CHAINS = 6
WORD_BITS = 16


def _nsa_kernel(qt_ref, cp_ref, kp_ref, vt_ref, glt_ref, slope_ref, qg_ref, kg_ref, pe_ref,
                w1_ref, w2k_ref, w2vt_ref, out_ref,
                ks_s, kw_s, kc_s, vct_s, vst_s, vwt_s, bias_s, cmat_s, acc_s, words_s, tiles_s, count_s, amat_s, cvis_s, imp_s, *, seq, tq):
    assert tq == KEY_TILE
    i = pl.program_id(2)
    n_cmp = seq // CMP_STRIDE
    n_sel = seq // SEL_BLOCK
    k_top = min(SEL_TOP_K, n_sel)
    hq = HEADS_PER_GROUP * tq

    @pl.when(i == 0)
    def _prepare_keys_and_values():
        kg = kg_ref[...]
        offs = (lax.broadcasted_iota(jnp.int32, (seq, HEAD_DIM), 0) & (KEY_TILE - 1)).astype(F32)
        offs = jnp.where(lax.broadcasted_iota(jnp.int32, (seq, HEAD_DIM), 1) < SLOPE_PARTS, offs, 0.0)
        ks_s[...] = jnp.concatenate([_rms_rows(kp_ref[:, 0:HEAD_DIM], kg[1:2]), offs], axis=1).astype(BF16)
        kw_s[...] = jnp.concatenate([_rms_rows(kp_ref[:, HEAD_DIM:2 * HEAD_DIM], kg[2:3]), offs], axis=1).astype(BF16)
        ones = jnp.ones((VALUE_ROWS - HEAD_DIM, seq), F32)
        vst_s[...] = jnp.concatenate([vt_ref[0:HEAD_DIM, :], ones], axis=0).astype(BF16)
        vwt_s[...] = jnp.concatenate([vt_ref[HEAD_DIM:2 * HEAD_DIM, :], ones], axis=0).astype(BF16)

        lo = jnp.zeros((n_cmp, 2 * CMP_HIDDEN), F32)
        hi = jnp.zeros((n_cmp, 2 * CMP_HIDDEN), F32)
        for tt in range(CMP_STRIDE):
            tok = cp_ref[pl.ds(tt, n_cmp, stride=CMP_STRIDE), :]
            lo = lo + _dot((tok + pe_ref[tt:tt + 1, :]).astype(BF16), w1_ref[tt])
            hi = hi + _dot((tok + pe_ref[CMP_STRIDE + tt:CMP_STRIDE + tt + 1, :]).astype(BF16),
                           w1_ref[CMP_STRIDE + tt])
        hidden = jax.nn.gelu(lo + pltpu.roll(hi, n_cmp - 1, 0)).astype(BF16)
        kc_s[...] = _rms_rows(_dot(hidden[:, :CMP_HIDDEN], w2k_ref[...]), kg[0:1]).astype(BF16)
        vct_s[...] = _dot_nt(w2vt_ref[...], hidden[:, CMP_HIDDEN:]).astype(BF16)

        r = lax.broadcasted_iota(jnp.int32, (KEY_TILE, hq), 0)
        a = lax.broadcasted_iota(jnp.int32, (KEY_TILE, hq), 1) & (tq - 1)
        cmat_s[0] = jnp.where(r <= a, 0.0, NEG_INF)
        cmat_s[1] = jnp.where(r > a, 0.0, NEG_INF)

        jrow = lax.broadcasted_iota(jnp.int32, (n_cmp, hq), 0)
        amat_s[...] = (slope_ref[0, 0:1, :] * LOG2E) * (jrow * CMP_STRIDE + (CMP_BLOCK - 1)).astype(F32)
        cvis_s[...] = (r * CMP_STRIDE + (CMP_BLOCK - 1)) - a
        imp_s[...] = jnp.zeros_like(imp_s)

    qs = i * tq
    lane = lax.broadcasted_iota(jnp.int32, (1, hq), 1)
    t_row = qs + (lane & (tq - 1))
    slope_row = slope_ref[0, 0:1, :] * LOG2E

    qg = qg_ref[...]
    heads = []
    for h in range(HEADS_PER_GROUP):
        qh = qt_ref[h * HEAD_DIM:(h + 1) * HEAD_DIM, :]
        ms = jnp.mean(qh * qh, axis=0, keepdims=True)
        heads.append((qh * lax.rsqrt(ms + EPS) * qg * (HEAD_DIM ** -0.5 * LOG2E)).astype(BF16))
    pad = lax.broadcasted_iota(jnp.int32, (HEAD_DIM, hq), 0)
    slope_rows = jnp.zeros((HEAD_DIM, hq), F32)
    rest = slope_row
    for part in range(SLOPE_PARTS):
        piece = rest.astype(BF16).astype(F32)
        slope_rows = jnp.where(pad == part, piece, slope_rows)
        rest = rest - piece
    qst = jnp.concatenate([jnp.concatenate(heads, axis=1), slope_rows.astype(BF16)], axis=0)

    def cmp_branch(n_tiles):
        def run():
            scores = []
            for jt in range(n_tiles):
                rows = pl.ds(jt * CMP_TILE, CMP_TILE)
                s = _dot(kc_s[rows, :], qst[0:HEAD_DIM]) + amat_s[rows, :]
                visible = cvis_s[...] <= qs - CMP_STRIDE * CMP_TILE * jt
                scores.append(jnp.where(visible, s, NEG_INF))
            m = functools.reduce(jnp.maximum, [jnp.max(s, axis=0, keepdims=True) for s in scores])
            shift = jnp.where(m > 0.5 * NEG_INF, m, -NEG_INF)
            exps = [jnp.exp2(s - shift) for s in scores]
            l = functools.reduce(lambda x, y: x + y, [jnp.sum(e, axis=0, keepdims=True) for e in exps])
            r = jnp.where(l > 0.0, 1.0 / l, 0.0)
            acc = jnp.zeros((HEAD_DIM, hq), F32)
            for jt, e in enumerate(exps):
                p = e * r
                imp = p[:, 0:tq]
                for h in range(1, HEADS_PER_GROUP):
                    imp = imp + p[:, h * tq:(h + 1) * tq]
                for k in range(tq // LANES):
                    imp_s[k, pl.ds(jt * CMP_TILE, CMP_TILE), :] = imp[:, k * LANES:(k + 1) * LANES]
                acc = acc + _dot(vct_s[:, pl.ds(jt * CMP_TILE, CMP_TILE)], p.astype(BF16))
            return acc
        return run

    n_cmp_live = (qs + tq - CMP_STRIDE) // CMP_STRIDE
    cmp_tiles = n_cmp // CMP_TILE
    if cmp_tiles >= 2:
        o_cmp = lax.cond(n_cmp_live <= (cmp_tiles // 2) * CMP_TILE, cmp_branch(cmp_tiles // 2), cmp_branch(cmp_tiles))
    else:
        o_cmp = cmp_branch(cmp_tiles)()

    per_block = SEL_BLOCK // CMP_STRIDE
    chunk = [jnp.concatenate([imp_s[k, pl.ds(c, n_sel, stride=per_block), :] for k in range(tq // LANES)], axis=1)
             for c in range(per_block)]
    imp_sel = chunk[0]
    for c in range(1, per_block):
        imp_sel = imp_sel + chunk[c]
    blk = lax.broadcasted_iota(jnp.int32, (n_sel, tq), 0)
    imp_sel = imp_sel + jnp.where(blk == 0, 0.0, pltpu.roll(chunk[per_block - 1], 1, 0))
    cur = jnp.right_shift(t_row[:, 0:tq], SEL_BLOCK.bit_length() - 1)
    forced = (blk == 0) | (blk == cur) | (blk == cur - 1)
    valid = blk <= cur
    blk_f = blk.astype(F32)

    def pick_round(_, carry):
        work, taken = carry
        top = jnp.max(work, axis=0, keepdims=True)
        first = jnp.min(jnp.where(work == top, blk_f, float(n_sel)), axis=0, keepdims=True)
        pick = blk_f == first
        return jnp.where(pick, REMOVED, work), jnp.where(pick, 1.0, taken)

    _, taken = lax.fori_loop(
        0, max(k_top - 3, 0), pick_round,
        (jnp.where(valid & jnp.logical_not(forced), imp_sel, NEG_INF), jnp.where(forced & valid, 1.0, 0.0)),
        unroll=True)
    chosen = taken > 0.5
    bias = jnp.where(chosen, 0.0, NEG_INF)
    bias_s[...] = jnp.concatenate([bias] * HEADS_PER_GROUP, axis=1)

    blocks_per_tile = KEY_TILE // SEL_BLOCK
    hit = jnp.where(chosen & (blk < blocks_per_tile * i), 1.0, 0.0)
    blk_hit = jnp.max(hit, axis=1, keepdims=True)
    weight = jnp.left_shift(1, lax.broadcasted_iota(jnp.int32, (n_sel, 1), 0) & (WORD_BITS - 1)).astype(F32)
    packed = blk_hit * weight
    for w in range(n_sel // WORD_BITS):
        word = jnp.sum(packed[w * WORD_BITS:(w + 1) * WORD_BITS], axis=0, keepdims=True).astype(jnp.int32)
        words_s[w] = word[0, 0]

    tiles_per_word = WORD_BITS // blocks_per_tile
    count_s[0] = 0
    for w in range(n_sel // WORD_BITS):
        @pl.when(words_s[w] != 0)
        def _append_tiles_of_word():
            word = words_s[w]
            count = count_s[0]
            for k in range(tiles_per_word):
                bits = jnp.right_shift(word, blocks_per_tile * k) & ((1 << blocks_per_tile) - 1)
                tiles_s[count] = w * tiles_per_word + k
                count = count + jnp.where(bits != 0, 1, 0)
            count_s[0] = count

    n_active = count_s[0]
    for c in range(SEL_CHAINS):
        tiles_s[n_active + c] = -1

    def tile_scores(kts, k_refs, masks):
        k0s = [pl.multiple_of(kt * KEY_TILE, KEY_TILE) for kt in kts]
        scores = []
        for k0, k_s, mask in zip(k0s, k_refs, masks):
            s = _dot(k_s[pl.ds(k0, KEY_TILE), :], qst)
            scores.append(s if mask is None else s + cmat_s[mask])
        return scores

    def tiles_finish(kts, scores, vt_refs, biases_per_tile, ms, accs):
        k0s = [pl.multiple_of(kt * KEY_TILE, KEY_TILE) for kt in kts]
        new_ms, alphas, probs = [], [], []
        for s, k0, biases, m in zip(scores, k0s, biases_per_tile, ms):
            rv = slope_row * (t_row - k0).astype(F32)
            rows = KEY_TILE // len(biases)
            parts = [s[b * rows:(b + 1) * rows] for b in range(len(biases))]
            m_new = m
            for part, bias_row in zip(parts, biases):
                m_new = jnp.maximum(m_new, jnp.max(part, axis=0, keepdims=True) + (bias_row - rv))
            seen = m_new > 0.5 * NEG_INF
            ps = []
            for part, bias_row in zip(parts, biases):
                shift = jnp.where(seen, m_new + (rv - bias_row), -NEG_INF)
                ps.append(jnp.exp2(part - shift))
            new_ms.append(m_new)
            alphas.append(jnp.exp2(m - m_new))
            probs.append((jnp.concatenate(ps, axis=0) if len(ps) > 1 else ps[0]).astype(BF16))
        new_accs = [alpha * acc + _dot(vt_s[:, pl.ds(k0, KEY_TILE)], p)
                    for alpha, acc, vt_s, k0, p in zip(alphas, accs, vt_refs, k0s, probs)]
        return new_ms, new_accs

    def tiles_update(kts, k_refs, vt_refs, masks, biases_per_tile, ms, accs):
        return tiles_finish(kts, tile_scores(kts, k_refs, masks), vt_refs, biases_per_tile, ms, accs)

    def merge(ms, accs):
        m = functools.reduce(jnp.maximum, ms)
        acc = functools.reduce(lambda x, y: x + y, [jnp.exp2(mc - m) * ac for mc, ac in zip(ms, accs)])
        return acc[0:HEAD_DIM] * (1.0 / acc[HEAD_DIM:HEAD_DIM + 1])

    m_init = jnp.full((1, hq), NEG_INF, F32)
    acc_init = jnp.zeros((VALUE_ROWS, hq), F32)
    dead_row = jnp.full((1, hq), NEG_INF, F32)
    live_row = jnp.zeros((1, hq), F32)

    def block_biases(kt):
        return [bias_s[pl.ds(blocks_per_tile * kt + b, 1), :] for b in range(blocks_per_tile)]

    for c in range(SEL_CHAINS):
        acc_s[c] = acc_init

    def sel_step(j, ms):
        entries = [tiles_s[j * SEL_CHAINS + c] for c in range(SEL_CHAINS)]
        kts = [jnp.maximum(e, 0) for e in entries]
        biases = [[jnp.where(e < 0, dead_row, row) for row in block_biases(kt)] for e, kt in zip(entries, kts)]
        new_ms, new_accs = tiles_update(kts, [ks_s] * SEL_CHAINS, [vst_s] * SEL_CHAINS, [None] * SEL_CHAINS, biases,
                                        ms, [acc_s[c] for c in range(SEL_CHAINS)])
        for c in range(SEL_CHAINS):
            acc_s[c] = new_accs[c]
        return tuple(new_ms)

    ms = lax.fori_loop(0, (n_active + SEL_CHAINS - 1) // SEL_CHAINS, sel_step, (m_init,) * SEL_CHAINS)

    far = WINDOW // KEY_TILE
    kts = [i - d for d in range(far + 1)]
    win_masks = [0 if d == 0 else (1 if d == far else None) for d in range(far + 1)]
    win_biases = [[jnp.where(kt < 0, dead_row, live_row)] for kt in kts]
    n_win = far + 1
    ms_x, accs_x = tiles_update(
        [i] + [jnp.maximum(kt, 0) for kt in kts], [ks_s] + [kw_s] * n_win, [vst_s] + [vwt_s] * n_win,
        [0] + win_masks, [block_biases(i)] + win_biases, [m_init] * (n_win + 1), [acc_init] * (n_win + 1))
    o_sel = merge(list(ms) + ms_x[:1], [acc_s[c] for c in range(SEL_CHAINS)] + accs_x[:1])
    o_win = merge(ms_x[1:], accs_x[1:])

    gl = glt_ref[...]

    def gate_row(branch):
        rows = [jax.nn.sigmoid(gl[branch * HEADS_PER_GROUP + h:branch * HEADS_PER_GROUP + h + 1, :])
                for h in range(HEADS_PER_GROUP)]
        return jnp.concatenate(rows, axis=1)

    ot = gate_row(0) * o_cmp + gate_row(1) * o_sel + gate_row(2) * o_win
    out_ref[...] = jnp.concatenate(
        [ot[:, h * tq:(h + 1) * tq] for h in range(HEADS_PER_GROUP)], axis=0).astype(out_ref.dtype)


def _nsa(qt, cp, kp, vt, glt, slopes, qg, kg, pe, w1, w2k, w2vt, *, batch, seq, tq):
    nq = seq // tq
    n_cmp = seq // CMP_STRIDE
    n_sel = seq // SEL_BLOCK
    hq = HEADS_PER_GROUP * tq
    const2 = lambda b, g, i: (0, 0)
    const3 = lambda b, g, i: (0, 0, 0)
    return pl.pallas_call(
        functools.partial(_nsa_kernel, seq=seq, tq=tq),
        grid=(batch, NSA_KV_GROUPS, nq),
        in_specs=[
            pl.BlockSpec((HEADS_PER_GROUP * HEAD_DIM, tq), lambda b, g, i: (g, b * nq + i)),
            pl.BlockSpec((seq, 2 * HEAD_DIM), lambda b, g, i: (b, g)),
            pl.BlockSpec((seq, 2 * HEAD_DIM), lambda b, g, i: (b, g)),
            pl.BlockSpec((2 * HEAD_DIM, seq), lambda b, g, i: (g, b)),
            pl.BlockSpec((GATE_ROWS, tq), lambda b, g, i: (g, b * nq + i)),
            pl.BlockSpec((1, 8, hq), lambda b, g, i: (g, 0, 0)),
            pl.BlockSpec(qg.shape, const2),
            pl.BlockSpec(kg.shape, const2),
            pl.BlockSpec(pe.shape, const2),
            pl.BlockSpec(w1.shape, const3),
            pl.BlockSpec(w2k.shape, const2),
            pl.BlockSpec(w2vt.shape, const2),
        ],
        out_specs=pl.BlockSpec((HEADS_PER_GROUP * HEAD_DIM, tq), lambda b, g, i: (g, b * nq + i)),
        out_shape=jax.ShapeDtypeStruct((NSA_WIDTH, batch * seq), BF16),
        scratch_shapes=[
            pltpu.VMEM((seq, 2 * HEAD_DIM), BF16),
            pltpu.VMEM((seq, 2 * HEAD_DIM), BF16),
            pltpu.VMEM((n_cmp, HEAD_DIM), BF16),
            pltpu.VMEM((HEAD_DIM, n_cmp), BF16),
            pltpu.VMEM((VALUE_ROWS, seq), BF16),
            pltpu.VMEM((VALUE_ROWS, seq), BF16),
            pltpu.VMEM((n_sel, hq), F32),
            pltpu.VMEM((2, KEY_TILE, hq), F32),
            pltpu.VMEM((SEL_CHAINS, VALUE_ROWS, hq), F32),
            pltpu.SMEM((n_sel // WORD_BITS,), jnp.int32),
            pltpu.SMEM((seq // KEY_TILE + SEL_CHAINS,), jnp.int32),
            pltpu.SMEM((1,), jnp.int32),
            pltpu.VMEM((n_cmp, hq), F32),
            pltpu.VMEM((CMP_TILE, hq), jnp.int32),
            pltpu.VMEM((tq // LANES, n_cmp, LANES), F32),
        ],
        compiler_params=pltpu.CompilerParams(
            dimension_semantics=("arbitrary", "arbitrary", "arbitrary"), vmem_limit_bytes=VMEM_LIMIT_BYTES),
        name="nsa",
    )(qt, cp, kp, vt, glt, slopes, qg, kg, pe, w1, w2k, w2vt)


def _final_kernel(x_ref, ys_ref, ynt_ref, g1_ref, wmg_ref, wps_ref, wpn_ref, wout_ref, g2_ref, wup_ref, wdn_ref,
                  o_ref):
    x = x_ref[...]
    h = _rms_rows(x, g1_ref[...]).astype(BF16)
    mg = _dot(h, wmg_ref[...])
    ps = _dot(ys_ref[...], wps_ref[...])
    pn = _dot_tn(ynt_ref[...], wpn_ref[...])
    merged = jax.nn.sigmoid(mg[:, :D_MODEL]) * ps + jax.nn.sigmoid(mg[:, D_MODEL:]) * pn
    x1 = x + _dot(merged.astype(BF16), wout_ref[...])
    h2 = _rms_rows(x1, g2_ref[...]).astype(BF16)
    a = jnp.maximum(_dot(h2, wup_ref[...]), 0.0)
    o_ref[...] = x1 + _dot((a * a).astype(BF16), wdn_ref[...])


def _final(x2, ys, ynt, g1, wmg, wps, wpn, wout, g2, wup, wdn, *, tm):
    n = x2.shape[0]
    const = lambda i: (0, 0)
    resident = lambda w: pl.BlockSpec(w.shape, const, pipeline_mode=pl.Buffered(1))
    return pl.pallas_call(
        _final_kernel,
        grid=(n // tm,),
        in_specs=[
            pl.BlockSpec((tm, D_MODEL), lambda i: (i, 0)),
            pl.BlockSpec((tm, SSM_WIDTH), lambda i: (i, 0)),
            pl.BlockSpec((NSA_WIDTH, tm), lambda i: (0, i)),
            resident(g1), resident(wmg), resident(wps), resident(wpn), resident(wout),
            resident(g2), resident(wup), resident(wdn),
        ],
        out_specs=pl.BlockSpec((tm, D_MODEL), lambda i: (i, 0)),
        out_shape=jax.ShapeDtypeStruct((n, D_MODEL), F32),
        compiler_params=pltpu.CompilerParams(
            dimension_semantics=("arbitrary",), vmem_limit_bytes=VMEM_LIMIT_BYTES),
        name="final",
    )(x2, ys, ynt, g1, wmg, wps, wpn, wout, g2, wup, wdn)


def _pack_in_proj(w_in):
    o1 = SSM_WIDTH
    o2 = o1 + NSA_WIDTH
    o3 = o2 + KV_WIDTH
    o4 = o3 + 3 * NSA_HEADS
    w_u, w_q, w_kv, w_gl = w_in[:, :o1], w_in[:, o1:o2], w_in[:, o2:o3], w_in[:, o3:o4]
    w_mg = w_in[:, o4:]
    kv = w_kv.reshape(D_MODEL, 3, 2, NSA_KV_GROUPS, HEAD_DIM)
    groups = range(NSA_KV_GROUPS)
    cpack = [jnp.concatenate([kv[:, 0, 0, g], kv[:, 0, 1, g]], axis=1) for g in groups]
    kpack = [jnp.concatenate([kv[:, 1, 0, g], kv[:, 2, 0, g]], axis=1) for g in groups]
    wa = jnp.concatenate([w_u] + cpack + kpack, axis=1)
    vrows = [jnp.concatenate([kv[:, 1, 1, g], kv[:, 2, 1, g]], axis=1) for g in range(NSA_KV_GROUPS)]
    gl = w_gl.reshape(D_MODEL, 3, NSA_KV_GROUPS, HEADS_PER_GROUP)
    pad = jnp.zeros((D_MODEL, GATE_ROWS - 3 * HEADS_PER_GROUP), w_in.dtype)
    glrows = [jnp.concatenate([gl[:, 0, g], gl[:, 1, g], gl[:, 2, g], pad], axis=1) for g in range(NSA_KV_GROUPS)]
    wbt = jnp.concatenate([w_q] + vrows + glrows, axis=1).T
    return wa.astype(BF16), wbt.astype(BF16), w_mg.astype(BF16)


def _pack_compress(pe_k, pe_v, w1k, w1v):
    pe = jnp.concatenate([pe_k, pe_v], axis=1)
    w1k = w1k.reshape(CMP_BLOCK, HEAD_DIM, CMP_HIDDEN)
    w1v = w1v.reshape(CMP_BLOCK, HEAD_DIM, CMP_HIDDEN)
    zero = jnp.zeros_like(w1k)
    w1 = jnp.concatenate([jnp.concatenate([w1k, zero], axis=2), jnp.concatenate([zero, w1v], axis=2)], axis=1)
    return pe, w1.astype(BF16)


def _pack_s5(a_re, a_im, log_dt, b_re, b_im, c_re, c_im, seg):
    dt = jnp.exp(log_dt)[:, None]
    mag = jnp.exp(a_re * dt)
    abar_re = mag * jnp.cos(a_im * dt)
    abar_im = mag * jnp.sin(a_im * dt)
    den = a_re * a_re + a_im * a_im
    nr = abar_re - 1.0
    fr = (nr * a_re + abar_im * a_im) / den
    fi = (abar_im * a_re - nr * a_im) / den
    bbar_re = fr[..., None] * b_re - fi[..., None] * b_im
    bbar_im = fr[..., None] * b_im + fi[..., None] * b_re
    gh = SSM_GROUPS // 2
    eye = jnp.eye(gh, dtype=F32)

    def in_blocks(m):
        return (m.transpose(0, 2, 1)[:, :, None, :] * eye[:, None, :, None]).reshape(gh * SSM_GROUP, gh * SSM_STATE)

    def out_blocks(m):
        return (m.transpose(0, 2, 1)[:, :, None, :] * eye[:, None, :, None]).reshape(gh * SSM_STATE, gh * SSM_GROUP)

    halves = [slice(0, gh), slice(gh, SSM_GROUPS)]
    bblk = jnp.stack([jnp.concatenate([in_blocks(bbar_re[h]), in_blocks(bbar_im[h])], axis=1) for h in halves])
    cblk = jnp.stack([jnp.concatenate([out_blocks(c_re[h]), -out_blocks(c_im[h])], axis=0) for h in halves])

    k = jnp.arange(1, seg + 1, dtype=F32)[:, None, None]
    pmag = jnp.exp(k * (a_re * dt))
    apow_re = (pmag * jnp.cos(k * (a_im * dt))).reshape(seg, SSM_STATES)
    apow_im = (pmag * jnp.sin(k * (a_im * dt))).reshape(seg, SSM_STATES)
    acoef = jnp.zeros((8, SSM_STATES), F32)
    acoef = acoef.at[0].set(abar_re.reshape(-1)).at[1].set(abar_im.reshape(-1))
    acoef = acoef.at[2].set(apow_re[seg - 1]).at[3].set(apow_im[seg - 1])
    return bblk.astype(BF16), cblk.astype(BF16), acoef, apow_re, apow_im


def kernel(x, norm1_g, w_in, ssm_a_re, ssm_a_im, ssm_log_dt, ssm_b_re, ssm_b_im, ssm_c_re, ssm_c_im, ssm_d, ssm_w_glu, ssm_b_glu, cmp_pe_k, cmp_pe_v, cmp_wk1, cmp_wk2, cmp_wv1, cmp_wv2, q_norm_g, k_norm_g, w_proj_ssm, w_proj_nsa, w_out, norm2_g, w_up, w_down):
    batch, seq, d_model = x.shape
    assert d_model == D_MODEL and seq % 512 == 0
    n = batch * seq
    tq = KEY_TILE
    depth = w_in.shape[0]
    head = jnp.arange(1, NSA_HEADS + 1, dtype=F32)
    slopes = jnp.exp2(-8.0 * head / NSA_HEADS).reshape(NSA_KV_GROUPS, 1, HEADS_PER_GROUP, 1)
    slopes = jnp.broadcast_to(slopes, (NSA_KV_GROUPS, 8, HEADS_PER_GROUP, tq)).reshape(NSA_KV_GROUPS, 8, -1)

    x2 = x.reshape(n, D_MODEL)
    for l in range(depth):
        wa, wbt, wmg = _pack_in_proj(w_in[l])
        g1 = norm1_g[l].reshape(1, D_MODEL)
        u, cp, kp, qt, vt, glt = _in_proj(x2, g1, wa, wbt, tm=512)

        tc = 256
        bblk, cblk, acoef, apow_re, apow_im = _pack_s5(ssm_a_re[l], ssm_a_im[l], ssm_log_dt[l], ssm_b_re[l],
                                                       ssm_b_im[l], ssm_c_re[l], ssm_c_im[l], tc // SCAN_SEGMENTS)
        ys = _s5(u, bblk, cblk, acoef, apow_re, apow_im, ssm_d[l].reshape(1, SSM_WIDTH),
                 ssm_w_glu[l].astype(BF16), ssm_b_glu[l].reshape(1, SSM_WIDTH), batch=batch, seq=seq, tc=tc)

        qg = jnp.broadcast_to(q_norm_g[l].reshape(HEAD_DIM, 1), (HEAD_DIM, tq))
        kg = jnp.zeros((8, HEAD_DIM), F32).at[0:3].set(k_norm_g[l])
        pe, w1 = _pack_compress(cmp_pe_k[l], cmp_pe_v[l], cmp_wk1[l], cmp_wv1[l])
        ynt = _nsa(qt, cp, kp, vt, glt, slopes, qg, kg, pe, w1,
                   cmp_wk2[l].astype(BF16), cmp_wv2[l].T.astype(BF16), batch=batch, seq=seq, tq=tq)

        x2 = _final(x2, ys, ynt, g1, wmg, w_proj_ssm[l].astype(BF16), w_proj_nsa[l].astype(BF16),
                    w_out[l].astype(BF16), norm2_g[l].reshape(1, D_MODEL), w_up[l].astype(BF16),
                    w_down[l].astype(BF16), tm=256)
    return x2.reshape(batch, seq, D_MODEL)
```

```python
import functools

import jax
import jax.numpy as jnp
from jax import lax
from jax.experimental import pallas as pl
from jax.experimental.pallas import tpu as pltpu

D_MODEL = 1024
SSM_WIDTH = D_MODEL // 2
SSM_GROUP = 16
SSM_GROUPS = SSM_WIDTH // SSM_GROUP
SSM_STATE = 64
SSM_STATES = SSM_GROUPS * SSM_STATE
NSA_HEADS = 8
NSA_KV_GROUPS = 2
HEADS_PER_GROUP = NSA_HEADS // NSA_KV_GROUPS
HEAD_DIM = 64
NSA_WIDTH = NSA_HEADS * HEAD_DIM
CMP_STRIDE = 16
CMP_BLOCK = 2 * CMP_STRIDE
CMP_HIDDEN = 2 * HEAD_DIM
SEL_BLOCK = 64
SEL_TOP_K = 16
WINDOW = 512
KV_WIDTH = 3 * 2 * NSA_KV_GROUPS * HEAD_DIM
EPS = 1e-6
NEG_INF = -1e30
REMOVED = -3e38

LANES = 128
VMEM_LIMIT_BYTES = 56 * 1024 * 1024

IN_PROJ_ROWS = 1024
S5_CHUNK = 256
FINAL_ROWS = 512

GATE_ROWS = 16

F32 = jnp.float32
BF16 = jnp.bfloat16


def _dot(a, b):
    return jnp.dot(a, b, preferred_element_type=F32)


def _dot_nt(a, b):
    return lax.dot_general(a, b, (((1,), (1,)), ((), ())), preferred_element_type=F32)


def _dot_tn(a, b):
    return lax.dot_general(a, b, (((0,), (0,)), ((), ())), preferred_element_type=F32)


def _rms_rows(x, g):
    ms = jnp.mean(x * x, axis=-1, keepdims=True)
    return x * lax.rsqrt(ms + EPS) * g


def _inproj_kernel(x_ref, g_ref, wa_ref, wbt_ref, u_ref, cp_ref, kp_ref, qt_ref, vt_ref, glt_ref):
    h = _rms_rows(x_ref[...], g_ref[...]).astype(BF16)
    a = _dot(h, wa_ref[...])
    for k in range(SSM_WIDTH // LANES):
        u_ref[k] = a[:, k * LANES:(k + 1) * LANES]
    cp_ref[...] = a[:, SSM_WIDTH:SSM_WIDTH + 4 * HEAD_DIM]
    kp_ref[...] = a[:, SSM_WIDTH + 4 * HEAD_DIM:]
    bt = _dot_nt(wbt_ref[...], h)
    qt_ref[...] = bt[:NSA_WIDTH]
    vt_ref[...] = bt[NSA_WIDTH:NSA_WIDTH + 4 * HEAD_DIM]
    glt_ref[...] = bt[NSA_WIDTH + 4 * HEAD_DIM:]


def _in_proj(x2, g1, wa, wbt, *, tm):
    n = x2.shape[0]
    const = lambda i: (0, 0)
    return pl.pallas_call(
        _inproj_kernel,
        grid=(n // tm,),
        in_specs=[
            pl.BlockSpec((tm, D_MODEL), lambda i: (i, 0)),
            pl.BlockSpec((1, D_MODEL), const),
            pl.BlockSpec(wa.shape, const),
            pl.BlockSpec(wbt.shape, const),
        ],
        out_specs=[
            pl.BlockSpec((SSM_WIDTH // LANES, tm, LANES), lambda i: (0, i, 0)),
            pl.BlockSpec((tm, 4 * HEAD_DIM), lambda i: (i, 0)),
            pl.BlockSpec((tm, 4 * HEAD_DIM), lambda i: (i, 0)),
            pl.BlockSpec((NSA_WIDTH, tm), lambda i: (0, i)),
            pl.BlockSpec((4 * HEAD_DIM, tm), lambda i: (0, i)),
            pl.BlockSpec((NSA_KV_GROUPS * GATE_ROWS, tm), lambda i: (0, i)),
        ],
        out_shape=[
            jax.ShapeDtypeStruct((SSM_WIDTH // LANES, n, LANES), F32),
            jax.ShapeDtypeStruct((n, 4 * HEAD_DIM), F32),
            jax.ShapeDtypeStruct((n, 4 * HEAD_DIM), F32),
            jax.ShapeDtypeStruct((NSA_WIDTH, n), F32),
            jax.ShapeDtypeStruct((4 * HEAD_DIM, n), F32),
            jax.ShapeDtypeStruct((NSA_KV_GROUPS * GATE_ROWS, n), F32),
        ],
        compiler_params=pltpu.CompilerParams(
            dimension_semantics=("arbitrary",), vmem_limit_bytes=VMEM_LIMIT_BYTES),
        name="in_proj",
    )(x2, g1, wa, wbt)


SCAN_SEGMENTS = 8
SCAN_SLAB = 512
STATE_HALF = SSM_STATES // 2
U_SLABS = SSM_WIDTH // LANES


def _s5_kernel(u_ref, bblk_ref, cblk_ref, acoef_ref, apow_re_ref, apow_im_ref, d_ref, wglu_ref, bglu_ref, y_ref,
               up_s, bx_s, carry_s, cseg_s, yp_s, *, tc):
    seg = tc // SCAN_SEGMENTS

    @pl.when(pl.program_id(1) == 0)
    def _reset():
        carry_s[...] = jnp.zeros_like(carry_s)

    for tl in range(seg):
        for k in range(U_SLABS):
            up_s[tl * SCAN_SEGMENTS:(tl + 1) * SCAN_SEGMENTS, k * LANES:(k + 1) * LANES] = (
                u_ref[k, pl.ds(tl, SCAN_SEGMENTS, stride=seg), :])
    up = up_s[...]
    half_w = SSM_WIDTH // 2
    for h in range(2):
        bx_s[:, h * 2 * STATE_HALF:(h + 1) * 2 * STATE_HALF] = _dot(
            up[:, h * half_w:(h + 1) * half_w].astype(BF16), bblk_ref[h])

    for slab in range(SSM_STATES // SCAN_SLAB):
        base = (slab * SCAN_SLAB // STATE_HALF) * 2 * STATE_HALF + (slab * SCAN_SLAB) % STATE_HALF
        re_cols = pl.ds(base, SCAN_SLAB)
        im_cols = pl.ds(base + STATE_HALF, SCAN_SLAB)
        nat = pl.ds(slab * SCAN_SLAB, SCAN_SLAB)
        a_re = acoef_ref[0:1, nat]
        a_im = acoef_ref[1:2, nat]
        al_re = acoef_ref[2:3, nat]
        al_im = acoef_ref[3:4, nat]

        def scan_step(tl, carry):
            xr, xi = carry
            rows = pl.ds(pl.multiple_of(tl * SCAN_SEGMENTS, SCAN_SEGMENTS), SCAN_SEGMENTS)
            nr = a_re * xr - a_im * xi + bx_s[rows, re_cols]
            ni = a_re * xi + a_im * xr + bx_s[rows, im_cols]
            bx_s[rows, re_cols] = nr
            bx_s[rows, im_cols] = ni
            return nr, ni

        zero = jnp.zeros((SCAN_SEGMENTS, SCAN_SLAB), F32)
        end_r, end_i = lax.fori_loop(0, seg, scan_step, (zero, zero), unroll=True)

        cr = carry_s[0:1, re_cols]
        ci = carry_s[0:1, im_cols]
        for r in range(SCAN_SEGMENTS):
            cseg_s[r:r + 1, re_cols] = cr
            cseg_s[r:r + 1, im_cols] = ci
            nr = al_re * cr - al_im * ci + end_r[r:r + 1]
            ni = al_re * ci + al_im * cr + end_i[r:r + 1]
            cr, ci = nr, ni
        carry_s[0:1, re_cols] = cr
        carry_s[0:1, im_cols] = ci

        cr8 = cseg_s[:, re_cols]
        ci8 = cseg_s[:, im_cols]

        def fix_step(tl, _):
            rows = pl.ds(pl.multiple_of(tl * SCAN_SEGMENTS, SCAN_SEGMENTS), SCAN_SEGMENTS)
            pr = apow_re_ref[pl.ds(tl, 1), nat]
            pi = apow_im_ref[pl.ds(tl, 1), nat]
            bx_s[rows, re_cols] = bx_s[rows, re_cols] + (pr * cr8 - pi * ci8)
            bx_s[rows, im_cols] = bx_s[rows, im_cols] + (pr * ci8 + pi * cr8)
            return 0

        lax.fori_loop(0, seg, fix_step, 0, unroll=True)

    y = jnp.concatenate(
        [_dot(bx_s[:, h * 2 * STATE_HALF:(h + 1) * 2 * STATE_HALF].astype(BF16), cblk_ref[h]) for h in range(2)],
        axis=1) + d_ref[...] * up
    yg = jax.nn.gelu(y)
    z = _dot(yg.astype(BF16), wglu_ref[...]) + bglu_ref[...]
    out = yg * jax.nn.sigmoid(z)
    for k in range(U_SLABS):
        yp_s[k] = out[:, k * LANES:(k + 1) * LANES]
    for r in range(SCAN_SEGMENTS):
        for k in range(U_SLABS):
            y_ref[r * seg:(r + 1) * seg, k * LANES:(k + 1) * LANES] = (
                yp_s[k, pl.ds(r, seg, stride=SCAN_SEGMENTS), :].astype(y_ref.dtype))


def _s5(u4, bblk, cblk, acoef, apow_re, apow_im, d_row, wglu, bglu, *, batch, seq, tc):
    nchunk = seq // tc
    const2 = lambda b, c: (0, 0)
    const3 = lambda b, c: (0, 0, 0)
    return pl.pallas_call(
        functools.partial(_s5_kernel, tc=tc),
        grid=(batch, nchunk),
        in_specs=[
            pl.BlockSpec((U_SLABS, tc, LANES), lambda b, c: (0, b * nchunk + c, 0)),
            pl.BlockSpec(bblk.shape, const3),
            pl.BlockSpec(cblk.shape, const3),
            pl.BlockSpec(acoef.shape, const2),
            pl.BlockSpec(apow_re.shape, const2),
            pl.BlockSpec(apow_im.shape, const2),
            pl.BlockSpec(d_row.shape, const2),
            pl.BlockSpec(wglu.shape, const2),
            pl.BlockSpec(bglu.shape, const2),
        ],
        out_specs=pl.BlockSpec((tc, SSM_WIDTH), lambda b, c: (b * nchunk + c, 0)),
        out_shape=jax.ShapeDtypeStruct((batch * seq, SSM_WIDTH), BF16),
        scratch_shapes=[
            pltpu.VMEM((tc, SSM_WIDTH), F32),
            pltpu.VMEM((tc, 2 * SSM_STATES), F32),
            pltpu.VMEM((8, 2 * SSM_STATES), F32),
            pltpu.VMEM((SCAN_SEGMENTS, 2 * SSM_STATES), F32),
            pltpu.VMEM((U_SLABS, tc, LANES), F32),
        ],
        compiler_params=pltpu.CompilerParams(
            dimension_semantics=("arbitrary", "arbitrary"), vmem_limit_bytes=VMEM_LIMIT_BYTES),
        name="s5",
    )(u4, bblk, cblk, acoef, apow_re, apow_im, d_row, wglu, bglu)


KEY_TILE = 256
CMP_TILE = 256
LOG2E = 1.4426950408889634
SLOPE_PARTS = 3
VALUE_ROWS = HEAD_DIM + 16
SEL_CHAINS = 6
WORD_BITS = 16


def _nsa_kernel(qt_ref, cp_ref, kp_ref, vt_ref, glt_ref, slope_ref, qg_ref, kg_ref, pe_ref,
                w1_ref, w2k_ref, w2vt_ref, out_ref,
                ks_s, kw_s, kc_s, vct_s, vst_s, vwt_s, bias_s, cmat_s, acc_s, words_s, tiles_s, count_s, amat_s, cvis_s, imp_s, *, seq, tq):
    assert tq == KEY_TILE
    i = pl.program_id(2)
    n_cmp = seq // CMP_STRIDE
    n_sel = seq // SEL_BLOCK
    k_top = min(SEL_TOP_K, n_sel)
    hq = HEADS_PER_GROUP * tq

    @pl.when(i == 0)
    def _prepare_keys_and_values():
        kg = kg_ref[...]
        offs = (lax.broadcasted_iota(jnp.int32, (seq, HEAD_DIM), 0) & (KEY_TILE - 1)).astype(F32)
        offs = jnp.where(lax.broadcasted_iota(jnp.int32, (seq, HEAD_DIM), 1) < SLOPE_PARTS, offs, 0.0)
        ks_s[...] = jnp.concatenate([_rms_rows(kp_ref[:, 0:HEAD_DIM], kg[1:2]), offs], axis=1).astype(BF16)
        kw_s[...] = jnp.concatenate([_rms_rows(kp_ref[:, HEAD_DIM:2 * HEAD_DIM], kg[2:3]), offs], axis=1).astype(BF16)
        ones = jnp.ones((VALUE_ROWS - HEAD_DIM, seq), F32)
        vst_s[...] = jnp.concatenate([vt_ref[0:HEAD_DIM, :], ones], axis=0).astype(BF16)
        vwt_s[...] = jnp.concatenate([vt_ref[HEAD_DIM:2 * HEAD_DIM, :], ones], axis=0).astype(BF16)

        lo = jnp.zeros((n_cmp, 2 * CMP_HIDDEN), F32)
        hi = jnp.zeros((n_cmp, 2 * CMP_HIDDEN), F32)
        for tt in range(CMP_STRIDE):
            tok = cp_ref[pl.ds(tt, n_cmp, stride=CMP_STRIDE), :]
            lo = lo + _dot((tok + pe_ref[tt:tt + 1, :]).astype(BF16), w1_ref[tt])
            hi = hi + _dot((tok + pe_ref[CMP_STRIDE + tt:CMP_STRIDE + tt + 1, :]).astype(BF16),
                           w1_ref[CMP_STRIDE + tt])
        hidden = jax.nn.gelu(lo + pltpu.roll(hi, n_cmp - 1, 0)).astype(BF16)
        kc_s[...] = _rms_rows(_dot(hidden[:, :CMP_HIDDEN], w2k_ref[...]), kg[0:1]).astype(BF16)
        vct_s[...] = _dot_nt(w2vt_ref[...], hidden[:, CMP_HIDDEN:]).astype(BF16)

        r = lax.broadcasted_iota(jnp.int32, (KEY_TILE, hq), 0)
        a = lax.broadcasted_iota(jnp.int32, (KEY_TILE, hq), 1) & (tq - 1)
        cmat_s[0] = jnp.where(r <= a, 0.0, NEG_INF)
        cmat_s[1] = jnp.where(r > a, 0.0, NEG_INF)

        jrow = lax.broadcasted_iota(jnp.int32, (n_cmp, hq), 0)
        amat_s[...] = (slope_ref[0, 0:1, :] * LOG2E) * (jrow * CMP_STRIDE + (CMP_BLOCK - 1)).astype(F32)
        cvis_s[...] = (r * CMP_STRIDE + (CMP_BLOCK - 1)) - a
        imp_s[...] = jnp.zeros_like(imp_s)

    qs = i * tq
    lane = lax.broadcasted_iota(jnp.int32, (1, hq), 1)
    t_row = qs + (lane & (tq - 1))
    slope_row = slope_ref[0, 0:1, :] * LOG2E

    qg = qg_ref[...]
    heads = []
    for h in range(HEADS_PER_GROUP):
        qh = qt_ref[h * HEAD_DIM:(h + 1) * HEAD_DIM, :]
        ms = jnp.mean(qh * qh, axis=0, keepdims=True)
        heads.append((qh * lax.rsqrt(ms + EPS) * qg * (HEAD_DIM ** -0.5 * LOG2E)).astype(BF16))
    pad = lax.broadcasted_iota(jnp.int32, (HEAD_DIM, hq), 0)
    slope_rows = jnp.zeros((HEAD_DIM, hq), F32)
    rest = slope_row
    for part in range(SLOPE_PARTS):
        piece = rest.astype(BF16).astype(F32)
        slope_rows = jnp.where(pad == part, piece, slope_rows)
        rest = rest - piece
    qst = jnp.concatenate([jnp.concatenate(heads, axis=1), slope_rows.astype(BF16)], axis=0)

    def cmp_branch(n_tiles):
        def run():
            scores = []
            for jt in range(n_tiles):
                rows = pl.ds(jt * CMP_TILE, CMP_TILE)
                s = _dot(kc_s[rows, :], qst[0:HEAD_DIM]) + amat_s[rows, :]
                visible = cvis_s[...] <= qs - CMP_STRIDE * CMP_TILE * jt
                scores.append(jnp.where(visible, s, NEG_INF))
            m = functools.reduce(jnp.maximum, [jnp.max(s, axis=0, keepdims=True) for s in scores])
            shift = jnp.where(m > 0.5 * NEG_INF, m, -NEG_INF)
            exps = [jnp.exp2(s - shift) for s in scores]
            l = functools.reduce(lambda x, y: x + y, [jnp.sum(e, axis=0, keepdims=True) for e in exps])
            r = jnp.where(l > 0.0, 1.0 / l, 0.0)
            acc = jnp.zeros((HEAD_DIM, hq), F32)
            for jt, e in enumerate(exps):
                p = e * r
                imp = p[:, 0:tq]
                for h in range(1, HEADS_PER_GROUP):
                    imp = imp + p[:, h * tq:(h + 1) * tq]
                for k in range(tq // LANES):
                    imp_s[k, pl.ds(jt * CMP_TILE, CMP_TILE), :] = imp[:, k * LANES:(k + 1) * LANES]
                acc = acc + _dot(vct_s[:, pl.ds(jt * CMP_TILE, CMP_TILE)], p.astype(BF16))
            return acc
        return run

    n_cmp_live = (qs + tq - CMP_STRIDE) // CMP_STRIDE
    cmp_tiles = n_cmp // CMP_TILE
    if cmp_tiles >= 2:
        o_cmp = lax.cond(n_cmp_live <= (cmp_tiles // 2) * CMP_TILE, cmp_branch(cmp_tiles // 2), cmp_branch(cmp_tiles))
    else:
        o_cmp = cmp_branch(cmp_tiles)()

    per_block = SEL_BLOCK // CMP_STRIDE
    chunk = [jnp.concatenate([imp_s[k, pl.ds(c, n_sel, stride=per_block), :] for k in range(tq // LANES)], axis=1)
             for c in range(per_block)]
    imp_sel = chunk[0]
    for c in range(1, per_block):
        imp_sel = imp_sel + chunk[c]
    blk = lax.broadcasted_iota(jnp.int32, (n_sel, tq), 0)
    imp_sel = imp_sel + jnp.where(blk == 0, 0.0, pltpu.roll(chunk[per_block - 1], 1, 0))
    cur = jnp.right_shift(t_row[:, 0:tq], SEL_BLOCK.bit_length() - 1)
    forced = (blk == 0) | (blk == cur) | (blk == cur - 1)
    valid = blk <= cur
    blk_f = blk.astype(F32)

    def pick_round(_, carry):
        work, taken = carry
        top = jnp.max(work, axis=0, keepdims=True)
        first = jnp.min(jnp.where(work == top, blk_f, float(n_sel)), axis=0, keepdims=True)
        pick = blk_f == first
        return jnp.where(pick, REMOVED, work), jnp.where(pick, 1.0, taken)

    _, taken = lax.fori_loop(
        0, max(k_top - 3, 0), pick_round,
        (jnp.where(valid & jnp.logical_not(forced), imp_sel, NEG_INF), jnp.where(forced & valid, 1.0, 0.0)),
        unroll=True)
    chosen = taken > 0.5
    bias = jnp.where(chosen, 0.0, NEG_INF)
    bias_s[...] = jnp.concatenate([bias] * HEADS_PER_GROUP, axis=1)

    blocks_per_tile = KEY_TILE // SEL_BLOCK
    hit = jnp.where(chosen & (blk < blocks_per_tile * i), 1.0, 0.0)
    blk_hit = jnp.max(hit, axis=1, keepdims=True)
    weight = jnp.left_shift(1, lax.broadcasted_iota(jnp.int32, (n_sel, 1), 0) & (WORD_BITS - 1)).astype(F32)
    packed = blk_hit * weight
    for w in range(n_sel // WORD_BITS):
        word = jnp.sum(packed[w * WORD_BITS:(w + 1) * WORD_BITS], axis=0, keepdims=True).astype(jnp.int32)
        words_s[w] = word[0, 0]

    tiles_per_word = WORD_BITS // blocks_per_tile
    count_s[0] = 0
    for w in range(n_sel // WORD_BITS):
        @pl.when(words_s[w] != 0)
        def _append_tiles_of_word():
            word = words_s[w]
            count = count_s[0]
            for k in range(tiles_per_word):
                bits = jnp.right_shift(word, blocks_per_tile * k) & ((1 << blocks_per_tile) - 1)
                tiles_s[count] = w * tiles_per_word + k
                count = count + jnp.where(bits != 0, 1, 0)
            count_s[0] = count

    n_active = count_s[0]
    for c in range(SEL_CHAINS):
        tiles_s[n_active + c] = -1

    def tile_scores(kts, k_refs, masks):
        k0s = [pl.multiple_of(kt * KEY_TILE, KEY_TILE) for kt in kts]
        scores = []
        for k0, k_s, mask in zip(k0s, k_refs, masks):
            s = _dot(k_s[pl.ds(k0, KEY_TILE), :], qst)
            scores.append(s if mask is None else s + cmat_s[mask])
        return scores

    def tiles_finish(kts, scores, vt_refs, biases_per_tile, ms, accs):
        k0s = [pl.multiple_of(kt * KEY_TILE, KEY_TILE) for kt in kts]
        new_ms, alphas, probs = [], [], []
        for s, k0, biases, m in zip(scores, k0s, biases_per_tile, ms):
            rv = slope_row * (t_row - k0).astype(F32)
            rows = KEY_TILE // len(biases)
            parts = [s[b * rows:(b + 1) * rows] for b in range(len(biases))]
            m_new = m
            for part, bias_row in zip(parts, biases):
                m_new = jnp.maximum(m_new, jnp.max(part, axis=0, keepdims=True) + (bias_row - rv))
            seen = m_new > 0.5 * NEG_INF
            ps = []
            for part, bias_row in zip(parts, biases):
                shift = jnp.where(seen, m_new + (rv - bias_row), -NEG_INF)
                ps.append(jnp.exp2(part - shift))
            new_ms.append(m_new)
            alphas.append(jnp.exp2(m - m_new))
            probs.append((jnp.concatenate(ps, axis=0) if len(ps) > 1 else ps[0]).astype(BF16))
        new_accs = [alpha * acc + _dot(vt_s[:, pl.ds(k0, KEY_TILE)], p)
                    for alpha, acc, vt_s, k0, p in zip(alphas, accs, vt_refs, k0s, probs)]
        return new_ms, new_accs

    def tiles_update(kts, k_refs, vt_refs, masks, biases_per_tile, ms, accs):
        return tiles_finish(kts, tile_scores(kts, k_refs, masks), vt_refs, biases_per_tile, ms, accs)

    def merge(ms, accs):
        m = functools.reduce(jnp.maximum, ms)
        acc = functools.reduce(lambda x, y: x + y, [jnp.exp2(mc - m) * ac for mc, ac in zip(ms, accs)])
        return acc[0:HEAD_DIM] * (1.0 / acc[HEAD_DIM:HEAD_DIM + 1])

    m_init = jnp.full((1, hq), NEG_INF, F32)
    acc_init = jnp.zeros((VALUE_ROWS, hq), F32)
    dead_row = jnp.full((1, hq), NEG_INF, F32)
    live_row = jnp.zeros((1, hq), F32)

    def block_biases(kt):
        return [bias_s[pl.ds(blocks_per_tile * kt + b, 1), :] for b in range(blocks_per_tile)]

    for c in range(SEL_CHAINS):
        acc_s[c] = acc_init

    def sel_step(j, ms):
        entries = [tiles_s[j * SEL_CHAINS + c] for c in range(SEL_CHAINS)]
        kts = [jnp.maximum(e, 0) for e in entries]
        biases = [[jnp.where(e < 0, dead_row, row) for row in block_biases(kt)] for e, kt in zip(entries, kts)]
        new_ms, new_accs = tiles_update(kts, [ks_s] * SEL_CHAINS, [vst_s] * SEL_CHAINS, [None] * SEL_CHAINS, biases,
                                        ms, [acc_s[c] for c in range(SEL_CHAINS)])
        for c in range(SEL_CHAINS):
            acc_s[c] = new_accs[c]
        return tuple(new_ms)

    ms = lax.fori_loop(0, (n_active + SEL_CHAINS - 1) // SEL_CHAINS, sel_step, (m_init,) * SEL_CHAINS)

    far = WINDOW // KEY_TILE
    kts = [i - d for d in range(far + 1)]
    win_masks = [0 if d == 0 else (1 if d == far else None) for d in range(far + 1)]
    win_biases = [[jnp.where(kt < 0, dead_row, live_row)] for kt in kts]
    n_win = far + 1
    ms_x, accs_x = tiles_update(
        [i] + [jnp.maximum(kt, 0) for kt in kts], [ks_s] + [kw_s] * n_win, [vst_s] + [vwt_s] * n_win,
        [0] + win_masks, [block_biases(i)] + win_biases, [m_init] * (n_win + 1), [acc_init] * (n_win + 1))
    o_sel = merge(list(ms) + ms_x[:1], [acc_s[c] for c in range(SEL_CHAINS)] + accs_x[:1])
    o_win = merge(ms_x[1:], accs_x[1:])

    gl = glt_ref[...]

    def gate_row(branch):
        rows = [jax.nn.sigmoid(gl[branch * HEADS_PER_GROUP + h:branch * HEADS_PER_GROUP + h + 1, :])
                for h in range(HEADS_PER_GROUP)]
        return jnp.concatenate(rows, axis=1)

    ot = gate_row(0) * o_cmp + gate_row(1) * o_sel + gate_row(2) * o_win
    out_ref[...] = jnp.concatenate(
        [ot[:, h * tq:(h + 1) * tq] for h in range(HEADS_PER_GROUP)], axis=0).astype(out_ref.dtype)


def _nsa(qt, cp, kp, vt, glt, slopes, qg, kg, pe, w1, w2k, w2vt, *, batch, seq, tq):
    nq = seq // tq
    n_cmp = seq // CMP_STRIDE
    n_sel = seq // SEL_BLOCK
    hq = HEADS_PER_GROUP * tq
    const2 = lambda b, g, i: (0, 0)
    const3 = lambda b, g, i: (0, 0, 0)
    return pl.pallas_call(
        functools.partial(_nsa_kernel, seq=seq, tq=tq),
        grid=(batch, NSA_KV_GROUPS, nq),
        in_specs=[
            pl.BlockSpec((HEADS_PER_GROUP * HEAD_DIM, tq), lambda b, g, i: (g, b * nq + i)),
            pl.BlockSpec((seq, 2 * HEAD_DIM), lambda b, g, i: (b, g)),
            pl.BlockSpec((seq, 2 * HEAD_DIM), lambda b, g, i: (b, g)),
            pl.BlockSpec((2 * HEAD_DIM, seq), lambda b, g, i: (g, b)),
            pl.BlockSpec((GATE_ROWS, tq), lambda b, g, i: (g, b * nq + i)),
            pl.BlockSpec((1, 8, hq), lambda b, g, i: (g, 0, 0)),
            pl.BlockSpec(qg.shape, const2),
            pl.BlockSpec(kg.shape, const2),
            pl.BlockSpec(pe.shape, const2),
            pl.BlockSpec(w1.shape, const3),
            pl.BlockSpec(w2k.shape, const2),
            pl.BlockSpec(w2vt.shape, const2),
        ],
        out_specs=pl.BlockSpec((HEADS_PER_GROUP * HEAD_DIM, tq), lambda b, g, i: (g, b * nq + i)),
        out_shape=jax.ShapeDtypeStruct((NSA_WIDTH, batch * seq), BF16),
        scratch_shapes=[
            pltpu.VMEM((seq, 2 * HEAD_DIM), BF16),
            pltpu.VMEM((seq, 2 * HEAD_DIM), BF16),
            pltpu.VMEM((n_cmp, HEAD_DIM), BF16),
            pltpu.VMEM((HEAD_DIM, n_cmp), BF16),
            pltpu.VMEM((VALUE_ROWS, seq), BF16),
            pltpu.VMEM((VALUE_ROWS, seq), BF16),
            pltpu.VMEM((n_sel, hq), F32),
            pltpu.VMEM((2, KEY_TILE, hq), F32),
            pltpu.VMEM((SEL_CHAINS, VALUE_ROWS, hq), F32),
            pltpu.SMEM((n_sel // WORD_BITS,), jnp.int32),
            pltpu.SMEM((seq // KEY_TILE + SEL_CHAINS,), jnp.int32),
            pltpu.SMEM((1,), jnp.int32),
            pltpu.VMEM((n_cmp, hq), F32),
            pltpu.VMEM((CMP_TILE, hq), jnp.int32),
            pltpu.VMEM((tq // LANES, n_cmp, LANES), F32),
        ],
        compiler_params=pltpu.CompilerParams(
            dimension_semantics=("arbitrary", "arbitrary", "arbitrary"), vmem_limit_bytes=VMEM_LIMIT_BYTES),
        name="nsa",
    )(qt, cp, kp, vt, glt, slopes, qg, kg, pe, w1, w2k, w2vt)


def _final_kernel(x_ref, ys_ref, ynt_ref, g1_ref, wmg_ref, wps_ref, wpn_ref, wout_ref, g2_ref, wup_ref, wdn_ref,
                  o_ref):
    x = x_ref[...]
    h = _rms_rows(x, g1_ref[...]).astype(BF16)
    mg = _dot(h, wmg_ref[...])
    ps = _dot(ys_ref[...], wps_ref[...])
    pn = _dot_tn(ynt_ref[...], wpn_ref[...])
    merged = jax.nn.sigmoid(mg[:, :D_MODEL]) * ps + jax.nn.sigmoid(mg[:, D_MODEL:]) * pn
    x1 = x + _dot(merged.astype(BF16), wout_ref[...])
    h2 = _rms_rows(x1, g2_ref[...]).astype(BF16)
    a = jnp.maximum(_dot(h2, wup_ref[...]), 0.0)
    o_ref[...] = x1 + _dot((a * a).astype(BF16), wdn_ref[...])


def _final(x2, ys, ynt, g1, wmg, wps, wpn, wout, g2, wup, wdn, *, tm):
    n = x2.shape[0]
    const = lambda i: (0, 0)
    resident = lambda w: pl.BlockSpec(w.shape, const, pipeline_mode=pl.Buffered(1))
    return pl.pallas_call(
        _final_kernel,
        grid=(n // tm,),
        in_specs=[
            pl.BlockSpec((tm, D_MODEL), lambda i: (i, 0)),
            pl.BlockSpec((tm, SSM_WIDTH), lambda i: (i, 0)),
            pl.BlockSpec((NSA_WIDTH, tm), lambda i: (0, i)),
            resident(g1), resident(wmg), resident(wps), resident(wpn), resident(wout),
            resident(g2), resident(wup), resident(wdn),
        ],
        out_specs=pl.BlockSpec((tm, D_MODEL), lambda i: (i, 0)),
        out_shape=jax.ShapeDtypeStruct((n, D_MODEL), F32),
        compiler_params=pltpu.CompilerParams(
            dimension_semantics=("arbitrary",), vmem_limit_bytes=VMEM_LIMIT_BYTES),
        name="final",
    )(x2, ys, ynt, g1, wmg, wps, wpn, wout, g2, wup, wdn)


def _pack_in_proj(w_in):
    o1 = SSM_WIDTH
    o2 = o1 + NSA_WIDTH
    o3 = o2 + KV_WIDTH
    o4 = o3 + 3 * NSA_HEADS
    w_u, w_q, w_kv, w_gl = w_in[:, :o1], w_in[:, o1:o2], w_in[:, o2:o3], w_in[:, o3:o4]
    w_mg = w_in[:, o4:]
    kv = w_kv.reshape(D_MODEL, 3, 2, NSA_KV_GROUPS, HEAD_DIM)
    groups = range(NSA_KV_GROUPS)
    cpack = [jnp.concatenate([kv[:, 0, 0, g], kv[:, 0, 1, g]], axis=1) for g in groups]
    kpack = [jnp.concatenate([kv[:, 1, 0, g], kv[:, 2, 0, g]], axis=1) for g in groups]
    wa = jnp.concatenate([w_u] + cpack + kpack, axis=1)
    vrows = [jnp.concatenate([kv[:, 1, 1, g], kv[:, 2, 1, g]], axis=1) for g in range(NSA_KV_GROUPS)]
    gl = w_gl.reshape(D_MODEL, 3, NSA_KV_GROUPS, HEADS_PER_GROUP)
    pad = jnp.zeros((D_MODEL, GATE_ROWS - 3 * HEADS_PER_GROUP), w_in.dtype)
    glrows = [jnp.concatenate([gl[:, 0, g], gl[:, 1, g], gl[:, 2, g], pad], axis=1) for g in range(NSA_KV_GROUPS)]
    wbt = jnp.concatenate([w_q] + vrows + glrows, axis=1).T
    return wa.astype(BF16), wbt.astype(BF16), w_mg.astype(BF16)


def _pack_compress(pe_k, pe_v, w1k, w1v):
    pe = jnp.concatenate([pe_k, pe_v], axis=1)
    w1k = w1k.reshape(CMP_BLOCK, HEAD_DIM, CMP_HIDDEN)
    w1v = w1v.reshape(CMP_BLOCK, HEAD_DIM, CMP_HIDDEN)
    zero = jnp.zeros_like(w1k)
    w1 = jnp.concatenate([jnp.concatenate([w1k, zero], axis=2), jnp.concatenate([zero, w1v], axis=2)], axis=1)
    return pe, w1.astype(BF16)


def _pack_s5(a_re, a_im, log_dt, b_re, b_im, c_re, c_im, seg):
    dt = jnp.exp(log_dt)[:, None]
    mag = jnp.exp(a_re * dt)
    abar_re = mag * jnp.cos(a_im * dt)
    abar_im = mag * jnp.sin(a_im * dt)
    den = a_re * a_re + a_im * a_im
    nr = abar_re - 1.0
    fr = (nr * a_re + abar_im * a_im) / den
    fi = (abar_im * a_re - nr * a_im) / den
    bbar_re = fr[..., None] * b_re - fi[..., None] * b_im
    bbar_im = fr[..., None] * b_im + fi[..., None] * b_re
    gh = SSM_GROUPS // 2
    eye = jnp.eye(gh, dtype=F32)

    def in_blocks(m):
        return (m.transpose(0, 2, 1)[:, :, None, :] * eye[:, None, :, None]).reshape(gh * SSM_GROUP, gh * SSM_STATE)

    def out_blocks(m):
        return (m.transpose(0, 2, 1)[:, :, None, :] * eye[:, None, :, None]).reshape(gh * SSM_STATE, gh * SSM_GROUP)

    halves = [slice(0, gh), slice(gh, SSM_GROUPS)]
    bblk = jnp.stack([jnp.concatenate([in_blocks(bbar_re[h]), in_blocks(bbar_im[h])], axis=1) for h in halves])
    cblk = jnp.stack([jnp.concatenate([out_blocks(c_re[h]), -out_blocks(c_im[h])], axis=0) for h in halves])

    k = jnp.arange(1, seg + 1, dtype=F32)[:, None, None]
    pmag = jnp.exp(k * (a_re * dt))
    apow_re = (pmag * jnp.cos(k * (a_im * dt))).reshape(seg, SSM_STATES)
    apow_im = (pmag * jnp.sin(k * (a_im * dt))).reshape(seg, SSM_STATES)
    acoef = jnp.zeros((8, SSM_STATES), F32)
    acoef = acoef.at[0].set(abar_re.reshape(-1)).at[1].set(abar_im.reshape(-1))
    acoef = acoef.at[2].set(apow_re[seg - 1]).at[3].set(apow_im[seg - 1])
    return bblk.astype(BF16), cblk.astype(BF16), acoef, apow_re, apow_im


def kernel(x, norm1_g, w_in, ssm_a_re, ssm_a_im, ssm_log_dt, ssm_b_re, ssm_b_im, ssm_c_re, ssm_c_im, ssm_d, ssm_w_glu, ssm_b_glu, cmp_pe_k, cmp_pe_v, cmp_wk1, cmp_wk2, cmp_wv1, cmp_wv2, q_norm_g, k_norm_g, w_proj_ssm, w_proj_nsa, w_out, norm2_g, w_up, w_down):
    batch, seq, d_model = x.shape
    n = batch * seq
    assert d_model == D_MODEL
    assert seq % (CMP_TILE * CMP_STRIDE) == 0 and seq % (WORD_BITS * SEL_BLOCK) == 0 and seq % S5_CHUNK == 0
    assert n % IN_PROJ_ROWS == 0 and n % FINAL_ROWS == 0
    tq = KEY_TILE
    depth = w_in.shape[0]
    head = jnp.arange(1, NSA_HEADS + 1, dtype=F32)
    slopes = jnp.exp2(-8.0 * head / NSA_HEADS).reshape(NSA_KV_GROUPS, 1, HEADS_PER_GROUP, 1)
    slopes = jnp.broadcast_to(slopes, (NSA_KV_GROUPS, 8, HEADS_PER_GROUP, tq)).reshape(NSA_KV_GROUPS, 8, -1)

    x2 = x.reshape(n, D_MODEL)
    for l in range(depth):
        wa, wbt, wmg = _pack_in_proj(w_in[l])
        g1 = norm1_g[l].reshape(1, D_MODEL)
        u, cp, kp, qt, vt, glt = _in_proj(x2, g1, wa, wbt, tm=IN_PROJ_ROWS)

        bblk, cblk, acoef, apow_re, apow_im = _pack_s5(ssm_a_re[l], ssm_a_im[l], ssm_log_dt[l], ssm_b_re[l], ssm_b_im[l],
                                                       ssm_c_re[l], ssm_c_im[l], S5_CHUNK // SCAN_SEGMENTS)
        ys = _s5(u, bblk, cblk, acoef, apow_re, apow_im, ssm_d[l].reshape(1, SSM_WIDTH),
                 ssm_w_glu[l].astype(BF16), ssm_b_glu[l].reshape(1, SSM_WIDTH), batch=batch, seq=seq, tc=S5_CHUNK)

        qg = jnp.broadcast_to(q_norm_g[l].reshape(HEAD_DIM, 1), (HEAD_DIM, tq))
        kg = jnp.zeros((8, HEAD_DIM), F32).at[0:3].set(k_norm_g[l])
        pe, w1 = _pack_compress(cmp_pe_k[l], cmp_pe_v[l], cmp_wk1[l], cmp_wv1[l])
        ynt = _nsa(qt, cp, kp, vt, glt, slopes, qg, kg, pe, w1,
                   cmp_wk2[l].astype(BF16), cmp_wv2[l].T.astype(BF16), batch=batch, seq=seq, tq=tq)

        x2 = _final(x2, ys, ynt, g1, wmg, w_proj_ssm[l].astype(BF16), w_proj_nsa[l].astype(BF16),
                    w_out[l].astype(BF16), norm2_g[l].reshape(1, D_MODEL), w_up[l].astype(BF16),
                    w_down[l].astype(BF16), tm=FINAL_ROWS)
    return x2.reshape(batch, seq, D_MODEL)
```

```python
import functools

import jax
import jax.numpy as jnp
from jax import lax
from jax.experimental import pallas as pl
from jax.experimental.pallas import tpu as pltpu

D_MODEL = 1024
SSM_WIDTH = D_MODEL // 2
SSM_GROUP = 16
SSM_GROUPS = SSM_WIDTH // SSM_GROUP
SSM_STATE = 64
SSM_STATES = SSM_GROUPS * SSM_STATE
NSA_HEADS = 8
NSA_KV_GROUPS = 2
HEADS_PER_GROUP = NSA_HEADS // NSA_KV_GROUPS
HEAD_DIM = 64
NSA_WIDTH = NSA_HEADS * HEAD_DIM
CMP_STRIDE = 16
CMP_BLOCK = 2 * CMP_STRIDE
CMP_HIDDEN = 2 * HEAD_DIM
SEL_BLOCK = 64
SEL_TOP_K = 16
WINDOW = 512
KV_WIDTH = 3 * 2 * NSA_KV_GROUPS * HEAD_DIM
EPS = 1e-6
NEG_INF = -1e30
REMOVED = -3e38

LANES = 128
VMEM_LIMIT_BYTES = 56 * 1024 * 1024

IN_PROJ_ROWS = 1024
S5_CHUNK = 512
FINAL_ROWS = 512

GATE_ROWS = 16

F32 = jnp.float32
BF16 = jnp.bfloat16


def _dot(a, b):
    return jnp.dot(a, b, preferred_element_type=F32)


def _dot_nt(a, b):
    return lax.dot_general(a, b, (((1,), (1,)), ((), ())), preferred_element_type=F32)


def _dot_tn(a, b):
    return lax.dot_general(a, b, (((0,), (0,)), ((), ())), preferred_element_type=F32)


def _rms_rows(x, g):
    ms = jnp.mean(x * x, axis=-1, keepdims=True)
    return x * lax.rsqrt(ms + EPS) * g


def _inproj_kernel(x_ref, g_ref, wa_ref, wbt_ref, u_ref, cp_ref, kp_ref, qt_ref, vt_ref, glt_ref):
    h = _rms_rows(x_ref[...], g_ref[...]).astype(BF16)
    a = _dot(h, wa_ref[...])
    for k in range(SSM_WIDTH // LANES):
        u_ref[k] = a[:, k * LANES:(k + 1) * LANES]
    cp_ref[...] = a[:, SSM_WIDTH:SSM_WIDTH + 4 * HEAD_DIM]
    kp_ref[...] = a[:, SSM_WIDTH + 4 * HEAD_DIM:]
    bt = _dot_nt(wbt_ref[...], h)
    qt_ref[...] = bt[:NSA_WIDTH]
    vt_ref[...] = bt[NSA_WIDTH:NSA_WIDTH + 4 * HEAD_DIM]
    glt_ref[...] = bt[NSA_WIDTH + 4 * HEAD_DIM:]


def _in_proj(x2, g1, wa, wbt, *, tm):
    n = x2.shape[0]
    const = lambda i: (0, 0)
    return pl.pallas_call(
        _inproj_kernel,
        grid=(n // tm,),
        in_specs=[
            pl.BlockSpec((tm, D_MODEL), lambda i: (i, 0)),
            pl.BlockSpec((1, D_MODEL), const),
            pl.BlockSpec(wa.shape, const),
            pl.BlockSpec(wbt.shape, const),
        ],
        out_specs=[
            pl.BlockSpec((SSM_WIDTH // LANES, tm, LANES), lambda i: (0, i, 0)),
            pl.BlockSpec((tm, 4 * HEAD_DIM), lambda i: (i, 0)),
            pl.BlockSpec((tm, 4 * HEAD_DIM), lambda i: (i, 0)),
            pl.BlockSpec((NSA_WIDTH, tm), lambda i: (0, i)),
            pl.BlockSpec((4 * HEAD_DIM, tm), lambda i: (0, i)),
            pl.BlockSpec((NSA_KV_GROUPS * GATE_ROWS, tm), lambda i: (0, i)),
        ],
        out_shape=[
            jax.ShapeDtypeStruct((SSM_WIDTH // LANES, n, LANES), F32),
            jax.ShapeDtypeStruct((n, 4 * HEAD_DIM), F32),
            jax.ShapeDtypeStruct((n, 4 * HEAD_DIM), F32),
            jax.ShapeDtypeStruct((NSA_WIDTH, n), F32),
            jax.ShapeDtypeStruct((4 * HEAD_DIM, n), F32),
            jax.ShapeDtypeStruct((NSA_KV_GROUPS * GATE_ROWS, n), F32),
        ],
        compiler_params=pltpu.CompilerParams(
            dimension_semantics=("arbitrary",), vmem_limit_bytes=VMEM_LIMIT_BYTES),
        name="in_proj",
    )(x2, g1, wa, wbt)


SCAN_SEGMENTS = 8
SCAN_SLAB = 512
STATE_HALF = SSM_STATES // 2
U_SLABS = SSM_WIDTH // LANES


def _s5_kernel(u_ref, bblk_ref, cblk_ref, acoef_ref, apow_re_ref, apow_im_ref, d_ref, wglu_ref, bglu_ref, y_ref,
               up_s, bx_s, carry_s, cseg_s, yp_s, *, tc):
    seg = tc // SCAN_SEGMENTS

    @pl.when(pl.program_id(1) == 0)
    def _reset():
        carry_s[...] = jnp.zeros_like(carry_s)

    for tl in range(seg):
        for k in range(U_SLABS):
            up_s[tl * SCAN_SEGMENTS:(tl + 1) * SCAN_SEGMENTS, k * LANES:(k + 1) * LANES] = (
                u_ref[k, pl.ds(tl, SCAN_SEGMENTS, stride=seg), :])
    up = up_s[...]
    half_w = SSM_WIDTH // 2
    for h in range(2):
        bx_s[:, h * 2 * STATE_HALF:(h + 1) * 2 * STATE_HALF] = _dot(
            up[:, h * half_w:(h + 1) * half_w].astype(BF16), bblk_ref[h])

    for slab in range(SSM_STATES // SCAN_SLAB):
        base = (slab * SCAN_SLAB // STATE_HALF) * 2 * STATE_HALF + (slab * SCAN_SLAB) % STATE_HALF
        re_cols = pl.ds(base, SCAN_SLAB)
        im_cols = pl.ds(base + STATE_HALF, SCAN_SLAB)
        nat = pl.ds(slab * SCAN_SLAB, SCAN_SLAB)
        a_re = acoef_ref[0:1, nat]
        a_im = acoef_ref[1:2, nat]
        al_re = acoef_ref[2:3, nat]
        al_im = acoef_ref[3:4, nat]

        def scan_step(tl, carry):
            xr, xi = carry
            rows = pl.ds(pl.multiple_of(tl * SCAN_SEGMENTS, SCAN_SEGMENTS), SCAN_SEGMENTS)
            nr = a_re * xr - a_im * xi + bx_s[rows, re_cols]
            ni = a_re * xi + a_im * xr + bx_s[rows, im_cols]
            bx_s[rows, re_cols] = nr
            bx_s[rows, im_cols] = ni
            return nr, ni

        zero = jnp.zeros((SCAN_SEGMENTS, SCAN_SLAB), F32)
        end_r, end_i = lax.fori_loop(0, seg, scan_step, (zero, zero), unroll=True)

        cr = carry_s[0:1, re_cols]
        ci = carry_s[0:1, im_cols]
        for r in range(SCAN_SEGMENTS):
            cseg_s[r:r + 1, re_cols] = cr
            cseg_s[r:r + 1, im_cols] = ci
            nr = al_re * cr - al_im * ci + end_r[r:r + 1]
            ni = al_re * ci + al_im * cr + end_i[r:r + 1]
            cr, ci = nr, ni
        carry_s[0:1, re_cols] = cr
        carry_s[0:1, im_cols] = ci

        cr8 = cseg_s[:, re_cols]
        ci8 = cseg_s[:, im_cols]

        def fix_step(tl, _):
            rows = pl.ds(pl.multiple_of(tl * SCAN_SEGMENTS, SCAN_SEGMENTS), SCAN_SEGMENTS)
            pr = apow_re_ref[pl.ds(tl, 1), nat]
            pi = apow_im_ref[pl.ds(tl, 1), nat]
            bx_s[rows, re_cols] = bx_s[rows, re_cols] + (pr * cr8 - pi * ci8)
            bx_s[rows, im_cols] = bx_s[rows, im_cols] + (pr * ci8 + pi * cr8)
            return 0

        lax.fori_loop(0, seg, fix_step, 0, unroll=True)

    y = jnp.concatenate(
        [_dot(bx_s[:, h * 2 * STATE_HALF:(h + 1) * 2 * STATE_HALF].astype(BF16), cblk_ref[h]) for h in range(2)],
        axis=1) + d_ref[...] * up
    yg = jax.nn.gelu(y)
    z = _dot(yg.astype(BF16), wglu_ref[...]) + bglu_ref[...]
    out = yg * jax.nn.sigmoid(z)
    for k in range(U_SLABS):
        yp_s[k] = out[:, k * LANES:(k + 1) * LANES]
    for r in range(SCAN_SEGMENTS):
        for k in range(U_SLABS):
            y_ref[r * seg:(r + 1) * seg, k * LANES:(k + 1) * LANES] = (
                yp_s[k, pl.ds(r, seg, stride=SCAN_SEGMENTS), :].astype(y_ref.dtype))


def _s5(u4, bblk, cblk, acoef, apow_re, apow_im, d_row, wglu, bglu, *, batch, seq, tc):
    nchunk = seq // tc
    const2 = lambda b, c: (0, 0)
    const3 = lambda b, c: (0, 0, 0)
    return pl.pallas_call(
        functools.partial(_s5_kernel, tc=tc),
        grid=(batch, nchunk),
        in_specs=[
            pl.BlockSpec((U_SLABS, tc, LANES), lambda b, c: (0, b * nchunk + c, 0)),
            pl.BlockSpec(bblk.shape, const3),
            pl.BlockSpec(cblk.shape, const3),
            pl.BlockSpec(acoef.shape, const2),
            pl.BlockSpec(apow_re.shape, const2),
            pl.BlockSpec(apow_im.shape, const2),
            pl.BlockSpec(d_row.shape, const2),
            pl.BlockSpec(wglu.shape, const2),
            pl.BlockSpec(bglu.shape, const2),
        ],
        out_specs=pl.BlockSpec((tc, SSM_WIDTH), lambda b, c: (b * nchunk + c, 0)),
        out_shape=jax.ShapeDtypeStruct((batch * seq, SSM_WIDTH), BF16),
        scratch_shapes=[
            pltpu.VMEM((tc, SSM_WIDTH), F32),
            pltpu.VMEM((tc, 2 * SSM_STATES), F32),
            pltpu.VMEM((8, 2 * SSM_STATES), F32),
            pltpu.VMEM((SCAN_SEGMENTS, 2 * SSM_STATES), F32),
            pltpu.VMEM((U_SLABS, tc, LANES), F32),
        ],
        compiler_params=pltpu.CompilerParams(
            dimension_semantics=("arbitrary", "arbitrary"), vmem_limit_bytes=VMEM_LIMIT_BYTES),
        name="s5",
    )(u4, bblk, cblk, acoef, apow_re, apow_im, d_row, wglu, bglu)


KEY_TILE = 256
CMP_TILE = 256
LOG2E = 1.4426950408889634
SLOPE_PARTS = 3
VALUE_ROWS = HEAD_DIM + 16
SEL_CHAINS = 6
WORD_BITS = 16


def _nsa_kernel(qt_ref, cp_ref, kp_ref, vt_ref, glt_ref, slope_ref, qg_ref, kg_ref, pe_ref,
                w1_ref, w2k_ref, w2vt_ref, out_ref,
                ks_s, kw_s, kc_s, vct_s, vst_s, vwt_s, bias_s, cmat_s, acc_s, words_s, tiles_s, count_s, amat_s, cvis_s, imp_s, *, seq, tq):
    assert tq == KEY_TILE
    i = pl.program_id(2)
    n_cmp = seq // CMP_STRIDE
    n_sel = seq // SEL_BLOCK
    k_top = min(SEL_TOP_K, n_sel)
    hq = HEADS_PER_GROUP * tq

    @pl.when(i == 0)
    def _prepare_keys_and_values():
        kg = kg_ref[...]
        offs = (lax.broadcasted_iota(jnp.int32, (seq, HEAD_DIM), 0) & (KEY_TILE - 1)).astype(F32)
        offs = jnp.where(lax.broadcasted_iota(jnp.int32, (seq, HEAD_DIM), 1) < SLOPE_PARTS, offs, 0.0)
        ks_s[...] = jnp.concatenate([_rms_rows(kp_ref[:, 0:HEAD_DIM], kg[1:2]), offs], axis=1).astype(BF16)
        kw_s[...] = jnp.concatenate([_rms_rows(kp_ref[:, HEAD_DIM:2 * HEAD_DIM], kg[2:3]), offs], axis=1).astype(BF16)
        ones = jnp.ones((VALUE_ROWS - HEAD_DIM, seq), F32)
        vst_s[...] = jnp.concatenate([vt_ref[0:HEAD_DIM, :], ones], axis=0).astype(BF16)
        vwt_s[...] = jnp.concatenate([vt_ref[HEAD_DIM:2 * HEAD_DIM, :], ones], axis=0).astype(BF16)

        lo = jnp.zeros((n_cmp, 2 * CMP_HIDDEN), F32)
        hi = jnp.zeros((n_cmp, 2 * CMP_HIDDEN), F32)
        for tt in range(CMP_STRIDE):
            tok = cp_ref[pl.ds(tt, n_cmp, stride=CMP_STRIDE), :]
            lo = lo + _dot((tok + pe_ref[tt:tt + 1, :]).astype(BF16), w1_ref[tt])
            hi = hi + _dot((tok + pe_ref[CMP_STRIDE + tt:CMP_STRIDE + tt + 1, :]).astype(BF16),
                           w1_ref[CMP_STRIDE + tt])
        hidden = jax.nn.gelu(lo + pltpu.roll(hi, n_cmp - 1, 0)).astype(BF16)
        kc_s[...] = _rms_rows(_dot(hidden[:, :CMP_HIDDEN], w2k_ref[...]), kg[0:1]).astype(BF16)
        vct_s[...] = _dot_nt(w2vt_ref[...], hidden[:, CMP_HIDDEN:]).astype(BF16)

        r = lax.broadcasted_iota(jnp.int32, (KEY_TILE, hq), 0)
        a = lax.broadcasted_iota(jnp.int32, (KEY_TILE, hq), 1) & (tq - 1)
        cmat_s[0] = jnp.where(r <= a, 0.0, NEG_INF)
        cmat_s[1] = jnp.where(r > a, 0.0, NEG_INF)

        jrow = lax.broadcasted_iota(jnp.int32, (n_cmp, hq), 0)
        amat_s[...] = (slope_ref[0, 0:1, :] * LOG2E) * (jrow * CMP_STRIDE + (CMP_BLOCK - 1)).astype(F32)
        cvis_s[...] = (r * CMP_STRIDE + (CMP_BLOCK - 1)) - a
        imp_s[...] = jnp.zeros_like(imp_s)

    qs = i * tq
    lane = lax.broadcasted_iota(jnp.int32, (1, hq), 1)
    t_row = qs + (lane & (tq - 1))
    slope_row = slope_ref[0, 0:1, :] * LOG2E

    qg = qg_ref[...]
    heads = []
    for h in range(HEADS_PER_GROUP):
        qh = qt_ref[h * HEAD_DIM:(h + 1) * HEAD_DIM, :]
        ms = jnp.mean(qh * qh, axis=0, keepdims=True)
        heads.append((qh * lax.rsqrt(ms + EPS) * qg * (HEAD_DIM ** -0.5 * LOG2E)).astype(BF16))
    pad = lax.broadcasted_iota(jnp.int32, (HEAD_DIM, hq), 0)
    slope_rows = jnp.zeros((HEAD_DIM, hq), F32)
    rest = slope_row
    for part in range(SLOPE_PARTS):
        piece = rest.astype(BF16).astype(F32)
        slope_rows = jnp.where(pad == part, piece, slope_rows)
        rest = rest - piece
    qst = jnp.concatenate([jnp.concatenate(heads, axis=1), slope_rows.astype(BF16)], axis=0)

    def cmp_branch(n_tiles):
        def run():
            scores = []
            for jt in range(n_tiles):
                rows = pl.ds(jt * CMP_TILE, CMP_TILE)
                s = _dot(kc_s[rows, :], qst[0:HEAD_DIM]) + amat_s[rows, :]
                visible = cvis_s[...] <= qs - CMP_STRIDE * CMP_TILE * jt
                scores.append(jnp.where(visible, s, NEG_INF))
            m = functools.reduce(jnp.maximum, [jnp.max(s, axis=0, keepdims=True) for s in scores])
            shift = jnp.where(m > 0.5 * NEG_INF, m, -NEG_INF)
            exps = [jnp.exp2(s - shift) for s in scores]
            l = functools.reduce(lambda x, y: x + y, [jnp.sum(e, axis=0, keepdims=True) for e in exps])
            r = jnp.where(l > 0.0, 1.0 / l, 0.0)
            acc = jnp.zeros((HEAD_DIM, hq), F32)
            for jt, e in enumerate(exps):
                p = e * r
                imp = p[:, 0:tq]
                for h in range(1, HEADS_PER_GROUP):
                    imp = imp + p[:, h * tq:(h + 1) * tq]
                for k in range(tq // LANES):
                    imp_s[k, pl.ds(jt * CMP_TILE, CMP_TILE), :] = imp[:, k * LANES:(k + 1) * LANES]
                acc = acc + _dot(vct_s[:, pl.ds(jt * CMP_TILE, CMP_TILE)], p.astype(BF16))
            return acc
        return run

    n_cmp_live = (qs + tq - CMP_STRIDE) // CMP_STRIDE
    cmp_tiles = n_cmp // CMP_TILE
    if cmp_tiles >= 2:
        o_cmp = lax.cond(n_cmp_live <= (cmp_tiles // 2) * CMP_TILE, cmp_branch(cmp_tiles // 2), cmp_branch(cmp_tiles))
    else:
        o_cmp = cmp_branch(cmp_tiles)()

    per_block = SEL_BLOCK // CMP_STRIDE
    chunk = [jnp.concatenate([imp_s[k, pl.ds(c, n_sel, stride=per_block), :] for k in range(tq // LANES)], axis=1)
             for c in range(per_block)]
    imp_sel = chunk[0]
    for c in range(1, per_block):
        imp_sel = imp_sel + chunk[c]
    blk = lax.broadcasted_iota(jnp.int32, (n_sel, tq), 0)
    imp_sel = imp_sel + jnp.where(blk == 0, 0.0, pltpu.roll(chunk[per_block - 1], 1, 0))
    cur = jnp.right_shift(t_row[:, 0:tq], SEL_BLOCK.bit_length() - 1)
    forced = (blk == 0) | (blk == cur) | (blk == cur - 1)
    valid = blk <= cur
    blk_f = blk.astype(F32)

    def pick_round(_, carry):
        work, taken = carry
        top = jnp.max(work, axis=0, keepdims=True)
        first = jnp.min(jnp.where(work == top, blk_f, float(n_sel)), axis=0, keepdims=True)
        pick = blk_f == first
        return jnp.where(pick, REMOVED, work), jnp.where(pick, 1.0, taken)

    _, taken = lax.fori_loop(
        0, max(k_top - 3, 0), pick_round,
        (jnp.where(valid & jnp.logical_not(forced), imp_sel, NEG_INF), jnp.where(forced & valid, 1.0, 0.0)),
        unroll=True)
    chosen = taken > 0.5
    bias = jnp.where(chosen, 0.0, NEG_INF)
    bias_s[...] = jnp.concatenate([bias] * HEADS_PER_GROUP, axis=1)

    blocks_per_tile = KEY_TILE // SEL_BLOCK
    hit = jnp.where(chosen & (blk < blocks_per_tile * i), 1.0, 0.0)
    blk_hit = jnp.max(hit, axis=1, keepdims=True)
    weight = jnp.left_shift(1, lax.broadcasted_iota(jnp.int32, (n_sel, 1), 0) & (WORD_BITS - 1)).astype(F32)
    packed = blk_hit * weight
    for w in range(n_sel // WORD_BITS):
        word = jnp.sum(packed[w * WORD_BITS:(w + 1) * WORD_BITS], axis=0, keepdims=True).astype(jnp.int32)
        words_s[w] = word[0, 0]

    tiles_per_word = WORD_BITS // blocks_per_tile
    count_s[0] = 0
    for w in range(n_sel // WORD_BITS):
        @pl.when(words_s[w] != 0)
        def _append_tiles_of_word():
            word = words_s[w]
            count = count_s[0]
            for k in range(tiles_per_word):
                bits = jnp.right_shift(word, blocks_per_tile * k) & ((1 << blocks_per_tile) - 1)
                tiles_s[count] = w * tiles_per_word + k
                count = count + jnp.where(bits != 0, 1, 0)
            count_s[0] = count

    n_active = count_s[0]
    for c in range(SEL_CHAINS):
        tiles_s[n_active + c] = -1

    def tile_scores(kts, k_refs, masks):
        k0s = [pl.multiple_of(kt * KEY_TILE, KEY_TILE) for kt in kts]
        scores = []
        for k0, k_s, mask in zip(k0s, k_refs, masks):
            s = _dot(k_s[pl.ds(k0, KEY_TILE), :], qst)
            scores.append(s if mask is None else s + cmat_s[mask])
        return scores

    def tiles_finish(kts, scores, vt_refs, biases_per_tile, ms, accs):
        k0s = [pl.multiple_of(kt * KEY_TILE, KEY_TILE) for kt in kts]
        new_ms, alphas, probs = [], [], []
        for s, k0, biases, m in zip(scores, k0s, biases_per_tile, ms):
            rv = slope_row * (t_row - k0).astype(F32)
            rows = KEY_TILE // len(biases)
            parts = [s[b * rows:(b + 1) * rows] for b in range(len(biases))]
            m_new = m
            for part, bias_row in zip(parts, biases):
                m_new = jnp.maximum(m_new, jnp.max(part, axis=0, keepdims=True) + (bias_row - rv))
            seen = m_new > 0.5 * NEG_INF
            ps = []
            for part, bias_row in zip(parts, biases):
                shift = jnp.where(seen, m_new + (rv - bias_row), -NEG_INF)
                ps.append(jnp.exp2(part - shift))
            new_ms.append(m_new)
            alphas.append(jnp.exp2(m - m_new))
            probs.append((jnp.concatenate(ps, axis=0) if len(ps) > 1 else ps[0]).astype(BF16))
        new_accs = [alpha * acc + _dot(vt_s[:, pl.ds(k0, KEY_TILE)], p)
                    for alpha, acc, vt_s, k0, p in zip(alphas, accs, vt_refs, k0s, probs)]
        return new_ms, new_accs

    def tiles_update(kts, k_refs, vt_refs, masks, biases_per_tile, ms, accs):
        return tiles_finish(kts, tile_scores(kts, k_refs, masks), vt_refs, biases_per_tile, ms, accs)

    def merge(ms, accs):
        m = functools.reduce(jnp.maximum, ms)
        acc = functools.reduce(lambda x, y: x + y, [jnp.exp2(mc - m) * ac for mc, ac in zip(ms, accs)])
        return acc[0:HEAD_DIM] * (1.0 / acc[HEAD_DIM:HEAD_DIM + 1])

    m_init = jnp.full((1, hq), NEG_INF, F32)
    acc_init = jnp.zeros((VALUE_ROWS, hq), F32)
    dead_row = jnp.full((1, hq), NEG_INF, F32)
    live_row = jnp.zeros((1, hq), F32)

    def block_biases(kt):
        return [bias_s[pl.ds(blocks_per_tile * kt + b, 1), :] for b in range(blocks_per_tile)]

    for c in range(SEL_CHAINS):
        acc_s[c] = acc_init

    def sel_step(j, ms):
        entries = [tiles_s[j * SEL_CHAINS + c] for c in range(SEL_CHAINS)]
        kts = [jnp.maximum(e, 0) for e in entries]
        biases = [[jnp.where(e < 0, dead_row, row) for row in block_biases(kt)] for e, kt in zip(entries, kts)]
        new_ms, new_accs = tiles_update(kts, [ks_s] * SEL_CHAINS, [vst_s] * SEL_CHAINS, [None] * SEL_CHAINS, biases,
                                        ms, [acc_s[c] for c in range(SEL_CHAINS)])
        for c in range(SEL_CHAINS):
            acc_s[c] = new_accs[c]
        return tuple(new_ms)

    ms = lax.fori_loop(0, (n_active + SEL_CHAINS - 1) // SEL_CHAINS, sel_step, (m_init,) * SEL_CHAINS)

    far = WINDOW // KEY_TILE
    kts = [i - d for d in range(far + 1)]
    win_masks = [0 if d == 0 else (1 if d == far else None) for d in range(far + 1)]
    win_biases = [[jnp.where(kt < 0, dead_row, live_row)] for kt in kts]
    n_win = far + 1
    ms_x, accs_x = tiles_update(
        [i] + [jnp.maximum(kt, 0) for kt in kts], [ks_s] + [kw_s] * n_win, [vst_s] + [vwt_s] * n_win,
        [0] + win_masks, [block_biases(i)] + win_biases, [m_init] * (n_win + 1), [acc_init] * (n_win + 1))
    o_sel = merge(list(ms) + ms_x[:1], [acc_s[c] for c in range(SEL_CHAINS)] + accs_x[:1])
    o_win = merge(ms_x[1:], accs_x[1:])

    gl = glt_ref[...]

    def gate_row(branch):
        rows = [jax.nn.sigmoid(gl[branch * HEADS_PER_GROUP + h:branch * HEADS_PER_GROUP + h + 1, :])
                for h in range(HEADS_PER_GROUP)]
        return jnp.concatenate(rows, axis=1)

    ot = gate_row(0) * o_cmp + gate_row(1) * o_sel + gate_row(2) * o_win
    out_ref[...] = jnp.concatenate(
        [ot[:, h * tq:(h + 1) * tq] for h in range(HEADS_PER_GROUP)], axis=0).astype(out_ref.dtype)


def _nsa(qt, cp, kp, vt, glt, slopes, qg, kg, pe, w1, w2k, w2vt, *, batch, seq, tq):
    nq = seq // tq
    n_cmp = seq // CMP_STRIDE
    n_sel = seq // SEL_BLOCK
    hq = HEADS_PER_GROUP * tq
    const2 = lambda b, g, i: (0, 0)
    const3 = lambda b, g, i: (0, 0, 0)
    return pl.pallas_call(
        functools.partial(_nsa_kernel, seq=seq, tq=tq),
        grid=(batch, NSA_KV_GROUPS, nq),
        in_specs=[
            pl.BlockSpec((HEADS_PER_GROUP * HEAD_DIM, tq), lambda b, g, i: (g, b * nq + i)),
            pl.BlockSpec((seq, 2 * HEAD_DIM), lambda b, g, i: (b, g)),
            pl.BlockSpec((seq, 2 * HEAD_DIM), lambda b, g, i: (b, g)),
            pl.BlockSpec((2 * HEAD_DIM, seq), lambda b, g, i: (g, b)),
            pl.BlockSpec((GATE_ROWS, tq), lambda b, g, i: (g, b * nq + i)),
            pl.BlockSpec((1, 8, hq), lambda b, g, i: (g, 0, 0)),
            pl.BlockSpec(qg.shape, const2),
            pl.BlockSpec(kg.shape, const2),
            pl.BlockSpec(pe.shape, const2),
            pl.BlockSpec(w1.shape, const3),
            pl.BlockSpec(w2k.shape, const2),
            pl.BlockSpec(w2vt.shape, const2),
        ],
        out_specs=pl.BlockSpec((HEADS_PER_GROUP * HEAD_DIM, tq), lambda b, g, i: (g, b * nq + i)),
        out_shape=jax.ShapeDtypeStruct((NSA_WIDTH, batch * seq), BF16),
        scratch_shapes=[
            pltpu.VMEM((seq, 2 * HEAD_DIM), BF16),
            pltpu.VMEM((seq, 2 * HEAD_DIM), BF16),
            pltpu.VMEM((n_cmp, HEAD_DIM), BF16),
            pltpu.VMEM((HEAD_DIM, n_cmp), BF16),
            pltpu.VMEM((VALUE_ROWS, seq), BF16),
            pltpu.VMEM((VALUE_ROWS, seq), BF16),
            pltpu.VMEM((n_sel, hq), F32),
            pltpu.VMEM((2, KEY_TILE, hq), F32),
            pltpu.VMEM((SEL_CHAINS, VALUE_ROWS, hq), F32),
            pltpu.SMEM((n_sel // WORD_BITS,), jnp.int32),
            pltpu.SMEM((seq // KEY_TILE + SEL_CHAINS,), jnp.int32),
            pltpu.SMEM((1,), jnp.int32),
            pltpu.VMEM((n_cmp, hq), F32),
            pltpu.VMEM((CMP_TILE, hq), jnp.int32),
            pltpu.VMEM((tq // LANES, n_cmp, LANES), F32),
        ],
        compiler_params=pltpu.CompilerParams(
            dimension_semantics=("arbitrary", "arbitrary", "arbitrary"), vmem_limit_bytes=VMEM_LIMIT_BYTES),
        name="nsa",
    )(qt, cp, kp, vt, glt, slopes, qg, kg, pe, w1, w2k, w2vt)


def _final_kernel(x_ref, ys_ref, ynt_ref, g1_ref, wmg_ref, wps_ref, wpn_ref, wout_ref, g2_ref, wup_ref, wdn_ref,
                  o_ref):
    x = x_ref[...]
    h = _rms_rows(x, g1_ref[...]).astype(BF16)
    mg = _dot(h, wmg_ref[...])
    ps = _dot(ys_ref[...], wps_ref[...])
    pn = _dot_tn(ynt_ref[...], wpn_ref[...])
    merged = jax.nn.sigmoid(mg[:, :D_MODEL]) * ps + jax.nn.sigmoid(mg[:, D_MODEL:]) * pn
    x1 = x + _dot(merged.astype(BF16), wout_ref[...])
    h2 = _rms_rows(x1, g2_ref[...]).astype(BF16)
    a = jnp.maximum(_dot(h2, wup_ref[...]), 0.0)
    o_ref[...] = x1 + _dot((a * a).astype(BF16), wdn_ref[...])


def _final(x2, ys, ynt, g1, wmg, wps, wpn, wout, g2, wup, wdn, *, tm):
    n = x2.shape[0]
    const = lambda i: (0, 0)
    resident = lambda w: pl.BlockSpec(w.shape, const, pipeline_mode=pl.Buffered(1))
    return pl.pallas_call(
        _final_kernel,
        grid=(n // tm,),
        in_specs=[
            pl.BlockSpec((tm, D_MODEL), lambda i: (i, 0)),
            pl.BlockSpec((tm, SSM_WIDTH), lambda i: (i, 0)),
            pl.BlockSpec((NSA_WIDTH, tm), lambda i: (0, i)),
            resident(g1), resident(wmg), resident(wps), resident(wpn), resident(wout),
            resident(g2), resident(wup), resident(wdn),
        ],
        out_specs=pl.BlockSpec((tm, D_MODEL), lambda i: (i, 0)),
        out_shape=jax.ShapeDtypeStruct((n, D_MODEL), F32),
        compiler_params=pltpu.CompilerParams(
            dimension_semantics=("arbitrary",), vmem_limit_bytes=VMEM_LIMIT_BYTES),
        name="final",
    )(x2, ys, ynt, g1, wmg, wps, wpn, wout, g2, wup, wdn)


def _pack_in_proj(w_in):
    o1 = SSM_WIDTH
    o2 = o1 + NSA_WIDTH
    o3 = o2 + KV_WIDTH
    o4 = o3 + 3 * NSA_HEADS
    w_u, w_q, w_kv, w_gl = w_in[:, :o1], w_in[:, o1:o2], w_in[:, o2:o3], w_in[:, o3:o4]
    w_mg = w_in[:, o4:]
    kv = w_kv.reshape(D_MODEL, 3, 2, NSA_KV_GROUPS, HEAD_DIM)
    groups = range(NSA_KV_GROUPS)
    cpack = [jnp.concatenate([kv[:, 0, 0, g], kv[:, 0, 1, g]], axis=1) for g in groups]
    kpack = [jnp.concatenate([kv[:, 1, 0, g], kv[:, 2, 0, g]], axis=1) for g in groups]
    wa = jnp.concatenate([w_u] + cpack + kpack, axis=1)
    vrows = [jnp.concatenate([kv[:, 1, 1, g], kv[:, 2, 1, g]], axis=1) for g in range(NSA_KV_GROUPS)]
    gl = w_gl.reshape(D_MODEL, 3, NSA_KV_GROUPS, HEADS_PER_GROUP)
    pad = jnp.zeros((D_MODEL, GATE_ROWS - 3 * HEADS_PER_GROUP), w_in.dtype)
    glrows = [jnp.concatenate([gl[:, 0, g], gl[:, 1, g], gl[:, 2, g], pad], axis=1) for g in range(NSA_KV_GROUPS)]
    wbt = jnp.concatenate([w_q] + vrows + glrows, axis=1).T
    return wa.astype(BF16), wbt.astype(BF16), w_mg.astype(BF16)


def _pack_compress(pe_k, pe_v, w1k, w1v):
    pe = jnp.concatenate([pe_k, pe_v], axis=1)
    w1k = w1k.reshape(CMP_BLOCK, HEAD_DIM, CMP_HIDDEN)
    w1v = w1v.reshape(CMP_BLOCK, HEAD_DIM, CMP_HIDDEN)
    zero = jnp.zeros_like(w1k)
    w1 = jnp.concatenate([jnp.concatenate([w1k, zero], axis=2), jnp.concatenate([zero, w1v], axis=2)], axis=1)
    return pe, w1.astype(BF16)


def _pack_s5(a_re, a_im, log_dt, b_re, b_im, c_re, c_im, seg):
    dt = jnp.exp(log_dt)[:, None]
    mag = jnp.exp(a_re * dt)
    abar_re = mag * jnp.cos(a_im * dt)
    abar_im = mag * jnp.sin(a_im * dt)
    den = a_re * a_re + a_im * a_im
    nr = abar_re - 1.0
    fr = (nr * a_re + abar_im * a_im) / den
    fi = (abar_im * a_re - nr * a_im) / den
    bbar_re = fr[..., None] * b_re - fi[..., None] * b_im
    bbar_im = fr[..., None] * b_im + fi[..., None] * b_re
    gh = SSM_GROUPS // 2
    eye = jnp.eye(gh, dtype=F32)

    def in_blocks(m):
        return (m.transpose(0, 2, 1)[:, :, None, :] * eye[:, None, :, None]).reshape(gh * SSM_GROUP, gh * SSM_STATE)

    def out_blocks(m):
        return (m.transpose(0, 2, 1)[:, :, None, :] * eye[:, None, :, None]).reshape(gh * SSM_STATE, gh * SSM_GROUP)

    halves = [slice(0, gh), slice(gh, SSM_GROUPS)]
    bblk = jnp.stack([jnp.concatenate([in_blocks(bbar_re[h]), in_blocks(bbar_im[h])], axis=1) for h in halves])
    cblk = jnp.stack([jnp.concatenate([out_blocks(c_re[h]), -out_blocks(c_im[h])], axis=0) for h in halves])

    k = jnp.arange(1, seg + 1, dtype=F32)[:, None, None]
    pmag = jnp.exp(k * (a_re * dt))
    apow_re = (pmag * jnp.cos(k * (a_im * dt))).reshape(seg, SSM_STATES)
    apow_im = (pmag * jnp.sin(k * (a_im * dt))).reshape(seg, SSM_STATES)
    acoef = jnp.zeros((8, SSM_STATES), F32)
    acoef = acoef.at[0].set(abar_re.reshape(-1)).at[1].set(abar_im.reshape(-1))
    acoef = acoef.at[2].set(apow_re[seg - 1]).at[3].set(apow_im[seg - 1])
    return bblk.astype(BF16), cblk.astype(BF16), acoef, apow_re, apow_im


def kernel(x, norm1_g, w_in, ssm_a_re, ssm_a_im, ssm_log_dt, ssm_b_re, ssm_b_im, ssm_c_re, ssm_c_im, ssm_d, ssm_w_glu, ssm_b_glu, cmp_pe_k, cmp_pe_v, cmp_wk1, cmp_wk2, cmp_wv1, cmp_wv2, q_norm_g, k_norm_g, w_proj_ssm, w_proj_nsa, w_out, norm2_g, w_up, w_down):
    batch, seq, d_model = x.shape
    n = batch * seq
    assert d_model == D_MODEL
    assert seq % (CMP_TILE * CMP_STRIDE) == 0 and seq % (WORD_BITS * SEL_BLOCK) == 0 and seq % S5_CHUNK == 0
    assert n % IN_PROJ_ROWS == 0 and n % FINAL_ROWS == 0
    tq = KEY_TILE
    depth = w_in.shape[0]
    head = jnp.arange(1, NSA_HEADS + 1, dtype=F32)
    slopes = jnp.exp2(-8.0 * head / NSA_HEADS).reshape(NSA_KV_GROUPS, 1, HEADS_PER_GROUP, 1)
    slopes = jnp.broadcast_to(slopes, (NSA_KV_GROUPS, 8, HEADS_PER_GROUP, tq)).reshape(NSA_KV_GROUPS, 8, -1)

    x2 = x.reshape(n, D_MODEL)
    for l in range(depth):
        wa, wbt, wmg = _pack_in_proj(w_in[l])
        g1 = norm1_g[l].reshape(1, D_MODEL)
        u, cp, kp, qt, vt, glt = _in_proj(x2, g1, wa, wbt, tm=IN_PROJ_ROWS)

        bblk, cblk, acoef, apow_re, apow_im = _pack_s5(ssm_a_re[l], ssm_a_im[l], ssm_log_dt[l], ssm_b_re[l], ssm_b_im[l],
                                                       ssm_c_re[l], ssm_c_im[l], S5_CHUNK // SCAN_SEGMENTS)
        ys = _s5(u, bblk, cblk, acoef, apow_re, apow_im, ssm_d[l].reshape(1, SSM_WIDTH),
                 ssm_w_glu[l].astype(BF16), ssm_b_glu[l].reshape(1, SSM_WIDTH), batch=batch, seq=seq, tc=S5_CHUNK)

        qg = jnp.broadcast_to(q_norm_g[l].reshape(HEAD_DIM, 1), (HEAD_DIM, tq))
        kg = jnp.zeros((8, HEAD_DIM), F32).at[0:3].set(k_norm_g[l])
        pe, w1 = _pack_compress(cmp_pe_k[l], cmp_pe_v[l], cmp_wk1[l], cmp_wv1[l])
        ynt = _nsa(qt, cp, kp, vt, glt, slopes, qg, kg, pe, w1,
                   cmp_wk2[l].astype(BF16), cmp_wv2[l].T.astype(BF16), batch=batch, seq=seq, tq=tq)

        x2 = _final(x2, ys, ynt, g1, wmg, w_proj_ssm[l].astype(BF16), w_proj_nsa[l].astype(BF16),
                    w_out[l].astype(BF16), norm2_g[l].reshape(1, D_MODEL), w_up[l].astype(BF16),
                    w_down[l].astype(BF16), tm=FINAL_ROWS)
    return x2.reshape(batch, seq, D_MODEL)
```

```python
import functools

import jax
import jax.numpy as jnp
from jax import lax
from jax.experimental import pallas as pl
from jax.experimental.pallas import tpu as pltpu

D_MODEL = 1024
SSM_WIDTH = D_MODEL // 2
SSM_GROUP = 16
SSM_GROUPS = SSM_WIDTH // SSM_GROUP
SSM_STATE = 64
SSM_STATES = SSM_GROUPS * SSM_STATE
NSA_HEADS = 8
NSA_KV_GROUPS = 2
HEADS_PER_GROUP = NSA_HEADS // NSA_KV_GROUPS
HEAD_DIM = 64
NSA_WIDTH = NSA_HEADS * HEAD_DIM
CMP_STRIDE = 16
CMP_BLOCK = 2 * CMP_STRIDE
CMP_HIDDEN = 2 * HEAD_DIM
SEL_BLOCK = 64
SEL_TOP_K = 16
WINDOW = 512
KV_WIDTH = 3 * 2 * NSA_KV_GROUPS * HEAD_DIM
EPS = 1e-6
NEG_INF = -1e30
REMOVED = -3e38

LANES = 128
VMEM_LIMIT_BYTES = 56 * 1024 * 1024

IN_PROJ_ROWS = 1024
S5_CHUNK = 512
FINAL_ROWS = 512

GATE_ROWS = 16

F32 = jnp.float32
BF16 = jnp.bfloat16


def _dot(a, b):
    return jnp.dot(a, b, preferred_element_type=F32)


def _dot_nt(a, b):
    return lax.dot_general(a, b, (((1,), (1,)), ((), ())), preferred_element_type=F32)


def _dot_tn(a, b):
    return lax.dot_general(a, b, (((0,), (0,)), ((), ())), preferred_element_type=F32)


def _rms_rows(x, g):
    ms = jnp.mean(x * x, axis=-1, keepdims=True)
    return x * lax.rsqrt(ms + EPS) * g


def _inproj_kernel(x_ref, g_ref, wa_ref, wbt_ref, u_ref, cp_ref, kp_ref, qt_ref, vt_ref, glt_ref):
    h = _rms_rows(x_ref[...], g_ref[...]).astype(BF16)
    a = _dot(h, wa_ref[...])
    for k in range(SSM_WIDTH // LANES):
        u_ref[k] = a[:, k * LANES:(k + 1) * LANES]
    cp_ref[...] = a[:, SSM_WIDTH:SSM_WIDTH + 4 * HEAD_DIM]
    kp_ref[...] = a[:, SSM_WIDTH + 4 * HEAD_DIM:]
    bt = _dot_nt(wbt_ref[...], h)
    qt_ref[...] = bt[:NSA_WIDTH]
    vt_ref[...] = bt[NSA_WIDTH:NSA_WIDTH + 4 * HEAD_DIM]
    glt_ref[...] = bt[NSA_WIDTH + 4 * HEAD_DIM:]


def _in_proj(x2, g1, wa, wbt, *, tm):
    n = x2.shape[0]
    const = lambda i: (0, 0)
    return pl.pallas_call(
        _inproj_kernel,
        grid=(n // tm,),
        in_specs=[
            pl.BlockSpec((tm, D_MODEL), lambda i: (i, 0)),
            pl.BlockSpec((1, D_MODEL), const),
            pl.BlockSpec(wa.shape, const),
            pl.BlockSpec(wbt.shape, const),
        ],
        out_specs=[
            pl.BlockSpec((SSM_WIDTH // LANES, tm, LANES), lambda i: (0, i, 0)),
            pl.BlockSpec((tm, 4 * HEAD_DIM), lambda i: (i, 0)),
            pl.BlockSpec((tm, 4 * HEAD_DIM), lambda i: (i, 0)),
            pl.BlockSpec((NSA_WIDTH, tm), lambda i: (0, i)),
            pl.BlockSpec((4 * HEAD_DIM, tm), lambda i: (0, i)),
            pl.BlockSpec((NSA_KV_GROUPS * GATE_ROWS, tm), lambda i: (0, i)),
        ],
        out_shape=[
            jax.ShapeDtypeStruct((SSM_WIDTH // LANES, n, LANES), F32),
            jax.ShapeDtypeStruct((n, 4 * HEAD_DIM), F32),
            jax.ShapeDtypeStruct((n, 4 * HEAD_DIM), F32),
            jax.ShapeDtypeStruct((NSA_WIDTH, n), F32),
            jax.ShapeDtypeStruct((4 * HEAD_DIM, n), F32),
            jax.ShapeDtypeStruct((NSA_KV_GROUPS * GATE_ROWS, n), F32),
        ],
        compiler_params=pltpu.CompilerParams(
            dimension_semantics=("arbitrary",), vmem_limit_bytes=VMEM_LIMIT_BYTES),
        name="in_proj",
    )(x2, g1, wa, wbt)


SCAN_SEGMENTS = 8
SCAN_SLAB = 512
STATE_HALF = SSM_STATES // 2
U_SLABS = SSM_WIDTH // LANES


def _s5_kernel(u_ref, bblk_ref, cblk_ref, acoef_ref, apow_re_ref, apow_im_ref, d_ref, wglu_ref, bglu_ref, y_ref,
               up_s, bx_s, carry_s, cseg_s, yp_s, *, tc):
    seg = tc // SCAN_SEGMENTS

    @pl.when(pl.program_id(1) == 0)
    def _reset():
        carry_s[...] = jnp.zeros_like(carry_s)

    for tl in range(seg):
        for k in range(U_SLABS):
            up_s[tl * SCAN_SEGMENTS:(tl + 1) * SCAN_SEGMENTS, k * LANES:(k + 1) * LANES] = (
                u_ref[k, pl.ds(tl, SCAN_SEGMENTS, stride=seg), :])
    up = up_s[...]
    half_w = SSM_WIDTH // 2
    for h in range(2):
        bx_s[:, h * 2 * STATE_HALF:(h + 1) * 2 * STATE_HALF] = _dot(
            up[:, h * half_w:(h + 1) * half_w].astype(BF16), bblk_ref[h])

    for slab in range(SSM_STATES // SCAN_SLAB):
        base = (slab * SCAN_SLAB // STATE_HALF) * 2 * STATE_HALF + (slab * SCAN_SLAB) % STATE_HALF
        re_cols = pl.ds(base, SCAN_SLAB)
        im_cols = pl.ds(base + STATE_HALF, SCAN_SLAB)
        nat = pl.ds(slab * SCAN_SLAB, SCAN_SLAB)
        a_re = acoef_ref[0:1, nat]
        a_im = acoef_ref[1:2, nat]
        al_re = acoef_ref[2:3, nat]
        al_im = acoef_ref[3:4, nat]

        def scan_step(tl, carry):
            xr, xi = carry
            rows = pl.ds(pl.multiple_of(tl * SCAN_SEGMENTS, SCAN_SEGMENTS), SCAN_SEGMENTS)
            nr = a_re * xr - a_im * xi + bx_s[rows, re_cols]
            ni = a_re * xi + a_im * xr + bx_s[rows, im_cols]
            bx_s[rows, re_cols] = nr
            bx_s[rows, im_cols] = ni
            return nr, ni

        zero = jnp.zeros((SCAN_SEGMENTS, SCAN_SLAB), F32)
        end_r, end_i = lax.fori_loop(0, seg, scan_step, (zero, zero), unroll=True)

        cr = carry_s[0:1, re_cols]
        ci = carry_s[0:1, im_cols]
        for r in range(SCAN_SEGMENTS):
            cseg_s[r:r + 1, re_cols] = cr
            cseg_s[r:r + 1, im_cols] = ci
            nr = al_re * cr - al_im * ci + end_r[r:r + 1]
            ni = al_re * ci + al_im * cr + end_i[r:r + 1]
            cr, ci = nr, ni
        carry_s[0:1, re_cols] = cr
        carry_s[0:1, im_cols] = ci

        cr8 = cseg_s[:, re_cols]
        ci8 = cseg_s[:, im_cols]

        def fix_step(tl, _):
            rows = pl.ds(pl.multiple_of(tl * SCAN_SEGMENTS, SCAN_SEGMENTS), SCAN_SEGMENTS)
            pr = apow_re_ref[pl.ds(tl, 1), nat]
            pi = apow_im_ref[pl.ds(tl, 1), nat]
            bx_s[rows, re_cols] = bx_s[rows, re_cols] + (pr * cr8 - pi * ci8)
            bx_s[rows, im_cols] = bx_s[rows, im_cols] + (pr * ci8 + pi * cr8)
            return 0

        lax.fori_loop(0, seg, fix_step, 0, unroll=True)

    y = jnp.concatenate(
        [_dot(bx_s[:, h * 2 * STATE_HALF:(h + 1) * 2 * STATE_HALF].astype(BF16), cblk_ref[h]) for h in range(2)],
        axis=1) + d_ref[...] * up
    yg = jax.nn.gelu(y)
    z = _dot(yg.astype(BF16), wglu_ref[...]) + bglu_ref[...]
    out = yg * jax.nn.sigmoid(z)
    for k in range(U_SLABS):
        yp_s[k] = out[:, k * LANES:(k + 1) * LANES]
    for r in range(SCAN_SEGMENTS):
        for k in range(U_SLABS):
            y_ref[r * seg:(r + 1) * seg, k * LANES:(k + 1) * LANES] = (
                yp_s[k, pl.ds(r, seg, stride=SCAN_SEGMENTS), :].astype(y_ref.dtype))


def _s5(u4, bblk, cblk, acoef, apow_re, apow_im, d_row, wglu, bglu, *, batch, seq, tc):
    nchunk = seq // tc
    const2 = lambda b, c: (0, 0)
    const3 = lambda b, c: (0, 0, 0)
    return pl.pallas_call(
        functools.partial(_s5_kernel, tc=tc),
        grid=(batch, nchunk),
        in_specs=[
            pl.BlockSpec((U_SLABS, tc, LANES), lambda b, c: (0, b * nchunk + c, 0)),
            pl.BlockSpec(bblk.shape, const3),
            pl.BlockSpec(cblk.shape, const3),
            pl.BlockSpec(acoef.shape, const2),
            pl.BlockSpec(apow_re.shape, const2),
            pl.BlockSpec(apow_im.shape, const2),
            pl.BlockSpec(d_row.shape, const2),
            pl.BlockSpec(wglu.shape, const2),
            pl.BlockSpec(bglu.shape, const2),
        ],
        out_specs=pl.BlockSpec((tc, SSM_WIDTH), lambda b, c: (b * nchunk + c, 0)),
        out_shape=jax.ShapeDtypeStruct((batch * seq, SSM_WIDTH), BF16),
        scratch_shapes=[
            pltpu.VMEM((tc, SSM_WIDTH), F32),
            pltpu.VMEM((tc, 2 * SSM_STATES), F32),
            pltpu.VMEM((8, 2 * SSM_STATES), F32),
            pltpu.VMEM((SCAN_SEGMENTS, 2 * SSM_STATES), F32),
            pltpu.VMEM((U_SLABS, tc, LANES), F32),
        ],
        compiler_params=pltpu.CompilerParams(
            dimension_semantics=("arbitrary", "arbitrary"), vmem_limit_bytes=VMEM_LIMIT_BYTES),
        name="s5",
    )(u4, bblk, cblk, acoef, apow_re, apow_im, d_row, wglu, bglu)


KEY_TILE = 256
CMP_TILE = 256
LOG2E = 1.4426950408889634
SLOPE_PARTS = 3
VALUE_ROWS = HEAD_DIM + 16
SEL_CHAINS = 6
WORD_BITS = 16


def _nsa_kernel(qt_ref, cp_ref, kp_ref, vt_ref, glt_ref, slope_ref, qg_ref, kg_ref, pe_ref,
                w1_ref, w2k_ref, w2vt_ref, out_ref,
                ks_s, kw_s, kc_s, vct_s, vst_s, vwt_s, bias_s, cmat_s, acc_s, words_s, tiles_s, count_s, amat_s, cvis_s, imp_s, *, seq, tq):
    assert tq == KEY_TILE
    i = pl.program_id(2)
    n_cmp = seq // CMP_STRIDE
    n_sel = seq // SEL_BLOCK
    k_top = min(SEL_TOP_K, n_sel)
    hq = HEADS_PER_GROUP * tq

    @pl.when(i == 0)
    def _prepare_keys_and_values():
        kg = kg_ref[...]
        offs = (lax.broadcasted_iota(jnp.int32, (seq, HEAD_DIM), 0) & (KEY_TILE - 1)).astype(F32)
        offs = jnp.where(lax.broadcasted_iota(jnp.int32, (seq, HEAD_DIM), 1) < SLOPE_PARTS, offs, 0.0)
        ks_s[...] = jnp.concatenate([_rms_rows(kp_ref[:, 0:HEAD_DIM], kg[1:2]), offs], axis=1).astype(BF16)
        kw_s[...] = jnp.concatenate([_rms_rows(kp_ref[:, HEAD_DIM:2 * HEAD_DIM], kg[2:3]), offs], axis=1).astype(BF16)
        ones = jnp.ones((VALUE_ROWS - HEAD_DIM, seq), F32)
        vst_s[...] = jnp.concatenate([vt_ref[0:HEAD_DIM, :], ones], axis=0).astype(BF16)
        vwt_s[...] = jnp.concatenate([vt_ref[HEAD_DIM:2 * HEAD_DIM, :], ones], axis=0).astype(BF16)

        lo = jnp.zeros((n_cmp, 2 * CMP_HIDDEN), F32)
        hi = jnp.zeros((n_cmp, 2 * CMP_HIDDEN), F32)
        for tt in range(CMP_STRIDE):
            tok = cp_ref[pl.ds(tt, n_cmp, stride=CMP_STRIDE), :]
            lo = lo + _dot((tok + pe_ref[tt:tt + 1, :]).astype(BF16), w1_ref[tt])
            hi = hi + _dot((tok + pe_ref[CMP_STRIDE + tt:CMP_STRIDE + tt + 1, :]).astype(BF16),
                           w1_ref[CMP_STRIDE + tt])
        hidden = jax.nn.gelu(lo + pltpu.roll(hi, n_cmp - 1, 0)).astype(BF16)
        kc_s[...] = _rms_rows(_dot(hidden[:, :CMP_HIDDEN], w2k_ref[...]), kg[0:1]).astype(BF16)
        vct_s[...] = _dot_nt(w2vt_ref[...], hidden[:, CMP_HIDDEN:]).astype(BF16)

        r = lax.broadcasted_iota(jnp.int32, (KEY_TILE, hq), 0)
        a = lax.broadcasted_iota(jnp.int32, (KEY_TILE, hq), 1) & (tq - 1)
        cmat_s[0] = jnp.where(r <= a, 0.0, NEG_INF)
        cmat_s[1] = jnp.where(r > a, 0.0, NEG_INF)

        jrow = lax.broadcasted_iota(jnp.int32, (n_cmp, hq), 0)
        amat_s[...] = (slope_ref[0, 0:1, :] * LOG2E) * (jrow * CMP_STRIDE + (CMP_BLOCK - 1)).astype(F32)
        cvis_s[...] = (r * CMP_STRIDE + (CMP_BLOCK - 1)) - a
        imp_s[...] = jnp.zeros_like(imp_s)

    qs = i * tq
    lane = lax.broadcasted_iota(jnp.int32, (1, hq), 1)
    t_row = qs + (lane & (tq - 1))
    slope_row = slope_ref[0, 0:1, :] * LOG2E

    qg = qg_ref[...]
    heads = []
    for h in range(HEADS_PER_GROUP):
        qh = qt_ref[h * HEAD_DIM:(h + 1) * HEAD_DIM, :]
        ms = jnp.mean(qh * qh, axis=0, keepdims=True)
        heads.append((qh * lax.rsqrt(ms + EPS) * qg * (HEAD_DIM ** -0.5 * LOG2E)).astype(BF16))
    pad = lax.broadcasted_iota(jnp.int32, (HEAD_DIM, hq), 0)
    slope_rows = jnp.zeros((HEAD_DIM, hq), F32)
    rest = slope_row
    for part in range(SLOPE_PARTS):
        piece = rest.astype(BF16).astype(F32)
        slope_rows = jnp.where(pad == part, piece, slope_rows)
        rest = rest - piece
    qst = jnp.concatenate([jnp.concatenate(heads, axis=1), slope_rows.astype(BF16)], axis=0)

    def cmp_branch(n_tiles):
        def run():
            scores = []
            for jt in range(n_tiles):
                rows = pl.ds(jt * CMP_TILE, CMP_TILE)
                s = _dot(kc_s[rows, :], qst[0:HEAD_DIM]) + amat_s[rows, :]
                visible = cvis_s[...] <= qs - CMP_STRIDE * CMP_TILE * jt
                scores.append(jnp.where(visible, s, NEG_INF))
            m = functools.reduce(jnp.maximum, [jnp.max(s, axis=0, keepdims=True) for s in scores])
            shift = jnp.where(m > 0.5 * NEG_INF, m, -NEG_INF)
            exps = [jnp.exp2(s - shift) for s in scores]
            l = functools.reduce(lambda x, y: x + y, [jnp.sum(e, axis=0, keepdims=True) for e in exps])
            r = jnp.where(l > 0.0, 1.0 / l, 0.0)
            acc = jnp.zeros((HEAD_DIM, hq), F32)
            for jt, e in enumerate(exps):
                p = e * r
                imp = p[:, 0:tq]
                for h in range(1, HEADS_PER_GROUP):
                    imp = imp + p[:, h * tq:(h + 1) * tq]
                for k in range(tq // LANES):
                    imp_s[k, pl.ds(jt * CMP_TILE, CMP_TILE), :] = imp[:, k * LANES:(k + 1) * LANES]
                acc = acc + _dot(vct_s[:, pl.ds(jt * CMP_TILE, CMP_TILE)], p.astype(BF16))
            return acc
        return run

    n_cmp_live = (qs + tq - CMP_STRIDE) // CMP_STRIDE
    cmp_tiles = n_cmp // CMP_TILE
    if cmp_tiles >= 2:
        o_cmp = lax.cond(n_cmp_live <= (cmp_tiles // 2) * CMP_TILE, cmp_branch(cmp_tiles // 2), cmp_branch(cmp_tiles))
    else:
        o_cmp = cmp_branch(cmp_tiles)()

    per_block = SEL_BLOCK // CMP_STRIDE
    chunk = [jnp.concatenate([imp_s[k, pl.ds(c, n_sel, stride=per_block), :] for k in range(tq // LANES)], axis=1)
             for c in range(per_block)]
    imp_sel = chunk[0]
    for c in range(1, per_block):
        imp_sel = imp_sel + chunk[c]
    blk = lax.broadcasted_iota(jnp.int32, (n_sel, tq), 0)
    imp_sel = imp_sel + jnp.where(blk == 0, 0.0, pltpu.roll(chunk[per_block - 1], 1, 0))
    cur = jnp.right_shift(t_row[:, 0:tq], SEL_BLOCK.bit_length() - 1)
    forced = (blk == 0) | (blk == cur) | (blk == cur - 1)
    valid = blk <= cur
    blk_f = blk.astype(F32)

    def pick_round(_, carry):
        work, taken = carry
        top = jnp.max(work, axis=0, keepdims=True)
        first = jnp.min(jnp.where(work == top, blk_f, float(n_sel)), axis=0, keepdims=True)
        pick = blk_f == first
        return jnp.where(pick, REMOVED, work), jnp.where(pick, 1.0, taken)

    _, taken = lax.fori_loop(
        0, max(k_top - 3, 0), pick_round,
        (jnp.where(valid & jnp.logical_not(forced), imp_sel, NEG_INF), jnp.where(forced & valid, 1.0, 0.0)),
        unroll=True)
    chosen = taken > 0.5
    bias = jnp.where(chosen, 0.0, NEG_INF)
    bias_s[...] = jnp.concatenate([bias] * HEADS_PER_GROUP, axis=1)

    blocks_per_tile = KEY_TILE // SEL_BLOCK
    hit = jnp.where(chosen & (blk < blocks_per_tile * i), 1.0, 0.0)
    blk_hit = jnp.max(hit, axis=1, keepdims=True)
    weight = jnp.left_shift(1, lax.broadcasted_iota(jnp.int32, (n_sel, 1), 0) & (WORD_BITS - 1)).astype(F32)
    packed = blk_hit * weight
    for w in range(n_sel // WORD_BITS):
        word = jnp.sum(packed[w * WORD_BITS:(w + 1) * WORD_BITS], axis=0, keepdims=True).astype(jnp.int32)
        words_s[w] = word[0, 0]

    tiles_per_word = WORD_BITS // blocks_per_tile
    count_s[0] = 0
    for w in range(n_sel // WORD_BITS):
        @pl.when(words_s[w] != 0)
        def _append_tiles_of_word():
            word = words_s[w]
            count = count_s[0]
            for k in range(tiles_per_word):
                bits = jnp.right_shift(word, blocks_per_tile * k) & ((1 << blocks_per_tile) - 1)
                tiles_s[count] = w * tiles_per_word + k
                count = count + jnp.where(bits != 0, 1, 0)
            count_s[0] = count

    n_active = count_s[0]
    for c in range(SEL_CHAINS):
        tiles_s[n_active + c] = -1

    def tile_scores(kts, k_refs, masks):
        k0s = [pl.multiple_of(kt * KEY_TILE, KEY_TILE) for kt in kts]
        scores = []
        for k0, k_s, mask in zip(k0s, k_refs, masks):
            s = _dot(k_s[pl.ds(k0, KEY_TILE), :], qst)
            scores.append(s if mask is None else s + cmat_s[mask])
        return scores

    def tiles_finish(kts, scores, vt_refs, biases_per_tile, ms, accs):
        k0s = [pl.multiple_of(kt * KEY_TILE, KEY_TILE) for kt in kts]
        new_ms, alphas, probs = [], [], []
        for s, k0, biases, m in zip(scores, k0s, biases_per_tile, ms):
            rv = slope_row * (t_row - k0).astype(F32)
            rows = KEY_TILE // len(biases)
            parts = [s[b * rows:(b + 1) * rows] for b in range(len(biases))]
            m_new = m
            for part, bias_row in zip(parts, biases):
                m_new = jnp.maximum(m_new, jnp.max(part, axis=0, keepdims=True) + (bias_row - rv))
            seen = m_new > 0.5 * NEG_INF
            ps = []
            for part, bias_row in zip(parts, biases):
                shift = jnp.where(seen, m_new + (rv - bias_row), -NEG_INF)
                ps.append(jnp.exp2(part - shift))
            new_ms.append(m_new)
            alphas.append(jnp.exp2(m - m_new))
            probs.append((jnp.concatenate(ps, axis=0) if len(ps) > 1 else ps[0]).astype(BF16))
        new_accs = [alpha * acc + _dot(vt_s[:, pl.ds(k0, KEY_TILE)], p)
                    for alpha, acc, vt_s, k0, p in zip(alphas, accs, vt_refs, k0s, probs)]
        return new_ms, new_accs

    def tiles_update(kts, k_refs, vt_refs, masks, biases_per_tile, ms, accs):
        return tiles_finish(kts, tile_scores(kts, k_refs, masks), vt_refs, biases_per_tile, ms, accs)

    def merge(ms, accs):
        m = functools.reduce(jnp.maximum, ms)
        acc = functools.reduce(lambda x, y: x + y, [jnp.exp2(mc - m) * ac for mc, ac in zip(ms, accs)])
        return acc[0:HEAD_DIM] * (1.0 / acc[HEAD_DIM:HEAD_DIM + 1])

    m_init = jnp.full((1, hq), NEG_INF, F32)
    acc_init = jnp.zeros((VALUE_ROWS, hq), F32)
    dead_row = jnp.full((1, hq), NEG_INF, F32)
    live_row = jnp.zeros((1, hq), F32)

    def block_biases(kt):
        return [bias_s[pl.ds(blocks_per_tile * kt + b, 1), :] for b in range(blocks_per_tile)]

    for c in range(SEL_CHAINS):
        acc_s[c] = acc_init

    def sel_sweep(chains):
        def sel_step(j, ms):
            entries = [tiles_s[j * chains + c] for c in range(chains)]
            kts = [jnp.maximum(e, 0) for e in entries]
            biases = [[jnp.where(e < 0, dead_row, row) for row in block_biases(kt)] for e, kt in zip(entries, kts)]
            new_ms, new_accs = tiles_update(kts, [ks_s] * chains, [vst_s] * chains, [None] * chains, biases,
                                            ms, [acc_s[c] for c in range(chains)])
            for c in range(chains):
                acc_s[c] = new_accs[c]
            return tuple(new_ms)

        def run():
            ms = lax.fori_loop(0, (n_active + chains - 1) // chains, sel_step, (m_init,) * chains)
            return tuple(ms) + (m_init,) * (SEL_CHAINS - chains)
        return run

    ms = lax.cond(n_active < SEL_CHAINS, sel_sweep(SEL_CHAINS - 1), sel_sweep(SEL_CHAINS))

    far = WINDOW // KEY_TILE
    kts = [i - d for d in range(far + 1)]
    win_masks = [0 if d == 0 else (1 if d == far else None) for d in range(far + 1)]
    win_biases = [[jnp.where(kt < 0, dead_row, live_row)] for kt in kts]
    n_win = far + 1
    ms_x, accs_x = tiles_update(
        [i] + [jnp.maximum(kt, 0) for kt in kts], [ks_s] + [kw_s] * n_win, [vst_s] + [vwt_s] * n_win,
        [0] + win_masks, [block_biases(i)] + win_biases, [m_init] * (n_win + 1), [acc_init] * (n_win + 1))
    o_sel = merge(list(ms) + ms_x[:1], [acc_s[c] for c in range(SEL_CHAINS)] + accs_x[:1])
    o_win = merge(ms_x[1:], accs_x[1:])

    gl = glt_ref[...]

    def gate_row(branch):
        rows = [jax.nn.sigmoid(gl[branch * HEADS_PER_GROUP + h:branch * HEADS_PER_GROUP + h + 1, :])
                for h in range(HEADS_PER_GROUP)]
        return jnp.concatenate(rows, axis=1)

    ot = gate_row(0) * o_cmp + gate_row(1) * o_sel + gate_row(2) * o_win
    out_ref[...] = jnp.concatenate(
        [ot[:, h * tq:(h + 1) * tq] for h in range(HEADS_PER_GROUP)], axis=0).astype(out_ref.dtype)


def _nsa(qt, cp, kp, vt, glt, slopes, qg, kg, pe, w1, w2k, w2vt, *, batch, seq, tq):
    nq = seq // tq
    n_cmp = seq // CMP_STRIDE
    n_sel = seq // SEL_BLOCK
    hq = HEADS_PER_GROUP * tq
    const2 = lambda b, g, i: (0, 0)
    const3 = lambda b, g, i: (0, 0, 0)
    return pl.pallas_call(
        functools.partial(_nsa_kernel, seq=seq, tq=tq),
        grid=(batch, NSA_KV_GROUPS, nq),
        in_specs=[
            pl.BlockSpec((HEADS_PER_GROUP * HEAD_DIM, tq), lambda b, g, i: (g, b * nq + i)),
            pl.BlockSpec((seq, 2 * HEAD_DIM), lambda b, g, i: (b, g)),
            pl.BlockSpec((seq, 2 * HEAD_DIM), lambda b, g, i: (b, g)),
            pl.BlockSpec((2 * HEAD_DIM, seq), lambda b, g, i: (g, b)),
            pl.BlockSpec((GATE_ROWS, tq), lambda b, g, i: (g, b * nq + i)),
            pl.BlockSpec((1, 8, hq), lambda b, g, i: (g, 0, 0)),
            pl.BlockSpec(qg.shape, const2),
            pl.BlockSpec(kg.shape, const2),
            pl.BlockSpec(pe.shape, const2),
            pl.BlockSpec(w1.shape, const3),
            pl.BlockSpec(w2k.shape, const2),
            pl.BlockSpec(w2vt.shape, const2),
        ],
        out_specs=pl.BlockSpec((HEADS_PER_GROUP * HEAD_DIM, tq), lambda b, g, i: (g, b * nq + i)),
        out_shape=jax.ShapeDtypeStruct((NSA_WIDTH, batch * seq), BF16),
        scratch_shapes=[
            pltpu.VMEM((seq, 2 * HEAD_DIM), BF16),
            pltpu.VMEM((seq, 2 * HEAD_DIM), BF16),
            pltpu.VMEM((n_cmp, HEAD_DIM), BF16),
            pltpu.VMEM((HEAD_DIM, n_cmp), BF16),
            pltpu.VMEM((VALUE_ROWS, seq), BF16),
            pltpu.VMEM((VALUE_ROWS, seq), BF16),
            pltpu.VMEM((n_sel, hq), F32),
            pltpu.VMEM((2, KEY_TILE, hq), F32),
            pltpu.VMEM((SEL_CHAINS, VALUE_ROWS, hq), F32),
            pltpu.SMEM((n_sel // WORD_BITS,), jnp.int32),
            pltpu.SMEM((seq // KEY_TILE + SEL_CHAINS,), jnp.int32),
            pltpu.SMEM((1,), jnp.int32),
            pltpu.VMEM((n_cmp, hq), F32),
            pltpu.VMEM((CMP_TILE, hq), jnp.int32),
            pltpu.VMEM((tq // LANES, n_cmp, LANES), F32),
        ],
        compiler_params=pltpu.CompilerParams(
            dimension_semantics=("arbitrary", "arbitrary", "arbitrary"), vmem_limit_bytes=VMEM_LIMIT_BYTES),
        name="nsa",
    )(qt, cp, kp, vt, glt, slopes, qg, kg, pe, w1, w2k, w2vt)


def _final_kernel(x_ref, ys_ref, ynt_ref, g1_ref, wmg_ref, wps_ref, wpn_ref, wout_ref, g2_ref, wup_ref, wdn_ref,
                  o_ref):
    x = x_ref[...]
    h = _rms_rows(x, g1_ref[...]).astype(BF16)
    mg = _dot(h, wmg_ref[...])
    ps = _dot(ys_ref[...], wps_ref[...])
    pn = _dot_tn(ynt_ref[...], wpn_ref[...])
    merged = jax.nn.sigmoid(mg[:, :D_MODEL]) * ps + jax.nn.sigmoid(mg[:, D_MODEL:]) * pn
    x1 = x + _dot(merged.astype(BF16), wout_ref[...])
    h2 = _rms_rows(x1, g2_ref[...]).astype(BF16)
    a = jnp.maximum(_dot(h2, wup_ref[...]), 0.0)
    o_ref[...] = x1 + _dot((a * a).astype(BF16), wdn_ref[...])


def _final(x2, ys, ynt, g1, wmg, wps, wpn, wout, g2, wup, wdn, *, tm):
    n = x2.shape[0]
    const = lambda i: (0, 0)
    resident = lambda w: pl.BlockSpec(w.shape, const, pipeline_mode=pl.Buffered(1))
    return pl.pallas_call(
        _final_kernel,
        grid=(n // tm,),
        in_specs=[
            pl.BlockSpec((tm, D_MODEL), lambda i: (i, 0)),
            pl.BlockSpec((tm, SSM_WIDTH), lambda i: (i, 0)),
            pl.BlockSpec((NSA_WIDTH, tm), lambda i: (0, i)),
            resident(g1), resident(wmg), resident(wps), resident(wpn), resident(wout),
            resident(g2), resident(wup), resident(wdn),
        ],
        out_specs=pl.BlockSpec((tm, D_MODEL), lambda i: (i, 0)),
        out_shape=jax.ShapeDtypeStruct((n, D_MODEL), F32),
        compiler_params=pltpu.CompilerParams(
            dimension_semantics=("arbitrary",), vmem_limit_bytes=VMEM_LIMIT_BYTES),
        name="final",
    )(x2, ys, ynt, g1, wmg, wps, wpn, wout, g2, wup, wdn)


def _pack_in_proj(w_in):
    o1 = SSM_WIDTH
    o2 = o1 + NSA_WIDTH
    o3 = o2 + KV_WIDTH
    o4 = o3 + 3 * NSA_HEADS
    w_u, w_q, w_kv, w_gl = w_in[:, :o1], w_in[:, o1:o2], w_in[:, o2:o3], w_in[:, o3:o4]
    w_mg = w_in[:, o4:]
    kv = w_kv.reshape(D_MODEL, 3, 2, NSA_KV_GROUPS, HEAD_DIM)
    groups = range(NSA_KV_GROUPS)
    cpack = [jnp.concatenate([kv[:, 0, 0, g], kv[:, 0, 1, g]], axis=1) for g in groups]
    kpack = [jnp.concatenate([kv[:, 1, 0, g], kv[:, 2, 0, g]], axis=1) for g in groups]
    wa = jnp.concatenate([w_u] + cpack + kpack, axis=1)
    vrows = [jnp.concatenate([kv[:, 1, 1, g], kv[:, 2, 1, g]], axis=1) for g in range(NSA_KV_GROUPS)]
    gl = w_gl.reshape(D_MODEL, 3, NSA_KV_GROUPS, HEADS_PER_GROUP)
    pad = jnp.zeros((D_MODEL, GATE_ROWS - 3 * HEADS_PER_GROUP), w_in.dtype)
    glrows = [jnp.concatenate([gl[:, 0, g], gl[:, 1, g], gl[:, 2, g], pad], axis=1) for g in range(NSA_KV_GROUPS)]
    wbt = jnp.concatenate([w_q] + vrows + glrows, axis=1).T
    return wa.astype(BF16), wbt.astype(BF16), w_mg.astype(BF16)


def _pack_compress(pe_k, pe_v, w1k, w1v):
    pe = jnp.concatenate([pe_k, pe_v], axis=1)
    w1k = w1k.reshape(CMP_BLOCK, HEAD_DIM, CMP_HIDDEN)
    w1v = w1v.reshape(CMP_BLOCK, HEAD_DIM, CMP_HIDDEN)
    zero = jnp.zeros_like(w1k)
    w1 = jnp.concatenate([jnp.concatenate([w1k, zero], axis=2), jnp.concatenate([zero, w1v], axis=2)], axis=1)
    return pe, w1.astype(BF16)


def _pack_s5(a_re, a_im, log_dt, b_re, b_im, c_re, c_im, seg):
    dt = jnp.exp(log_dt)[:, None]
    mag = jnp.exp(a_re * dt)
    abar_re = mag * jnp.cos(a_im * dt)
    abar_im = mag * jnp.sin(a_im * dt)
    den = a_re * a_re + a_im * a_im
    nr = abar_re - 1.0
    fr = (nr * a_re + abar_im * a_im) / den
    fi = (abar_im * a_re - nr * a_im) / den
    bbar_re = fr[..., None] * b_re - fi[..., None] * b_im
    bbar_im = fr[..., None] * b_im + fi[..., None] * b_re
    gh = SSM_GROUPS // 2
    eye = jnp.eye(gh, dtype=F32)

    def in_blocks(m):
        return (m.transpose(0, 2, 1)[:, :, None, :] * eye[:, None, :, None]).reshape(gh * SSM_GROUP, gh * SSM_STATE)

    def out_blocks(m):
        return (m.transpose(0, 2, 1)[:, :, None, :] * eye[:, None, :, None]).reshape(gh * SSM_STATE, gh * SSM_GROUP)

    halves = [slice(0, gh), slice(gh, SSM_GROUPS)]
    bblk = jnp.stack([jnp.concatenate([in_blocks(bbar_re[h]), in_blocks(bbar_im[h])], axis=1) for h in halves])
    cblk = jnp.stack([jnp.concatenate([out_blocks(c_re[h]), -out_blocks(c_im[h])], axis=0) for h in halves])

    k = jnp.arange(1, seg + 1, dtype=F32)[:, None, None]
    pmag = jnp.exp(k * (a_re * dt))
    apow_re = (pmag * jnp.cos(k * (a_im * dt))).reshape(seg, SSM_STATES)
    apow_im = (pmag * jnp.sin(k * (a_im * dt))).reshape(seg, SSM_STATES)
    acoef = jnp.zeros((8, SSM_STATES), F32)
    acoef = acoef.at[0].set(abar_re.reshape(-1)).at[1].set(abar_im.reshape(-1))
    acoef = acoef.at[2].set(apow_re[seg - 1]).at[3].set(apow_im[seg - 1])
    return bblk.astype(BF16), cblk.astype(BF16), acoef, apow_re, apow_im


def kernel(x, norm1_g, w_in, ssm_a_re, ssm_a_im, ssm_log_dt, ssm_b_re, ssm_b_im, ssm_c_re, ssm_c_im, ssm_d, ssm_w_glu, ssm_b_glu, cmp_pe_k, cmp_pe_v, cmp_wk1, cmp_wk2, cmp_wv1, cmp_wv2, q_norm_g, k_norm_g, w_proj_ssm, w_proj_nsa, w_out, norm2_g, w_up, w_down):
    batch, seq, d_model = x.shape
    n = batch * seq
    assert d_model == D_MODEL
    assert seq % (CMP_TILE * CMP_STRIDE) == 0 and seq % (WORD_BITS * SEL_BLOCK) == 0 and seq % S5_CHUNK == 0
    assert n % IN_PROJ_ROWS == 0 and n % FINAL_ROWS == 0
    tq = KEY_TILE
    depth = w_in.shape[0]
    head = jnp.arange(1, NSA_HEADS + 1, dtype=F32)
    slopes = jnp.exp2(-8.0 * head / NSA_HEADS).reshape(NSA_KV_GROUPS, 1, HEADS_PER_GROUP, 1)
    slopes = jnp.broadcast_to(slopes, (NSA_KV_GROUPS, 8, HEADS_PER_GROUP, tq)).reshape(NSA_KV_GROUPS, 8, -1)

    x2 = x.reshape(n, D_MODEL)
    for l in range(depth):
        wa, wbt, wmg = _pack_in_proj(w_in[l])
        g1 = norm1_g[l].reshape(1, D_MODEL)
        u, cp, kp, qt, vt, glt = _in_proj(x2, g1, wa, wbt, tm=IN_PROJ_ROWS)

        bblk, cblk, acoef, apow_re, apow_im = _pack_s5(ssm_a_re[l], ssm_a_im[l], ssm_log_dt[l], ssm_b_re[l], ssm_b_im[l],
                                                       ssm_c_re[l], ssm_c_im[l], S5_CHUNK // SCAN_SEGMENTS)
        ys = _s5(u, bblk, cblk, acoef, apow_re, apow_im, ssm_d[l].reshape(1, SSM_WIDTH),
                 ssm_w_glu[l].astype(BF16), ssm_b_glu[l].reshape(1, SSM_WIDTH), batch=batch, seq=seq, tc=S5_CHUNK)

        qg = jnp.broadcast_to(q_norm_g[l].reshape(HEAD_DIM, 1), (HEAD_DIM, tq))
        kg = jnp.zeros((8, HEAD_DIM), F32).at[0:3].set(k_norm_g[l])
        pe, w1 = _pack_compress(cmp_pe_k[l], cmp_pe_v[l], cmp_wk1[l], cmp_wv1[l])
        ynt = _nsa(qt, cp, kp, vt, glt, slopes, qg, kg, pe, w1,
                   cmp_wk2[l].astype(BF16), cmp_wv2[l].T.astype(BF16), batch=batch, seq=seq, tq=tq)

        x2 = _final(x2, ys, ynt, g1, wmg, w_proj_ssm[l].astype(BF16), w_proj_nsa[l].astype(BF16),
                    w_out[l].astype(BF16), norm2_g[l].reshape(1, D_MODEL), w_up[l].astype(BF16),
                    w_down[l].astype(BF16), tm=FINAL_ROWS)
    return x2.reshape(batch, seq, D_MODEL)
```
